```python
import math
import jax
import jax.numpy as jnp
from jax import lax
import numpy as np

D_MODEL = 2048
BATCH = 2
SEQ = 8192
DEPTH = 1
DEC_BATCH = 4
DEC_SEQ = 8192
PAST_LEN = 128

GRID_W = 64
PLE_DIM = 256
HY_WIDTH = D_MODEL // 2
HY_SHORT = 3
FILTER_EMB = 33
FILTER_BANDS = (FILTER_EMB - 1) // 2
FILTER_ORDER = 64
DECAY_FAST = 0.3
DECAY_SLOW = 1.5
DECAY_TARGET = 1e-2
HEAD_DIM = 128
N_HEADS = (D_MODEL // 2) // HEAD_DIM
N_KV_HEADS = N_HEADS // 4
GROUP = N_HEADS // N_KV_HEADS
ROPE_THETA = 10000.0
ROPE_AXIS_DIM = HEAD_DIM // 2
Q_BLOCK = 128
ATTN_WIDTH = N_HEADS * HEAD_DIM
KV_WIDTH = N_KV_HEADS * HEAD_DIM
IN_WIDTH = 3 * HY_WIDTH + ATTN_WIDTH + 2 * KV_WIDTH + 2 * D_MODEL
N_EXPERTS = 32
TOP_K = 4
D_FF = D_MODEL
SWIGLU_LIMIT = 7.0
SWIGLU_ALPHA = 1.702
MOE_BLOCK = 512
EPS = 1e-6

kernel_name = 'hyena_gqa_axial_moe_ple_encoder'

F32 = jnp.float32


def rms_norm(x, g):
    xf = x.astype(F32)
    y = xf * lax.rsqrt(jnp.mean(xf * xf, axis=-1, keepdims=True) + EPS)
    return (y * g.astype(F32)).astype(x.dtype)


def axial_rope_tables(L):
    rows = L // GRID_W
    row = jnp.repeat(jnp.arange(rows, dtype=F32), GRID_W)
    col = jnp.tile(jnp.arange(GRID_W, dtype=F32), rows)
    inv = 1.0 / (ROPE_THETA ** (jnp.arange(0, ROPE_AXIS_DIM, 2, dtype=F32) / ROPE_AXIS_DIM))
    ang = jnp.concatenate([row[:, None] * inv, col[:, None] * inv], axis=-1)
    return jnp.cos(ang), jnp.sin(ang)


def apply_rope(x, cos, sin):
    xf = x.astype(F32)
    half = HEAD_DIM // 2
    x1, x2 = xf[..., :half], xf[..., half:]
    c = cos[None, :, None, :]
    s = sin[None, :, None, :]
    return jnp.concatenate([x1 * c - x2 * s, x2 * c + x1 * s], axis=-1).astype(x.dtype)


def hyena_filter_spectrum(L, w_f1, b_f1, w_f2, b_f2, w_f3, b_f3, w_f4, freq):
    t = jnp.linspace(0.0, 1.0, L, dtype=F32)[:, None]
    w = 2.0 * math.pi * jnp.arange(L, dtype=F32)[:, None] / L
    f = jnp.linspace(1e-4, FILTER_BANDS - 1, FILTER_BANDS, dtype=F32)[None, :]
    z = jnp.concatenate([t, jnp.cos(f * w), -jnp.sin(f * w)], axis=-1)
    fr = freq.astype(F32)
    h = jnp.sin(fr * (z @ w_f1.astype(F32) + b_f1.astype(F32)))
    h = jnp.sin(fr * (h @ w_f2.astype(F32) + b_f2.astype(F32)))
    h = jnp.sin(fr * (h @ w_f3.astype(F32) + b_f3.astype(F32)))
    h = h @ w_f4.astype(F32)
    deltas = jnp.linspace(math.log(DECAY_TARGET) / DECAY_SLOW,
                          math.log(DECAY_TARGET) / DECAY_FAST, HY_WIDTH, dtype=F32)
    decay = jnp.exp(-t * jnp.abs(deltas))
    h_fwd = h[:, :HY_WIDTH] * decay
    h_bwd = h[:, HY_WIDTH:] * decay
    k = jnp.concatenate([h_fwd, jnp.zeros((1, HY_WIDTH), F32), h_bwd[:0:-1]], axis=0)
    return jnp.fft.rfft(k, axis=0)


def hyena_mixer(u, w_short, b_short, k_f, hy_bias):
    L = u.shape[1]
    pad = HY_SHORT // 2
    up = jnp.pad(u, ((0, 0), (pad, pad), (0, 0)))
    uc = b_short
    for j in range(HY_SHORT):
        uc = uc + up[:, j:j + L] * w_short[j]
    x0, x1, v = jnp.split(uc, 3, axis=-1)
    z = v * x1
    zf = z.astype(F32)
    zs = jnp.fft.rfft(zf, n=2 * L, axis=1)
    y = jnp.fft.irfft(zs * k_f[None], n=2 * L, axis=1)[:, :L]
    y = (y + zf * hy_bias.astype(F32)).astype(u.dtype)
    return y * x0


def attention_mixer(q, k, v, g_q, g_k, cos, sin):
    B_, L, _ = q.shape
    q = q.reshape(B_, L, N_HEADS, HEAD_DIM)
    k = k.reshape(B_, L, N_KV_HEADS, HEAD_DIM)
    v = v.reshape(B_, L, N_KV_HEADS, HEAD_DIM)
    q = apply_rope(rms_norm(q, g_q), cos, sin)
    k = apply_rope(rms_norm(k, g_k), cos, sin)
    nb = L // Q_BLOCK
    qb = q.reshape(B_, nb, Q_BLOCK, N_KV_HEADS, GROUP, HEAD_DIM).transpose(1, 0, 3, 4, 2, 5)
    scale = HEAD_DIM ** -0.5

    def block(qi):
        s = jnp.einsum('bkgqd,bskd->bkgqs', qi, k, preferred_element_type=F32) * scale
        pr = jax.nn.softmax(s, axis=-1).astype(v.dtype)
        return jnp.einsum('bkgqs,bskd->bkgqd', pr, v)

    o = lax.map(block, qb)
    return o.transpose(1, 0, 4, 2, 3, 5).reshape(B_, L, ATTN_WIDTH)


def moe_ffn(h, w_router, b_router, w_gate, b_gate, w_up, b_up, w_down, b_down):
    shape = h.shape
    xt = h.reshape(-1, shape[-1])
    n_tok = xt.shape[0]
    n_assign = n_tok * TOP_K
    logits = xt.astype(F32) @ w_router.astype(F32) + b_router.astype(F32)
    top_val, top_idx = lax.top_k(logits, TOP_K)
    top_w = jax.nn.softmax(top_val, axis=-1)
    flat_e = top_idx.reshape(-1)
    order = jnp.argsort(flat_e)
    sorted_e = flat_e[order]
    sorted_tok = (order // TOP_K).astype(jnp.int32)
    sorted_w = top_w.reshape(-1)[order]
    counts = jnp.bincount(flat_e, length=N_EXPERTS)
    padded = (counts + MOE_BLOCK - 1) // MOE_BLOCK * MOE_BLOCK
    pad_end = jnp.cumsum(padded)
    pad_start = pad_end - padded
    grp_start = jnp.cumsum(counts) - counts
    dest = pad_start[sorted_e] + jnp.arange(n_assign, dtype=jnp.int32) - grp_start[sorted_e]
    cap = n_assign + N_EXPERTS * MOE_BLOCK
    n_blk = cap // MOE_BLOCK
    buf_tok = jnp.zeros((cap,), jnp.int32).at[dest].set(sorted_tok)
    buf_w = jnp.zeros((cap,), F32).at[dest].set(sorted_w)
    blk_e = jnp.minimum(jnp.searchsorted(pad_end, jnp.arange(n_blk, dtype=jnp.int32) * MOE_BLOCK,
                                         side='right'), N_EXPERTS - 1)

    def body(acc, blk):
        tok, wt, e = blk
        xb = xt[tok]
        g = xb @ w_gate[e] + b_gate[e]
        u = xb @ w_up[e] + b_up[e]
        g = jnp.minimum(g, SWIGLU_LIMIT)
        u = jnp.clip(u, -SWIGLU_LIMIT, SWIGLU_LIMIT)
        a = g * jax.nn.sigmoid(SWIGLU_ALPHA * g) * (u + 1.0)
        yb = (a @ w_down[e] + b_down[e]) * wt[:, None].astype(xt.dtype)
        return acc.at[tok].add(yb), None

    acc, _ = lax.scan(body, jnp.zeros_like(xt),
                      (buf_tok.reshape(n_blk, MOE_BLOCK), buf_w.reshape(n_blk, MOE_BLOCK), blk_e))
    return acc.reshape(shape)


def trunk(x, p, W):
    L = x.shape[1]
    cos, sin = axial_rope_tables(L)
    splits = [3 * HY_WIDTH, 3 * HY_WIDTH + ATTN_WIDTH,
              3 * HY_WIDTH + ATTN_WIDTH + KV_WIDTH, 3 * HY_WIDTH + ATTN_WIDTH + 2 * KV_WIDTH]
    h = x
    for i in range(DEPTH):
        a = rms_norm(h, W['g_mix'][i])
        proj = a @ W['w_in'][i]
        u_hy, q, k, v, gates = jnp.split(proj, splits, axis=-1)
        k_f = hyena_filter_spectrum(L, W['w_f1'][i], W['b_f1'][i], W['w_f2'][i], W['b_f2'][i],
                                    W['w_f3'][i], W['b_f3'][i], W['w_f4'][i], W['filter_freq'][i])
        y_hy = hyena_mixer(u_hy, W['w_short'][i], W['b_short'][i], k_f, W['hy_bias'][i])
        y_at = attention_mixer(q, k, v, W['g_q'][i], W['g_k'][i], cos, sin)
        g = jax.nn.sigmoid(gates.astype(F32)).astype(h.dtype)
        g_hy, g_at = jnp.split(g, 2, axis=-1)
        mix = g_hy * (y_hy @ W['w_hy_br'][i]) + g_at * (y_at @ W['w_at_br'][i])
        h = h + mix @ W['w_out'][i]
        h = h + moe_ffn(rms_norm(h, W['g_ffn'][i]), W['w_router'][i], W['b_router'][i],
                        W['w_gate'][i], W['b_gate'][i], W['w_up'][i], W['b_up'][i],
                        W['w_down'][i], W['b_down'][i])
        ple_gate = jax.nn.sigmoid((rms_norm(h, W['g_ple'][i]) @ W['w_ple_gate'][i]).astype(F32)).astype(h.dtype)
        h = h + ple_gate * (p[i] @ W['w_ple_proj'][i])
    return rms_norm(h, W['g_final'])


def setup_inputs(seed: int = 0) -> dict:
    key = jax.random.key(seed)
    ks = jax.random.split(key, 40)

    def nrm(i, shape, scale):
        return jax.random.normal(ks[i], shape, F32) * scale

    def gain(i, shape, noise=0.02):
        return 1.0 + nrm(i, shape, noise)

    return {
        'x_prompt': nrm(0, (BATCH, SEQ, D_MODEL), 1.0),
        'x_sample': nrm(1, (DEC_BATCH, DEC_SEQ, D_MODEL), 1.0),
        'p_prompt': nrm(2, (DEPTH, BATCH, SEQ, PLE_DIM), 1.0),
        'p_sample': nrm(3, (DEPTH, DEC_BATCH, DEC_SEQ, PLE_DIM), 1.0),
        'g_mix': gain(4, (DEPTH, D_MODEL)),
        'w_in': nrm(5, (DEPTH, D_MODEL, IN_WIDTH), D_MODEL ** -0.5),
        'w_short': nrm(6, (DEPTH, HY_SHORT, 3 * HY_WIDTH), HY_SHORT ** -0.5),
        'b_short': nrm(7, (DEPTH, 3 * HY_WIDTH), 0.02),
        'w_f1': nrm(8, (DEPTH, FILTER_EMB, FILTER_ORDER), FILTER_EMB ** -0.5),
        'b_f1': nrm(9, (DEPTH, FILTER_ORDER), 0.02),
        'w_f2': nrm(10, (DEPTH, FILTER_ORDER, FILTER_ORDER), FILTER_ORDER ** -0.5),
        'b_f2': nrm(11, (DEPTH, FILTER_ORDER), 0.02),
        'w_f3': nrm(12, (DEPTH, FILTER_ORDER, FILTER_ORDER), FILTER_ORDER ** -0.5),
        'b_f3': nrm(13, (DEPTH, FILTER_ORDER), 0.02),
        'w_f4': nrm(14, (DEPTH, FILTER_ORDER, 2 * HY_WIDTH), 0.01),
        'filter_freq': gain(15, (DEPTH, FILTER_ORDER), 0.05),
        'hy_bias': nrm(16, (DEPTH, HY_WIDTH), 0.5),
        'g_q': gain(17, (DEPTH, HEAD_DIM)),
        'g_k': gain(18, (DEPTH, HEAD_DIM)),
        'w_hy_br': nrm(19, (DEPTH, HY_WIDTH, D_MODEL), HY_WIDTH ** -0.5),
        'w_at_br': nrm(20, (DEPTH, ATTN_WIDTH, D_MODEL), ATTN_WIDTH ** -0.5),
        'w_out': nrm(21, (DEPTH, D_MODEL, D_MODEL), D_MODEL ** -0.5),
        'g_ffn': gain(22, (DEPTH, D_MODEL)),
        'w_router': nrm(23, (DEPTH, D_MODEL, N_EXPERTS), D_MODEL ** -0.5),
        'b_router': nrm(24, (DEPTH, N_EXPERTS), 0.01),
        'w_gate': nrm(25, (DEPTH, N_EXPERTS, D_MODEL, D_FF), D_MODEL ** -0.5),
        'b_gate': nrm(26, (DEPTH, N_EXPERTS, D_FF), 0.02),
        'w_up': nrm(27, (DEPTH, N_EXPERTS, D_MODEL, D_FF), D_MODEL ** -0.5),
        'b_up': nrm(28, (DEPTH, N_EXPERTS, D_FF), 0.02),
        'w_down': nrm(29, (DEPTH, N_EXPERTS, D_FF, D_MODEL), D_FF ** -0.5),
        'b_down': nrm(30, (DEPTH, N_EXPERTS, D_MODEL), 0.02),
        'g_ple': gain(31, (DEPTH, D_MODEL)),
        'w_ple_gate': nrm(32, (DEPTH, D_MODEL, D_MODEL), D_MODEL ** -0.5),
        'w_ple_proj': nrm(33, (DEPTH, PLE_DIM, D_MODEL), PLE_DIM ** -0.5),
        'g_final': gain(34, (D_MODEL,)),
    }


def reference(x_prompt, x_sample, p_prompt, p_sample, g_mix, w_in, w_short, b_short,
              w_f1, b_f1, w_f2, b_f2, w_f3, b_f3, w_f4, filter_freq, hy_bias, g_q, g_k,
              w_hy_br, w_at_br, w_out, g_ffn, w_router, b_router, w_gate, b_gate, w_up, b_up,
              w_down, b_down, g_ple, w_ple_gate, w_ple_proj, g_final):
    W = {
        'g_mix': g_mix, 'w_in': w_in, 'w_short': w_short, 'b_short': b_short,
        'w_f1': w_f1, 'b_f1': b_f1, 'w_f2': w_f2, 'b_f2': b_f2, 'w_f3': w_f3, 'b_f3': b_f3,
        'w_f4': w_f4, 'filter_freq': filter_freq, 'hy_bias': hy_bias, 'g_q': g_q, 'g_k': g_k,
        'w_hy_br': w_hy_br, 'w_at_br': w_at_br, 'w_out': w_out, 'g_ffn': g_ffn,
        'w_router': w_router, 'b_router': b_router, 'w_gate': w_gate, 'b_gate': b_gate,
        'w_up': w_up, 'b_up': b_up, 'w_down': w_down, 'b_down': b_down,
        'g_ple': g_ple, 'w_ple_gate': w_ple_gate, 'w_ple_proj': w_ple_proj, 'g_final': g_final,
    }
    y_prompt = trunk(x_prompt, p_prompt, W)
    y_sample = trunk(x_sample, p_sample, W)
    return (y_prompt, y_sample)
```

```python
import functools
import math

import jax
import jax.numpy as jnp
import numpy as np
from jax import lax
from jax.experimental import pallas as pl
from jax.experimental.pallas import tpu as pltpu

F32 = jnp.float32
BF16 = jnp.bfloat16
I32 = jnp.int32

EPS = 1e-6
HEAD_DIM = 128
GROUP = 4
GRID_W = 64
ROPE_THETA = 10000.0
HY_SHORT = 3
FILTER_EMB = 33
FILTER_BANDS = 16
DECAY_FAST = 0.3
DECAY_SLOW = 1.5
DECAY_TARGET = 1e-2
N_EXPERTS = 32
TOP_K = 4
SWIGLU_LIMIT = 7.0
SWIGLU_ALPHA = 1.702

LANES = 128
FFT_R = 128
EXPERT_BLOCK = 512
ROUTE_BLOCK = 256
VMEM_LIMIT = 56 * 1024 * 1024


def _cparams(sem, vmem=VMEM_LIMIT):
    return pltpu.CompilerParams(dimension_semantics=sem, vmem_limit_bytes=vmem)


def _rms(x, g):
    return x * lax.rsqrt(jnp.mean(x * x, axis=-1, keepdims=True) + EPS) * g


def _inproj_body(x_ref, g_ref, w_ref, o_ref, a_scr):
    @pl.when(pl.program_id(1) == 0)
    def _():
        a_scr[...] = _rms(x_ref[...], g_ref[...]).astype(BF16)

    o_ref[...] = jnp.dot(a_scr[...], w_ref[...], preferred_element_type=F32).astype(o_ref.dtype)


def in_proj(x2d, g, w_bf16, tm=1024, tn=512):
    n, d = x2d.shape
    nout = w_bf16.shape[1]
    return pl.pallas_call(
        _inproj_body,
        grid=(n // tm, nout // tn),
        in_specs=[
            pl.BlockSpec((tm, d), lambda i, j: (i, 0)),
            pl.BlockSpec((1, d), lambda i, j: (0, 0)),
            pl.BlockSpec((d, tn), lambda i, j: (0, j)),
        ],
        out_specs=pl.BlockSpec((tm, tn), lambda i, j: (i, j)),
        out_shape=jax.ShapeDtypeStruct((n, nout), BF16),
        scratch_shapes=[pltpu.VMEM((tm, d), BF16)],
        compiler_params=_cparams(("parallel", "arbitrary")),
        name="in_proj",
    )(x2d, g, w_bf16)


def _hyprep_body(u_ref, up_ref, un_ref, w_ref, b_ref, z_ref, x0_ref, *, tr, c):
    r = pl.program_id(1)
    nr = pl.num_programs(1)
    u = u_ref[0].astype(F32)
    hp = up_ref[0][15:16, :].astype(F32)
    hn = un_ref[0][0:1, :].astype(F32)
    hp = jnp.where(r == 0, 0.0, hp)
    hn = jnp.where(r == nr - 1, 0.0, hn)
    row = lax.broadcasted_iota(I32, u.shape, 0)
    prev = jnp.where(row == 0, hp, pltpu.roll(u, 1, axis=0))
    nxt = jnp.where(row == tr - 1, hn, pltpu.roll(u, tr - 1, axis=0))
    w = w_ref[...]
    uc = b_ref[...] + prev * w[0:1] + u * w[1:2] + nxt * w[2:3]
    x0 = uc[:, :c]
    x1 = uc[:, c:2 * c]
    v = uc[:, 2 * c:]
    z_ref[0] = (v * x1).astype(z_ref.dtype)
    x0_ref[0] = x0.astype(x0_ref.dtype)


def hy_prep(proj3, w_short, b_short, c, tr=256):
    b, l, _ = proj3.shape
    hb = tr // 16
    nh = l // 16
    body = functools.partial(_hyprep_body, tr=tr, c=c)
    return pl.pallas_call(
        body,
        grid=(b, l // tr),
        in_specs=[
            pl.BlockSpec((1, tr, 3 * c), lambda i, r: (i, r, 0)),
            pl.BlockSpec((1, 16, 3 * c), lambda i, r: (i, jnp.maximum(r * hb - 1, 0), 0)),
            pl.BlockSpec((1, 16, 3 * c), lambda i, r: (i, jnp.minimum((r + 1) * hb, nh - 1), 0)),
            pl.BlockSpec((HY_SHORT, 3 * c), lambda i, r: (0, 0)),
            pl.BlockSpec((1, 3 * c), lambda i, r: (0, 0)),
        ],
        out_specs=[
            pl.BlockSpec((1, tr, c), lambda i, r: (i, r, 0)),
            pl.BlockSpec((1, tr, c), lambda i, r: (i, r, 0)),
        ],
        out_shape=[jax.ShapeDtypeStruct((b, l, c), BF16), jax.ShapeDtypeStruct((b, l, c), BF16)],
        compiler_params=_cparams(("parallel", "parallel")),
        name="hy_prep",
    )(proj3, proj3, proj3, w_short, b_short)


def _hyfilter_body(z_ref, w1_ref, b1_ref, w2_ref, b2_ref, w3_ref, b3_ref, fr_ref, w4_ref, dl_ref, o_ref):
    hi = lax.Precision.HIGHEST
    zb = z_ref[...]
    fr = fr_ref[...]
    h = jnp.sin(fr * (jnp.dot(zb, w1_ref[...], precision=hi, preferred_element_type=F32) + b1_ref[...]))
    h = jnp.sin(fr * (jnp.dot(h, w2_ref[...], precision=hi, preferred_element_type=F32) + b2_ref[...]))
    h = jnp.sin(fr * (jnp.dot(h, w3_ref[...], precision=hi, preferred_element_type=F32) + b3_ref[...]))
    h4 = jnp.dot(h, w4_ref[...], precision=hi, preferred_element_type=F32)
    t = zb[:, 0:1]
    mask = zb[:, FILTER_EMB:FILTER_EMB + 1]
    o_ref[...] = h4 * jnp.exp(-t * dl_ref[...]) * mask


def hy_filter(zfeat, w1p, b1, w2, b2, w3, b3, freq, w4, absdelta, c, tr=1024):
    rows = zfeat.shape[0]
    half_blocks = rows // 2 // tr
    fo = w2.shape[0]
    full = lambda i: (0, 0)
    return pl.pallas_call(
        _hyfilter_body,
        grid=(rows // tr,),
        in_specs=[
            pl.BlockSpec((tr, LANES), lambda i: (i, 0)),
            pl.BlockSpec((LANES, fo), full),
            pl.BlockSpec((1, fo), full),
            pl.BlockSpec((fo, fo), full),
            pl.BlockSpec((1, fo), full),
            pl.BlockSpec((fo, fo), full),
            pl.BlockSpec((1, fo), full),
            pl.BlockSpec((1, fo), full),
            pl.BlockSpec((fo, c), lambda i: (0, i // half_blocks)),
            pl.BlockSpec((1, c), full),
        ],
        out_specs=pl.BlockSpec((tr, c), lambda i: (i, 0)),
        out_shape=jax.ShapeDtypeStruct((rows, c), F32),
        compiler_params=_cparams(("parallel",)),
        name="hy_filter",
    )(zfeat, w1p, b1, w2, b2, w3, b3, freq, w4, absdelta)


def _dft_constants():
    r = FFT_R
    n = r * r
    k = np.arange(r)
    ang = -2.0 * np.pi * np.outer(k, k) / r
    fr, fi = np.cos(ang), np.sin(ang)
    m1 = np.concatenate([fr, fi], axis=0)
    tw = -2.0 * np.pi * np.outer(k, k) / n
    m2 = np.concatenate([fr, fi], axis=1)
    gr, gi = fr, -fi
    g = np.block([[gr, gi], [-gi, gr]])
    h2 = np.concatenate([gr[: r // 2], gi[: r // 2]], axis=0) / n
    return dict(
        m1=jnp.asarray(m1, BF16), twr=jnp.asarray(np.cos(tw), F32), twi=jnp.asarray(np.sin(tw), F32),
        m2=jnp.asarray(m2, BF16), g=jnp.asarray(g, BF16), h2=jnp.asarray(h2, BF16))


CH_HALF = 64
CH_CHUNK = 8


def _dft_forward(zs, s1, m1_ref, twr, twi, m2_ref, half, n1_rows, emit):
    r = FFT_R

    def stage1(c, carry):
        zc = zs[pl.ds(pl.multiple_of((half * CH_HALF + c) * n1_rows, n1_rows), n1_rows), :].astype(BF16)
        a = jnp.dot(m1_ref[:, :n1_rows], zc, preferred_element_type=F32)
        ar, ai = a[:r], a[r:]
        s1[pl.ds(pl.multiple_of(c * 2 * r, 2 * r), r), :] = (ar * twr - ai * twi).astype(BF16)
        s1[pl.ds(pl.multiple_of(c * 2 * r + r, r), r), :] = (ar * twi + ai * twr).astype(BF16)
        return carry

    lax.fori_loop(0, CH_HALF, stage1, 0)

    def stage2(j, carry):
        rows = CH_CHUNK * 2 * r
        lhs = s1[pl.ds(pl.multiple_of(j * rows, rows), rows), :]
        o = jnp.dot(lhs, m2_ref[...], preferred_element_type=F32).reshape(CH_CHUNK, 2 * r, 2 * r)
        xr = o[:, :r, :r] - o[:, r:, r:]
        xi = o[:, :r, r:] + o[:, r:, :r]
        emit(j, xr, xi)
        return carry

    lax.fori_loop(0, CH_HALF // CH_CHUNK, stage2, 0)


def _hyspec_body(k_ref, m1_ref, twr_ref, twi_ref, m2_ref, o_ref, zs, s1):
    r = FFT_R
    for n1 in range(r):
        blk = k_ref[n1 * r:(n1 + 1) * r, :]
        zs[pl.ds(n1, LANES, stride=r), :] = blk.T
    twr = twr_ref[...]
    twi = twi_ref[...]
    for half in range(LANES // CH_HALF):
        def emit(j, xr, xi, half=half):
            c0 = pl.multiple_of(half * CH_HALF + j * CH_CHUNK, CH_CHUNK)
            o_ref[pl.ds(c0, CH_CHUNK)] = jnp.concatenate([xr, xi], axis=-1).astype(o_ref.dtype)

        _dft_forward(zs, s1, m1_ref, twr, twi, m2_ref, half, r, emit)


def hy_spectrum(kt, consts):
    rows, c = kt.shape
    r = FFT_R
    full2 = lambda i: (0, 0)
    return pl.pallas_call(
        _hyspec_body,
        grid=(c // LANES,),
        in_specs=[
            pl.BlockSpec((rows, LANES), lambda i: (0, i)),
            pl.BlockSpec((2 * r, r), full2),
            pl.BlockSpec((r, r), full2),
            pl.BlockSpec((r, r), full2),
            pl.BlockSpec((r, 2 * r), full2),
        ],
        out_specs=pl.BlockSpec((LANES, r, 2 * r), lambda i: (i, 0, 0)),
        out_shape=jax.ShapeDtypeStruct((c, r, 2 * r), BF16),
        scratch_shapes=[pltpu.VMEM((LANES * r, r), F32), pltpu.VMEM((CH_HALF * 2 * r, r), BF16)],
        compiler_params=_cparams(("parallel",)),
        name="hy_spectrum",
    )(kt, consts["m1"], consts["twr"], consts["twi"], consts["m2"])


def _hyconv_body(z_ref, x0_ref, kf_ref, m1_ref, twr_ref, twi_ref, m2_ref, g_ref, h2_ref, bias_ref,
                 o_ref, zs, s1, s2):
    r = FFT_R
    n1_rows = r // 2
    for n1 in range(n1_rows):
        blk = z_ref[0, n1 * r:(n1 + 1) * r, :].astype(F32)
        zs[pl.ds(n1, LANES, stride=n1_rows), :] = blk.T
    twr = twr_ref[...]
    twi = twi_ref[...]
    for half in range(LANES // CH_HALF):
        def emit(j, xr, xi, half=half):
            c0 = pl.multiple_of(half * CH_HALF + j * CH_CHUNK, CH_CHUNK)
            kf = kf_ref[pl.ds(c0, CH_CHUNK)].astype(F32)
            kr, ki = kf[..., :r], kf[..., r:]
            y = jnp.concatenate([xr * kr - xi * ki, xr * ki + xi * kr], axis=-1).astype(BF16)
            rows = CH_CHUNK * r
            s2[pl.ds(pl.multiple_of(j * rows, rows), rows), :] = y.reshape(rows, 2 * r)

        _dft_forward(zs, s1, m1_ref, twr, twi, m2_ref, half, n1_rows, emit)

        def inv1(j, carry):
            rows = CH_CHUNK * r
            sl = pl.ds(pl.multiple_of(j * rows, rows), rows)
            bm = jnp.dot(s2[sl, :], g_ref[...], preferred_element_type=F32).reshape(CH_CHUNK, r, 2 * r)
            br, bi = bm[..., :r], bm[..., r:]
            b2 = jnp.concatenate([br * twr + bi * twi, bi * twr - br * twi], axis=-1).astype(BF16)
            s2[sl, :] = b2.reshape(rows, 2 * r)
            return carry

        lax.fori_loop(0, CH_HALF // CH_CHUNK, inv1, 0)

        def inv2(c, carry, half=half):
            q = jnp.dot(h2_ref[...], s2[pl.ds(pl.multiple_of(c * r, r), r), :], preferred_element_type=F32)
            yc = q[:n1_rows, :r] - q[n1_rows:, r:]
            zs[pl.ds(pl.multiple_of((half * CH_HALF + c) * n1_rows, n1_rows), n1_rows), :] = yc
            return carry

        lax.fori_loop(0, CH_HALF, inv2, 0)

    bias = bias_ref[...]
    for n1 in range(n1_rows):
        y = zs[pl.ds(n1, LANES, stride=n1_rows), :].T
        sl = slice(n1 * r, (n1 + 1) * r)
        zz = z_ref[0, sl, :].astype(F32)
        x0 = x0_ref[0, sl, :].astype(F32)
        o_ref[0, sl, :] = ((y + zz * bias) * x0).astype(o_ref.dtype)


def hy_conv(z, x0c, kf, hy_bias, consts):
    b, l, c = z.shape
    r = FFT_R
    full2 = lambda i, j: (0, 0)
    return pl.pallas_call(
        _hyconv_body,
        grid=(c // LANES, b),
        in_specs=[
            pl.BlockSpec((1, l, LANES), lambda j, i: (i, 0, j)),
            pl.BlockSpec((1, l, LANES), lambda j, i: (i, 0, j)),
            pl.BlockSpec((LANES, r, 2 * r), lambda j, i: (j, 0, 0)),
            pl.BlockSpec((2 * r, r), full2),
            pl.BlockSpec((r, r), full2),
            pl.BlockSpec((r, r), full2),
            pl.BlockSpec((r, 2 * r), full2),
            pl.BlockSpec((2 * r, 2 * r), full2),
            pl.BlockSpec((r, r), full2),
            pl.BlockSpec((1, LANES), lambda j, i: (0, j)),
        ],
        out_specs=pl.BlockSpec((1, l, LANES), lambda j, i: (i, 0, j)),
        out_shape=jax.ShapeDtypeStruct((b, l, c), BF16),
        scratch_shapes=[
            pltpu.VMEM((LANES * r // 2, r), F32),
            pltpu.VMEM((CH_HALF * 2 * r, r), BF16),
            pltpu.VMEM((CH_HALF * r, 2 * r), BF16),
        ],
        compiler_params=_cparams(("parallel", "parallel")),
        name="hy_conv",
    )(z, x0c, kf, consts["m1"], consts["twr"], consts["twi"], consts["m2"], consts["g"], consts["h2"], hy_bias)


def _rope(x, c, s):
    return x * c + pltpu.roll(x, HEAD_DIM // 2, axis=1) * s


def _attn_body(q_ref, k_ref, v_ref, cos_ref, sin_ref, gq_ref, gk_ref, o_ref,
               kt_scr, q_scr, m_scr, l_scr, acc_scr, *, tq, tk, l):
    qi = pl.program_id(2)
    nk = l // tk

    @pl.when(qi == 0)
    def _():
        for r in range(nk):
            sl = slice(r * tk, (r + 1) * tk)
            kn = _rms(k_ref[0, sl, :].astype(F32), gk_ref[...])
            kt_scr[r] = _rope(kn, cos_ref[sl, :], sin_ref[sl, :]).T.astype(BF16)

    row0 = pl.multiple_of(qi * tq, tq)
    c = cos_ref[pl.ds(row0, tq), :]
    s = sin_ref[pl.ds(row0, tq), :]
    scale = HEAD_DIM ** -0.5
    for g in range(GROUP):
        qn = _rms(q_ref[0, :, g * HEAD_DIM:(g + 1) * HEAD_DIM].astype(F32), gq_ref[...])
        q_scr[g * tq:(g + 1) * tq, :] = (_rope(qn, c, s) * scale).astype(BF16)

    m_scr[...] = jnp.full(m_scr.shape, -jnp.inf, F32)
    l_scr[...] = jnp.zeros(l_scr.shape, F32)
    acc_scr[...] = jnp.zeros(acc_scr.shape, F32)

    def kv_step(j, carry):
        sc = jnp.dot(q_scr[...], kt_scr[j], preferred_element_type=F32)
        m_prev = m_scr[...]
        m_new = jnp.maximum(m_prev, jnp.max(sc, axis=-1, keepdims=True))
        alpha = jnp.exp(m_prev - m_new)
        p = jnp.exp(sc - m_new)
        l_scr[...] = alpha * l_scr[...] + jnp.sum(p, axis=-1, keepdims=True)
        vv = v_ref[0, pl.ds(pl.multiple_of(j * tk, tk), tk), :]
        acc_scr[...] = alpha * acc_scr[...] + jnp.dot(p.astype(BF16), vv, preferred_element_type=F32)
        m_scr[...] = m_new
        return carry

    lax.fori_loop(0, nk, kv_step, 0)
    o = acc_scr[...] / l_scr[...]
    for g in range(GROUP):
        o_ref[0, :, g * HEAD_DIM:(g + 1) * HEAD_DIM] = o[g * tq:(g + 1) * tq].astype(o_ref.dtype)


def attention(proj3, cos2, sin2, g_q, g_k, q_col, k_col, v_col, n_kv, tq=256, tk=512):
    b, l, _ = proj3.shape
    gw = GROUP * HEAD_DIM
    body = functools.partial(_attn_body, tq=tq, tk=tk, l=l)
    qb, kb, vb = q_col // gw, k_col // HEAD_DIM, v_col // HEAD_DIM
    return pl.pallas_call(
        body,
        grid=(b, n_kv, l // tq),
        in_specs=[
            pl.BlockSpec((1, tq, gw), lambda i, h, q: (i, q, qb + h)),
            pl.BlockSpec((1, l, HEAD_DIM), lambda i, h, q: (i, 0, kb + h)),
            pl.BlockSpec((1, l, HEAD_DIM), lambda i, h, q: (i, 0, vb + h)),
            pl.BlockSpec((l, HEAD_DIM), lambda i, h, q: (0, 0)),
            pl.BlockSpec((l, HEAD_DIM), lambda i, h, q: (0, 0)),
            pl.BlockSpec((1, HEAD_DIM), lambda i, h, q: (0, 0)),
            pl.BlockSpec((1, HEAD_DIM), lambda i, h, q: (0, 0)),
        ],
        out_specs=pl.BlockSpec((1, tq, gw), lambda i, h, q: (i, q, h)),
        out_shape=jax.ShapeDtypeStruct((b, l, n_kv * gw), BF16),
        scratch_shapes=[
            pltpu.VMEM((l // tk, HEAD_DIM, tk), BF16),
            pltpu.VMEM((GROUP * tq, HEAD_DIM), BF16),
            pltpu.VMEM((GROUP * tq, 1), F32),
            pltpu.VMEM((GROUP * tq, 1), F32),
            pltpu.VMEM((GROUP * tq, HEAD_DIM), F32),
        ],
        compiler_params=_cparams(("parallel", "parallel", "arbitrary")),
        name="attention",
    )(proj3, proj3, proj3, cos2, sin2, g_q, g_k)


def _merge_body(yh_ref, ya_ref, gh_ref, ga_ref, h_ref, whb_ref, wab_ref, wo_ref, gf_ref, wr_ref, br_ref,
                h1_ref, xn_ref, meta_ref, tw_ref, cnt_ref, *, tm):
    a = jnp.dot(yh_ref[...], whb_ref[...], preferred_element_type=F32)
    bmat = jnp.dot(ya_ref[...], wab_ref[...], preferred_element_type=F32)
    mix = jax.nn.sigmoid(gh_ref[...].astype(F32)) * a + jax.nn.sigmoid(ga_ref[...].astype(F32)) * bmat
    h1 = h_ref[...] + jnp.dot(mix.astype(BF16), wo_ref[...], preferred_element_type=F32)
    h1_ref[...] = h1
    xn = _rms(h1, gf_ref[...])
    xn_ref[...] = xn.astype(xn_ref.dtype)

    logits = jnp.dot(xn, wr_ref[...], precision=lax.Precision.HIGHEST,
                     preferred_element_type=F32) + br_ref[...]
    lane = lax.broadcasted_iota(I32, logits.shape, 1)
    work = logits
    vals, idxs = [], []
    sel = jnp.zeros(logits.shape, F32)
    for _ in range(TOP_K):
        m = jnp.max(work, axis=-1, keepdims=True)
        idx = jnp.min(jnp.where(work == m, lane, LANES), axis=-1, keepdims=True)
        hit = lane == idx
        vals.append(m)
        idxs.append(idx)
        sel = sel + hit.astype(F32)
        work = jnp.where(hit, -jnp.inf, work)
    ex = [jnp.exp(v - vals[0]) for v in vals]
    den = ex[0] + ex[1] + ex[2] + ex[3]

    rr = lax.broadcasted_iota(I32, (tm, tm), 0)
    cc = lax.broadcasted_iota(I32, (tm, tm), 1)
    tri = (cc < rr).astype(BF16)
    prefix = jnp.dot(tri, sel.astype(BF16), preferred_element_type=F32)
    cnt_ref[...] = jnp.broadcast_to(jnp.sum(sel, axis=0, keepdims=True), cnt_ref.shape)

    meta = jnp.zeros(logits.shape, I32)
    tw = jnp.zeros(logits.shape, F32)
    for k in range(TOP_K):
        rank = jnp.sum(jnp.where(lane == idxs[k], prefix, 0.0), axis=-1, keepdims=True)
        meta = jnp.where(lane == k, rank.astype(I32), meta)
        meta = jnp.where(lane == TOP_K + k, idxs[k], meta)
        tw = jnp.where(lane == k, ex[k] / den, tw)
    meta_ref[...] = meta
    tw_ref[...] = tw


def merge_router(y_hy, y_at, proj, x2d, whb, wab, wo, g_ffn, wr_pad, br_pad, gate_col, tm):
    n, d = x2d.shape
    ch = y_hy.shape[1]
    ca = y_at.shape[1]
    gb = gate_col // d
    body = functools.partial(_merge_body, tm=tm)
    full = lambda i: (0, 0)
    row = lambda i: (i, 0)
    return pl.pallas_call(
        body,
        grid=(n // tm,),
        in_specs=[
            pl.BlockSpec((tm, ch), row),
            pl.BlockSpec((tm, ca), row),
            pl.BlockSpec((tm, d), lambda i: (i, gb)),
            pl.BlockSpec((tm, d), lambda i: (i, gb + 1)),
            pl.BlockSpec((tm, d), row),
            pl.BlockSpec((ch, d), full),
            pl.BlockSpec((ca, d), full),
            pl.BlockSpec((d, d), full),
            pl.BlockSpec((1, d), full),
            pl.BlockSpec((d, LANES), full),
            pl.BlockSpec((1, LANES), full),
        ],
        out_specs=[
            pl.BlockSpec((tm, d), row),
            pl.BlockSpec((tm, d), row),
            pl.BlockSpec((tm, LANES), row),
            pl.BlockSpec((tm, LANES), row),
            pl.BlockSpec((8, LANES), row),
        ],
        out_shape=[
            jax.ShapeDtypeStruct((n, d), F32),
            jax.ShapeDtypeStruct((n, d), BF16),
            jax.ShapeDtypeStruct((n, LANES), I32),
            jax.ShapeDtypeStruct((n, LANES), F32),
            jax.ShapeDtypeStruct((n // tm * 8, LANES), F32),
        ],
        compiler_params=_cparams(("parallel",)),
        name="merge_router",
    )(y_hy, y_at, proj, proj, x2d, whb, wab, wo, g_ffn, wr_pad, br_pad)


RUN_ALIGN = 8
RUN_BITS = ROUTE_BLOCK.bit_length()
SORT_ROWS = TOP_K * ROUTE_BLOCK + N_EXPERTS * RUN_ALIGN


def _run_copies(i, cnt_ref, dst_ref, make_copy, op):
    def per_expert(e, local):
        c = cnt_ref[i * N_EXPERTS + e]
        d = dst_ref[i * N_EXPERTS + e]
        for bit in range(RUN_ALIGN.bit_length() - 1, RUN_BITS):
            size = 1 << bit

            @pl.when((c & size) != 0)
            def _(size=size):
                done = c & (size - 1)
                op(make_copy(pl.multiple_of(local + done, RUN_ALIGN), pl.multiple_of(d + done, RUN_ALIGN), size))

        return local + c

    lax.fori_loop(0, N_EXPERTS, per_expert, 0)


def _dispatch_body(cnt_ref, dst_ref, pend_ref, padded_ref, nused_ref, slot_ref, xn_ref, xs_hbm,
                   sort_scr, zero_scr, sem, zsem, *, tm, nblk):
    i = pl.program_id(0)

    @pl.when(i == 0)
    def _():
        zero_scr[...] = jnp.zeros(zero_scr.shape, F32)

        def zero_block(start):
            cp = pltpu.make_async_copy(zero_scr, xs_hbm.at[pl.ds(start, EXPERT_BLOCK), :], zsem)
            cp.start()
            cp.wait()

        def pad_rows(e, carry):
            @pl.when(padded_ref[e] > 0)
            def _():
                zero_block(pl.multiple_of(pend_ref[e] - EXPERT_BLOCK, EXPERT_BLOCK))

            return carry

        lax.fori_loop(0, N_EXPERTS, pad_rows, 0)

        def tail(bk, carry):
            zero_block(pl.multiple_of(bk * EXPERT_BLOCK, EXPERT_BLOCK))
            return carry

        lax.fori_loop(nused_ref[0], nblk, tail, 0)

    slots = slot_ref[0]
    j = lax.broadcasted_iota(I32, (SORT_ROWS, tm), 0)
    perm = jnp.zeros((SORT_ROWS, tm), F32)
    for k in range(TOP_K):
        perm = perm + (j == slots[k:k + 1, :]).astype(F32)
    sort_scr[...] = jnp.dot(perm.astype(BF16), xn_ref[...], preferred_element_type=F32)

    def make_copy(local, d, size):
        return pltpu.make_async_copy(sort_scr.at[pl.ds(local, size), :], xs_hbm.at[pl.ds(d, size), :], sem)

    _run_copies(i, cnt_ref, dst_ref, make_copy, lambda cp: cp.start())
    _run_copies(i, cnt_ref, dst_ref, make_copy, lambda cp: cp.wait())


def dispatch(xn, slots_t, cnt_flat, dst_flat, pad_end, padded, nused, cap, tm):
    n, d = xn.shape
    nblk = cap // EXPERT_BLOCK
    body = functools.partial(_dispatch_body, tm=tm, nblk=nblk)
    return pl.pallas_call(
        body,
        grid_spec=pltpu.PrefetchScalarGridSpec(
            num_scalar_prefetch=5,
            grid=(n // tm,),
            in_specs=[
                pl.BlockSpec((1, 8, tm), lambda i, *_: (i, 0, 0)),
                pl.BlockSpec((tm, d), lambda i, *_: (i, 0)),
            ],
            out_specs=pl.BlockSpec(memory_space=pl.ANY),
            scratch_shapes=[
                pltpu.VMEM((SORT_ROWS, d), F32),
                pltpu.VMEM((EXPERT_BLOCK, d), F32),
                pltpu.SemaphoreType.DMA(()),
                pltpu.SemaphoreType.DMA(()),
            ],
        ),
        out_shape=jax.ShapeDtypeStruct((cap, d), F32),
        compiler_params=_cparams(("arbitrary",)),
        name="dispatch",
    )(cnt_flat, dst_flat, pad_end, padded, nused, slots_t, xn)


def _experts_body(be_ref, nu_ref, x_ref, wg_ref, bg_ref, wu_ref, bu_ref, wd_ref, bd_ref, o_ref, xb_scr, acc_scr):
    i = pl.program_id(0)
    f = pl.program_id(1)
    nf = pl.num_programs(1)
    used = i < nu_ref[0]

    @pl.when(jnp.logical_and(used, f == 0))
    def _():
        xb_scr[...] = x_ref[...].astype(BF16)
        acc_scr[...] = jnp.zeros(acc_scr.shape, F32)

    @pl.when(used)
    def _():
        xb = xb_scr[...]
        g = jnp.dot(xb, wg_ref[0], preferred_element_type=F32) + bg_ref[0]
        u = jnp.dot(xb, wu_ref[0], preferred_element_type=F32) + bu_ref[0]
        g = jnp.minimum(g, SWIGLU_LIMIT)
        u = jnp.clip(u, -SWIGLU_LIMIT, SWIGLU_LIMIT)
        a = g * jax.nn.sigmoid(SWIGLU_ALPHA * g) * (u + 1.0)
        acc_scr[...] += jnp.dot(a.astype(BF16), wd_ref[0], preferred_element_type=F32)

    @pl.when(f == nf - 1)
    def _():
        o_ref[...] = jnp.where(used, acc_scr[...] + bd_ref[0], 0.0)


def experts(xs, blk_e, nused, wg, bg, wu, bu, wd, bd, tf=1024):
    cap, d = xs.shape
    dff = wg.shape[2]
    nblk = cap // EXPERT_BLOCK
    nf = dff // tf

    def eidx(i, be, nu):
        return be[jnp.minimum(i, nu[0] - 1)]

    def fidx(i, f, nu):
        return jnp.where(i < nu[0], f, nf - 1)

    return pl.pallas_call(
        _experts_body,
        grid_spec=pltpu.PrefetchScalarGridSpec(
            num_scalar_prefetch=2,
            grid=(nblk, nf),
            in_specs=[
                pl.BlockSpec((EXPERT_BLOCK, d), lambda i, f, be, nu: (jnp.minimum(i, nu[0] - 1), 0)),
                pl.BlockSpec((1, d, tf), lambda i, f, be, nu: (eidx(i, be, nu), 0, fidx(i, f, nu))),
                pl.BlockSpec((1, 1, tf), lambda i, f, be, nu: (eidx(i, be, nu), 0, fidx(i, f, nu))),
                pl.BlockSpec((1, d, tf), lambda i, f, be, nu: (eidx(i, be, nu), 0, fidx(i, f, nu))),
                pl.BlockSpec((1, 1, tf), lambda i, f, be, nu: (eidx(i, be, nu), 0, fidx(i, f, nu))),
                pl.BlockSpec((1, tf, d), lambda i, f, be, nu: (eidx(i, be, nu), fidx(i, f, nu), 0)),
                pl.BlockSpec((1, 1, d), lambda i, f, be, nu: (eidx(i, be, nu), 0, 0)),
            ],
            out_specs=pl.BlockSpec((EXPERT_BLOCK, d), lambda i, f, be, nu: (i, 0)),
            scratch_shapes=[pltpu.VMEM((EXPERT_BLOCK, d), BF16), pltpu.VMEM((EXPERT_BLOCK, d), F32)],
        ),
        out_shape=jax.ShapeDtypeStruct((cap, d), F32),
        compiler_params=_cparams(("arbitrary", "arbitrary")),
        name="experts",
    )(blk_e, nused, xs, wg, bg, wu, bu, wd, bd)


def _combine_body(cnt_ref, dst_ref, slot_ref, h_ref, tw_ref, p_ref, y_hbm, wpg_ref, wpp_ref, gp_ref, gfin_ref,
                  o_ref, gath, sem, *, tm):
    i = pl.program_id(0)

    gath[TOP_K * tm:, :] = jnp.zeros((SORT_ROWS - TOP_K * tm, gath.shape[1]), F32)

    def make_copy(local, d, size):
        return pltpu.make_async_copy(y_hbm.at[pl.ds(d, size), :], gath.at[pl.ds(local, size), :], sem)

    _run_copies(i, cnt_ref, dst_ref, make_copy, lambda cp: cp.start())
    _run_copies(i, cnt_ref, dst_ref, make_copy, lambda cp: cp.wait())

    tw = tw_ref[...]
    slots = slot_ref[...]
    lane = lax.broadcasted_iota(I32, (tm, SORT_ROWS), 1)
    wmat = jnp.zeros((tm, SORT_ROWS), F32)
    for k in range(TOP_K):
        wmat = wmat + jnp.where(lane == slots[:, k:k + 1], tw[:, k:k + 1], 0.0)
    w_hi = wmat.astype(BF16)
    w_lo = (wmat - w_hi.astype(F32)).astype(BF16)
    rows = gath[...].astype(BF16)
    moe = jnp.dot(w_hi, rows, preferred_element_type=F32) + jnp.dot(w_lo, rows, preferred_element_type=F32)
    h2 = h_ref[...] + moe
    a = _rms(h2, gp_ref[...]).astype(BF16)
    gate = jax.nn.sigmoid(jnp.dot(a, wpg_ref[...], preferred_element_type=F32))
    pp = jnp.dot(p_ref[...].astype(BF16), wpp_ref[...], preferred_element_type=F32)
    h3 = h2 + gate * pp
    o_ref[...] = _rms(h3, gfin_ref[...])


def combine_ple(h1, slots, tw, p2d, y, cnt_flat, dst_flat, wpg, wpp, g_ple, g_final, tm):
    n, d = h1.shape
    pd = p2d.shape[1]
    body = functools.partial(_combine_body, tm=tm)
    full = lambda i, *_: (0, 0)
    row = lambda i, *_: (i, 0)
    return pl.pallas_call(
        body,
        grid_spec=pltpu.PrefetchScalarGridSpec(
            num_scalar_prefetch=2,
            grid=(n // tm,),
            in_specs=[
                pl.BlockSpec((tm, LANES), row),
                pl.BlockSpec((tm, d), row),
                pl.BlockSpec((tm, LANES), row),
                pl.BlockSpec((tm, pd), row),
                pl.BlockSpec(memory_space=pl.ANY),
                pl.BlockSpec((d, d), full),
                pl.BlockSpec((pd, d), full),
                pl.BlockSpec((1, d), full),
                pl.BlockSpec((1, d), full),
            ],
            out_specs=pl.BlockSpec((tm, d), row),
            scratch_shapes=[pltpu.VMEM((SORT_ROWS, d), F32), pltpu.SemaphoreType.DMA(())],
        ),
        out_shape=jax.ShapeDtypeStruct((n, d), F32),
        compiler_params=_cparams(("arbitrary",)),
        name="combine_ple",
    )(cnt_flat, dst_flat, slots, h1, tw, p2d, y, wpg, wpp, g_ple, g_final)


def _rope_tables(l):
    rows = l // GRID_W
    row = jnp.repeat(jnp.arange(rows, dtype=F32), GRID_W)
    col = jnp.tile(jnp.arange(GRID_W, dtype=F32), rows)
    axis_dim = HEAD_DIM // 2
    inv = 1.0 / (ROPE_THETA ** (jnp.arange(0, axis_dim, 2, dtype=F32) / axis_dim))
    ang = jnp.concatenate([row[:, None] * inv, col[:, None] * inv], axis=-1)
    c, s = jnp.cos(ang), jnp.sin(ang)
    return jnp.concatenate([c, c], axis=-1), jnp.concatenate([-s, s], axis=-1)


def _filter_features(l):
    t = jnp.linspace(0.0, 1.0, l, dtype=F32)[:, None]
    w = 2.0 * math.pi * jnp.arange(l, dtype=F32)[:, None] / l
    f = jnp.linspace(1e-4, FILTER_BANDS - 1, FILTER_BANDS, dtype=F32)[None, :]
    z = jnp.concatenate([t, jnp.cos(f * w), -jnp.sin(f * w)], axis=-1)
    pos = jnp.concatenate([jnp.arange(l), jnp.zeros((1,), jnp.int32), jnp.arange(l - 1, 0, -1)])
    mask = jnp.ones((2 * l, 1), F32).at[l, 0].set(0.0)
    z2 = jnp.concatenate([z[pos], mask], axis=-1)
    return jnp.pad(z2, ((0, 0), (0, LANES - z2.shape[1])))


def kernel(x_prompt, x_sample, p_prompt, p_sample, g_mix, w_in, w_short, b_short, w_f1, b_f1, w_f2, b_f2,
           w_f3, b_f3, w_f4, filter_freq, hy_bias, g_q, g_k, w_hy_br, w_at_br, w_out, g_ffn, w_router,
           b_router, w_gate, b_gate, w_up, b_up, w_down, b_down, g_ple, w_ple_gate, w_ple_proj, g_final):
    assert w_in.shape[0] == 1, "single layer"
    l, d = x_prompt.shape[1], x_prompt.shape[2]
    c = w_hy_br.shape[1]
    aw = w_at_br.shape[1]
    n_kv = aw // (GROUP * HEAD_DIM)
    kvw = n_kv * HEAD_DIM
    assert 2 * l == FFT_R * FFT_R and x_sample.shape[1] == l

    x = jnp.concatenate([x_prompt, x_sample], axis=0)
    p = jnp.concatenate([p_prompt[0], p_sample[0]], axis=0)
    b = x.shape[0]
    n = b * l
    x2d = x.reshape(n, d)

    o_q, o_k, o_v, o_g = 3 * c, 3 * c + aw, 3 * c + aw + kvw, 3 * c + aw + 2 * kvw
    wi = w_in[0]
    w_perm = jnp.concatenate([wi[:, :o_k], wi[:, o_g:], wi[:, o_k:o_g]], axis=1).astype(BF16)
    q_col, gate_col = o_q, o_k
    k_col = gate_col + 2 * d
    v_col = k_col + kvw

    proj = in_proj(x2d, g_mix, w_perm)
    proj3 = proj.reshape(b, l, proj.shape[1])

    consts = _dft_constants()
    z, x0c = hy_prep(proj3, w_short[0], b_short, c)
    w1p = jnp.pad(w_f1[0], ((0, LANES - FILTER_EMB), (0, 0)))
    deltas = jnp.linspace(math.log(DECAY_TARGET) / DECAY_SLOW, math.log(DECAY_TARGET) / DECAY_FAST, c, dtype=F32)
    kt = hy_filter(_filter_features(l), w1p, b_f1, w_f2[0], b_f2, w_f3[0], b_f3, filter_freq, w_f4[0],
                   jnp.abs(deltas)[None, :], c)
    kf = hy_spectrum(kt, consts)
    y_hy = hy_conv(z, x0c, kf, hy_bias, consts)

    cos2, sin2 = _rope_tables(l)
    y_at = attention(proj3, cos2, sin2, g_q, g_k, q_col, k_col, v_col, n_kv)

    out = _merge_moe_ple(y_hy.reshape(n, c), y_at.reshape(n, aw), proj, gate_col, x2d, p.reshape(n, p.shape[-1]),
                         w_hy_br[0], w_at_br[0], w_out[0], g_ffn, w_router[0], b_router, w_gate[0], b_gate[0],
                         w_up[0], b_up[0], w_down[0], b_down[0], g_ple, w_ple_gate[0], w_ple_proj[0], g_final)
    out = out.reshape(b, l, d)
    nb = x_prompt.shape[0]
    return out[:nb], out[nb:]


def _merge_moe_ple(y_hy, y_at, proj, gate_col, x2d, p2d, w_hy_br, w_at_br, w_out, g_ffn, w_router, b_router,
                   w_gate, b_gate, w_up, b_up, w_down, b_down, g_ple, w_ple_gate, w_ple_proj, g_final):
    n = x2d.shape[0]
    wr_pad = jnp.pad(w_router, ((0, 0), (0, LANES - N_EXPERTS)))
    br_pad = jnp.pad(b_router, ((0, 0), (0, LANES - N_EXPERTS)), constant_values=-1e30)
    h1, xn, meta, tw, cnt = merge_router(
        y_hy, y_at, proj, x2d, w_hy_br.astype(BF16), w_at_br.astype(BF16),
        w_out.astype(BF16), g_ffn, wr_pad, br_pad, gate_col, ROUTE_BLOCK)

    nrb = n // ROUTE_BLOCK
    cnt_be = cnt.reshape(nrb, 8, LANES)[:, 0, :N_EXPERTS].astype(I32)
    cnt_be = (cnt_be + RUN_ALIGN - 1) // RUN_ALIGN * RUN_ALIGN
    counts = jnp.sum(cnt_be, axis=0)
    padded = (counts + EXPERT_BLOCK - 1) // EXPERT_BLOCK * EXPERT_BLOCK
    pad_end = jnp.cumsum(padded).astype(I32)
    pad_start = pad_end - padded
    before = jnp.cumsum(cnt_be, axis=0) - cnt_be
    dst_be = pad_start[None, :] + before
    eoff_be = jnp.cumsum(cnt_be, axis=1) - cnt_be
    e4 = meta[:, TOP_K:2 * TOP_K].reshape(nrb, ROUTE_BLOCK, TOP_K)
    slots = jnp.take_along_axis(eoff_be[:, None, :], e4, axis=2) + meta[:, :TOP_K].reshape(nrb, ROUTE_BLOCK, TOP_K)
    slots_t = jnp.pad(jnp.swapaxes(slots, 1, 2), ((0, 0), (0, 8 - TOP_K), (0, 0)), constant_values=-1)
    slots_pad = jnp.pad(slots.reshape(n, TOP_K), ((0, 0), (0, LANES - TOP_K)), constant_values=-1)
    cap = nrb * SORT_ROWS + N_EXPERTS * EXPERT_BLOCK
    nblk = cap // EXPERT_BLOCK
    nused = (pad_end[-1:] // EXPERT_BLOCK).astype(I32)
    blk_e = jnp.minimum(jnp.searchsorted(pad_end, jnp.arange(nblk, dtype=I32) * EXPERT_BLOCK, side="right"),
                        N_EXPERTS - 1).astype(I32)
    cnt_flat = cnt_be.reshape(-1)
    dst_flat = dst_be.reshape(-1).astype(I32)

    xs = dispatch(xn, slots_t, cnt_flat, dst_flat, pad_end, padded, nused, cap, ROUTE_BLOCK)
    ys = experts(xs, blk_e, nused, w_gate.astype(BF16), b_gate[:, None, :], w_up.astype(BF16),
                 b_up[:, None, :], w_down.astype(BF16), b_down[:, None, :])
    return combine_ple(h1, slots_pad, tw, p2d, ys, cnt_flat, dst_flat, w_ple_gate.astype(BF16),
                       w_ple_proj.astype(BF16), g_ple, g_final[None, :], ROUTE_BLOCK)
```

```python
import functools
import math

import jax
import jax.numpy as jnp
import numpy as np
from jax import lax
from jax.experimental import pallas as pl
from jax.experimental.pallas import tpu as pltpu

F32 = jnp.float32
BF16 = jnp.bfloat16
I32 = jnp.int32

EPS = 1e-6
HEAD_DIM = 128
GROUP = 4
GRID_W = 64
ROPE_THETA = 10000.0
HY_SHORT = 3
FILTER_EMB = 33
FILTER_BANDS = 16
DECAY_FAST = 0.3
DECAY_SLOW = 1.5
DECAY_TARGET = 1e-2
N_EXPERTS = 32
TOP_K = 4
SWIGLU_LIMIT = 7.0
SWIGLU_ALPHA = 1.702

LANES = 128
FFT_R = 128
EXPERT_BLOCK = 512
ROUTE_BLOCK = 256
VMEM_LIMIT = 56 * 1024 * 1024


def _cparams(sem, vmem=VMEM_LIMIT):
    return pltpu.CompilerParams(dimension_semantics=sem, vmem_limit_bytes=vmem)


def _rms(x, g):
    return x * lax.rsqrt(jnp.mean(x * x, axis=-1, keepdims=True) + EPS) * g


def _inproj_body(x_ref, g_ref, w_ref, o_ref, a_scr):
    @pl.when(pl.program_id(1) == 0)
    def _():
        a_scr[...] = _rms(x_ref[...], g_ref[...]).astype(BF16)

    o_ref[...] = jnp.dot(a_scr[...], w_ref[...], preferred_element_type=F32).astype(o_ref.dtype)


def in_proj(x2d, g, w_bf16, tm=1024, tn=512):
    n, d = x2d.shape
    nout = w_bf16.shape[1]
    return pl.pallas_call(
        _inproj_body,
        grid=(n // tm, nout // tn),
        in_specs=[
            pl.BlockSpec((tm, d), lambda i, j: (i, 0)),
            pl.BlockSpec((1, d), lambda i, j: (0, 0)),
            pl.BlockSpec((d, tn), lambda i, j: (0, j)),
        ],
        out_specs=pl.BlockSpec((tm, tn), lambda i, j: (i, j)),
        out_shape=jax.ShapeDtypeStruct((n, nout), BF16),
        scratch_shapes=[pltpu.VMEM((tm, d), BF16)],
        compiler_params=_cparams(("parallel", "arbitrary")),
        name="in_proj",
    )(x2d, g, w_bf16)


def _hyprep_body(u_ref, up_ref, un_ref, w_ref, b_ref, z_ref, x0_ref, *, tr, c):
    r = pl.program_id(1)
    nr = pl.num_programs(1)
    u = u_ref[0].astype(F32)
    hp = up_ref[0][15:16, :].astype(F32)
    hn = un_ref[0][0:1, :].astype(F32)
    hp = jnp.where(r == 0, 0.0, hp)
    hn = jnp.where(r == nr - 1, 0.0, hn)
    row = lax.broadcasted_iota(I32, u.shape, 0)
    prev = jnp.where(row == 0, hp, pltpu.roll(u, 1, axis=0))
    nxt = jnp.where(row == tr - 1, hn, pltpu.roll(u, tr - 1, axis=0))
    w = w_ref[...]
    uc = b_ref[...] + prev * w[0:1] + u * w[1:2] + nxt * w[2:3]
    x0 = uc[:, :c]
    x1 = uc[:, c:2 * c]
    v = uc[:, 2 * c:]
    z_ref[0] = (v * x1).astype(z_ref.dtype)
    x0_ref[0] = x0.astype(x0_ref.dtype)


def hy_prep(proj3, w_short, b_short, c, tr=256):
    b, l, _ = proj3.shape
    hb = tr // 16
    nh = l // 16
    body = functools.partial(_hyprep_body, tr=tr, c=c)
    return pl.pallas_call(
        body,
        grid=(b, l // tr),
        in_specs=[
            pl.BlockSpec((1, tr, 3 * c), lambda i, r: (i, r, 0)),
            pl.BlockSpec((1, 16, 3 * c), lambda i, r: (i, jnp.maximum(r * hb - 1, 0), 0)),
            pl.BlockSpec((1, 16, 3 * c), lambda i, r: (i, jnp.minimum((r + 1) * hb, nh - 1), 0)),
            pl.BlockSpec((HY_SHORT, 3 * c), lambda i, r: (0, 0)),
            pl.BlockSpec((1, 3 * c), lambda i, r: (0, 0)),
        ],
        out_specs=[
            pl.BlockSpec((1, tr, c), lambda i, r: (i, r, 0)),
            pl.BlockSpec((1, tr, c), lambda i, r: (i, r, 0)),
        ],
        out_shape=[jax.ShapeDtypeStruct((b, l, c), BF16), jax.ShapeDtypeStruct((b, l, c), BF16)],
        compiler_params=_cparams(("parallel", "parallel")),
        name="hy_prep",
    )(proj3, proj3, proj3, w_short, b_short)


def _hyfilter_body(z_ref, w1_ref, b1_ref, w2_ref, b2_ref, w3_ref, b3_ref, fr_ref, w4_ref, dl_ref, o_ref):
    hi = lax.Precision.HIGHEST
    zb = z_ref[...]
    fr = fr_ref[...]
    h = jnp.sin(fr * (jnp.dot(zb, w1_ref[...], precision=hi, preferred_element_type=F32) + b1_ref[...]))
    h = jnp.sin(fr * (jnp.dot(h, w2_ref[...], precision=hi, preferred_element_type=F32) + b2_ref[...]))
    h = jnp.sin(fr * (jnp.dot(h, w3_ref[...], precision=hi, preferred_element_type=F32) + b3_ref[...]))
    h4 = jnp.dot(h, w4_ref[...], precision=hi, preferred_element_type=F32)
    t = zb[:, 0:1]
    mask = zb[:, FILTER_EMB:FILTER_EMB + 1]
    o_ref[...] = h4 * jnp.exp(-t * dl_ref[...]) * mask


def hy_filter(zfeat, w1p, b1, w2, b2, w3, b3, freq, w4, absdelta, c, tr=1024):
    rows = zfeat.shape[0]
    half_blocks = rows // 2 // tr
    fo = w2.shape[0]
    full = lambda i: (0, 0)
    return pl.pallas_call(
        _hyfilter_body,
        grid=(rows // tr,),
        in_specs=[
            pl.BlockSpec((tr, LANES), lambda i: (i, 0)),
            pl.BlockSpec((LANES, fo), full),
            pl.BlockSpec((1, fo), full),
            pl.BlockSpec((fo, fo), full),
            pl.BlockSpec((1, fo), full),
            pl.BlockSpec((fo, fo), full),
            pl.BlockSpec((1, fo), full),
            pl.BlockSpec((1, fo), full),
            pl.BlockSpec((fo, c), lambda i: (0, i // half_blocks)),
            pl.BlockSpec((1, c), full),
        ],
        out_specs=pl.BlockSpec((tr, c), lambda i: (i, 0)),
        out_shape=jax.ShapeDtypeStruct((rows, c), F32),
        compiler_params=_cparams(("parallel",)),
        name="hy_filter",
    )(zfeat, w1p, b1, w2, b2, w3, b3, freq, w4, absdelta)


def _dft_constants():
    r = FFT_R
    n = r * r
    k = np.arange(r)
    ang = -2.0 * np.pi * np.outer(k, k) / r
    fr, fi = np.cos(ang), np.sin(ang)
    m1 = np.concatenate([fr, fi], axis=0)
    tw = -2.0 * np.pi * np.outer(k, k) / n
    m2 = np.concatenate([fr, fi], axis=1)
    gr, gi = fr, -fi
    g = np.block([[gr, gi], [-gi, gr]])
    h2 = np.concatenate([gr[: r // 2], gi[: r // 2]], axis=0) / n
    return dict(
        m1=jnp.asarray(m1, BF16), twr=jnp.asarray(np.cos(tw), F32), twi=jnp.asarray(np.sin(tw), F32),
        m2=jnp.asarray(m2, BF16), g=jnp.asarray(g, BF16), h2=jnp.asarray(h2, BF16))


CH_HALF = 64
CH_CHUNK = 8
CH_UNROLL = 4


def _dft_forward(zs, s1, m1_ref, twr, twi, m2_ref, half, n1_rows, emit):
    r = FFT_R

    def stage1(c, carry):
        zc = zs[pl.ds(pl.multiple_of((half * CH_HALF + c) * n1_rows, n1_rows), n1_rows), :].astype(BF16)
        a = jnp.dot(m1_ref[:, :n1_rows], zc, preferred_element_type=F32)
        ar, ai = a[:r], a[r:]
        s1[pl.ds(pl.multiple_of(c * 2 * r, 2 * r), r), :] = (ar * twr - ai * twi).astype(BF16)
        s1[pl.ds(pl.multiple_of(c * 2 * r + r, r), r), :] = (ar * twi + ai * twr).astype(BF16)
        return carry

    lax.fori_loop(0, CH_HALF, stage1, 0, unroll=CH_UNROLL)

    def stage2(j, carry):
        rows = CH_CHUNK * 2 * r
        lhs = s1[pl.ds(pl.multiple_of(j * rows, rows), rows), :]
        o = jnp.dot(lhs, m2_ref[...], preferred_element_type=F32).reshape(CH_CHUNK, 2 * r, 2 * r)
        xr = o[:, :r, :r] - o[:, r:, r:]
        xi = o[:, :r, r:] + o[:, r:, :r]
        emit(j, xr, xi)
        return carry

    lax.fori_loop(0, CH_HALF // CH_CHUNK, stage2, 0)


def _hyspec_body(k_ref, m1_ref, twr_ref, twi_ref, m2_ref, o_ref, zs, s1):
    r = FFT_R
    for n1 in range(r):
        blk = k_ref[n1 * r:(n1 + 1) * r, :]
        zs[pl.ds(n1, LANES, stride=r), :] = blk.T
    twr = twr_ref[...]
    twi = twi_ref[...]
    for half in range(LANES // CH_HALF):
        def emit(j, xr, xi, half=half):
            c0 = pl.multiple_of(half * CH_HALF + j * CH_CHUNK, CH_CHUNK)
            o_ref[pl.ds(c0, CH_CHUNK)] = jnp.concatenate([xr, xi], axis=-1).astype(o_ref.dtype)

        _dft_forward(zs, s1, m1_ref, twr, twi, m2_ref, half, r, emit)


def hy_spectrum(kt, consts):
    rows, c = kt.shape
    r = FFT_R
    full2 = lambda i: (0, 0)
    return pl.pallas_call(
        _hyspec_body,
        grid=(c // LANES,),
        in_specs=[
            pl.BlockSpec((rows, LANES), lambda i: (0, i)),
            pl.BlockSpec((2 * r, r), full2),
            pl.BlockSpec((r, r), full2),
            pl.BlockSpec((r, r), full2),
            pl.BlockSpec((r, 2 * r), full2),
        ],
        out_specs=pl.BlockSpec((LANES, r, 2 * r), lambda i: (i, 0, 0)),
        out_shape=jax.ShapeDtypeStruct((c, r, 2 * r), BF16),
        scratch_shapes=[pltpu.VMEM((LANES * r, r), F32), pltpu.VMEM((CH_HALF * 2 * r, r), BF16)],
        compiler_params=_cparams(("parallel",)),
        name="hy_spectrum",
    )(kt, consts["m1"], consts["twr"], consts["twi"], consts["m2"])


def _hyconv_body(z_ref, x0_ref, kf_ref, m1_ref, twr_ref, twi_ref, m2_ref, g_ref, h2_ref, bias_ref,
                 o_ref, zs, s1, s2):
    r = FFT_R
    n1_rows = r // 2
    for n1 in range(n1_rows):
        blk = z_ref[0, n1 * r:(n1 + 1) * r, :].astype(F32)
        zs[pl.ds(n1, LANES, stride=n1_rows), :] = blk.T
    twr = twr_ref[...]
    twi = twi_ref[...]
    for half in range(LANES // CH_HALF):
        def emit(j, xr, xi, half=half):
            c0 = pl.multiple_of(half * CH_HALF + j * CH_CHUNK, CH_CHUNK)
            kf = kf_ref[pl.ds(c0, CH_CHUNK)].astype(F32)
            kr, ki = kf[..., :r], kf[..., r:]
            y = jnp.concatenate([xr * kr - xi * ki, xr * ki + xi * kr], axis=-1).astype(BF16)
            rows = CH_CHUNK * r
            s2[pl.ds(pl.multiple_of(j * rows, rows), rows), :] = y.reshape(rows, 2 * r)

        _dft_forward(zs, s1, m1_ref, twr, twi, m2_ref, half, n1_rows, emit)

        def inv1(j, carry):
            rows = CH_CHUNK * r
            sl = pl.ds(pl.multiple_of(j * rows, rows), rows)
            bm = jnp.dot(s2[sl, :], g_ref[...], preferred_element_type=F32).reshape(CH_CHUNK, r, 2 * r)
            br, bi = bm[..., :r], bm[..., r:]
            b2 = jnp.concatenate([br * twr + bi * twi, bi * twr - br * twi], axis=-1).astype(BF16)
            s2[sl, :] = b2.reshape(rows, 2 * r)
            return carry

        lax.fori_loop(0, CH_HALF // CH_CHUNK, inv1, 0)

        def inv2(c, carry, half=half):
            q = jnp.dot(h2_ref[...], s2[pl.ds(pl.multiple_of(c * r, r), r), :], preferred_element_type=F32)
            yc = q[:n1_rows, :r] - q[n1_rows:, r:]
            zs[pl.ds(pl.multiple_of((half * CH_HALF + c) * n1_rows, n1_rows), n1_rows), :] = yc
            return carry

        lax.fori_loop(0, CH_HALF, inv2, 0, unroll=CH_UNROLL)

    bias = bias_ref[...]
    for n1 in range(n1_rows):
        y = zs[pl.ds(n1, LANES, stride=n1_rows), :].T
        sl = slice(n1 * r, (n1 + 1) * r)
        zz = z_ref[0, sl, :].astype(F32)
        x0 = x0_ref[0, sl, :].astype(F32)
        o_ref[0, sl, :] = ((y + zz * bias) * x0).astype(o_ref.dtype)


def hy_conv(z, x0c, kf, hy_bias, consts):
    b, l, c = z.shape
    r = FFT_R
    full2 = lambda i, j: (0, 0)
    return pl.pallas_call(
        _hyconv_body,
        grid=(c // LANES, b),
        in_specs=[
            pl.BlockSpec((1, l, LANES), lambda j, i: (i, 0, j)),
            pl.BlockSpec((1, l, LANES), lambda j, i: (i, 0, j)),
            pl.BlockSpec((LANES, r, 2 * r), lambda j, i: (j, 0, 0)),
            pl.BlockSpec((2 * r, r), full2),
            pl.BlockSpec((r, r), full2),
            pl.BlockSpec((r, r), full2),
            pl.BlockSpec((r, 2 * r), full2),
            pl.BlockSpec((2 * r, 2 * r), full2),
            pl.BlockSpec((r, r), full2),
            pl.BlockSpec((1, LANES), lambda j, i: (0, j)),
        ],
        out_specs=pl.BlockSpec((1, l, LANES), lambda j, i: (i, 0, j)),
        out_shape=jax.ShapeDtypeStruct((b, l, c), BF16),
        scratch_shapes=[
            pltpu.VMEM((LANES * r // 2, r), F32),
            pltpu.VMEM((CH_HALF * 2 * r, r), BF16),
            pltpu.VMEM((CH_HALF * r, 2 * r), BF16),
        ],
        compiler_params=_cparams(("parallel", "parallel")),
        name="hy_conv",
    )(z, x0c, kf, consts["m1"], consts["twr"], consts["twi"], consts["m2"], consts["g"], consts["h2"], hy_bias)


def _rope(x, c, s):
    return x * c + pltpu.roll(x, HEAD_DIM // 2, axis=1) * s


def _attn_body(q_ref, k_ref, v_ref, cos_ref, sin_ref, gq_ref, gk_ref, o_ref,
               kn_scr, vt_scr, qt_scr, sa_scr, sb_scr, m_scr, l_scr, acc_scr, *, tq, tk, l):
    qi = pl.program_id(2)
    nk = l // tk

    @pl.when(qi == 0)
    def _():
        for r in range(nk):
            sl = slice(r * tk, (r + 1) * tk)
            kn = _rms(k_ref[0, sl, :].astype(F32), gk_ref[...])
            kn_scr[sl, :] = _rope(kn, cos_ref[sl, :], sin_ref[sl, :]).astype(BF16)
            vt_scr[r] = v_ref[0, sl, :].astype(F32).T.astype(BF16)

    row0 = pl.multiple_of(qi * tq, tq)
    c = cos_ref[pl.ds(row0, tq), :]
    s = sin_ref[pl.ds(row0, tq), :]
    scale = HEAD_DIM ** -0.5
    for g in range(GROUP):
        qn = _rms(q_ref[0, :, g * HEAD_DIM:(g + 1) * HEAD_DIM].astype(F32), gq_ref[...])
        qt_scr[:, g * tq:(g + 1) * tq] = (_rope(qn, c, s) * scale).T.astype(BF16)

    m_scr[...] = jnp.full(m_scr.shape, -jnp.inf, F32)
    l_scr[...] = jnp.zeros(l_scr.shape, F32)
    acc_scr[...] = jnp.zeros(acc_scr.shape, F32)
    nq = GROUP * tq

    def scores(j):
        kc = kn_scr[pl.ds(pl.multiple_of(j * tk, tk), tk), :]
        return jnp.dot(kc, qt_scr[...], preferred_element_type=F32)

    sa_scr[...] = scores(0)

    def half_step(j, cur, nxt):
        nxt[...] = scores(jnp.minimum(j + 1, nk - 1))
        st = cur[...]
        m_prev = m_scr[...]
        m_new = jnp.maximum(m_prev, jnp.max(st, axis=0, keepdims=True))
        alpha = jnp.exp(m_prev - m_new)
        p = jnp.exp(st - m_new)
        l_scr[...] = alpha * l_scr[...] + jnp.sum(p, axis=0, keepdims=True)
        acc_scr[...] = alpha * acc_scr[...] + jnp.dot(vt_scr[j], p.astype(BF16), preferred_element_type=F32)
        m_scr[...] = m_new

    def kv_pair(jj, carry):
        half_step(2 * jj, sa_scr, sb_scr)
        half_step(2 * jj + 1, sb_scr, sa_scr)
        return carry

    lax.fori_loop(0, nk // 2, kv_pair, 0)
    o = acc_scr[...] / l_scr[...]
    for g in range(GROUP):
        o_ref[0, :, g * HEAD_DIM:(g + 1) * HEAD_DIM] = o[:, g * tq:(g + 1) * tq].T.astype(o_ref.dtype)


def attention(proj3, cos2, sin2, g_q, g_k, q_col, k_col, v_col, n_kv, tq=256, tk=512):
    b, l, _ = proj3.shape
    gw = GROUP * HEAD_DIM
    body = functools.partial(_attn_body, tq=tq, tk=tk, l=l)
    qb, kb, vb = q_col // gw, k_col // HEAD_DIM, v_col // HEAD_DIM
    return pl.pallas_call(
        body,
        grid=(b, n_kv, l // tq),
        in_specs=[
            pl.BlockSpec((1, tq, gw), lambda i, h, q: (i, q, qb + h)),
            pl.BlockSpec((1, l, HEAD_DIM), lambda i, h, q: (i, 0, kb + h)),
            pl.BlockSpec((1, l, HEAD_DIM), lambda i, h, q: (i, 0, vb + h)),
            pl.BlockSpec((l, HEAD_DIM), lambda i, h, q: (0, 0)),
            pl.BlockSpec((l, HEAD_DIM), lambda i, h, q: (0, 0)),
            pl.BlockSpec((1, HEAD_DIM), lambda i, h, q: (0, 0)),
            pl.BlockSpec((1, HEAD_DIM), lambda i, h, q: (0, 0)),
        ],
        out_specs=pl.BlockSpec((1, tq, gw), lambda i, h, q: (i, q, h)),
        out_shape=jax.ShapeDtypeStruct((b, l, n_kv * gw), BF16),
        scratch_shapes=[
            pltpu.VMEM((l, HEAD_DIM), BF16),
            pltpu.VMEM((l // tk, HEAD_DIM, tk), BF16),
            pltpu.VMEM((HEAD_DIM, GROUP * tq), BF16),
            pltpu.VMEM((tk, GROUP * tq), F32),
            pltpu.VMEM((tk, GROUP * tq), F32),
            pltpu.VMEM((1, GROUP * tq), F32),
            pltpu.VMEM((1, GROUP * tq), F32),
            pltpu.VMEM((HEAD_DIM, GROUP * tq), F32),
        ],
        compiler_params=_cparams(("parallel", "parallel", "arbitrary")),
        name="attention",
    )(proj3, proj3, proj3, cos2, sin2, g_q, g_k)


def _merge_body(yh_ref, ya_ref, gh_ref, ga_ref, h_ref, whb_ref, wab_ref, wo_ref, gf_ref, wr_ref, br_ref,
                h1_ref, xn_ref, meta_ref, slott_ref, tw_ref, cnt_ref, *, tm):
    a = jnp.dot(yh_ref[...], whb_ref[...], preferred_element_type=F32)
    bmat = jnp.dot(ya_ref[...], wab_ref[...], preferred_element_type=F32)
    mix = jax.nn.sigmoid(gh_ref[...].astype(F32)) * a + jax.nn.sigmoid(ga_ref[...].astype(F32)) * bmat
    h1 = h_ref[...] + jnp.dot(mix.astype(BF16), wo_ref[...], preferred_element_type=F32)
    h1_ref[...] = h1
    xn = _rms(h1, gf_ref[...])
    xn_ref[...] = xn.astype(xn_ref.dtype)

    logits = jnp.dot(xn, wr_ref[...], precision=lax.Precision.HIGHEST,
                     preferred_element_type=F32) + br_ref[...]
    lane = lax.broadcasted_iota(I32, logits.shape, 1)
    work = logits
    vals, idxs = [], []
    sel = jnp.zeros(logits.shape, F32)
    for _ in range(TOP_K):
        m = jnp.max(work, axis=-1, keepdims=True)
        idx = jnp.min(jnp.where(work == m, lane, LANES), axis=-1, keepdims=True)
        hit = lane == idx
        vals.append(m)
        idxs.append(idx)
        sel = sel + hit.astype(F32)
        work = jnp.where(hit, -jnp.inf, work)
    ex = [jnp.exp(v - vals[0]) for v in vals]
    den = ex[0] + ex[1] + ex[2] + ex[3]

    rr = lax.broadcasted_iota(I32, (tm, tm), 0)
    cc = lax.broadcasted_iota(I32, (tm, tm), 1)
    tri = (cc < rr).astype(BF16)
    prefix = jnp.dot(tri, sel.astype(BF16), preferred_element_type=F32)
    units = jnp.floor((jnp.sum(sel, axis=0, keepdims=True) + (RUN_ALIGN - 1)) * (1.0 / RUN_ALIGN))
    cnt_ref[...] = jnp.broadcast_to(units * RUN_ALIGN, cnt_ref.shape)
    er = lax.broadcasted_iota(I32, (LANES, LANES), 0)
    ec = lax.broadcasted_iota(I32, (LANES, LANES), 1)
    upper = (er < ec).astype(BF16)
    eoff = RUN_ALIGN * jnp.dot(jnp.broadcast_to(units, (8, LANES)).astype(BF16), upper,
                               preferred_element_type=F32)[0:1, :]
    slot_of = prefix + eoff

    meta = jnp.zeros(logits.shape, F32)
    tw = jnp.zeros(logits.shape, F32)
    for k in range(TOP_K):
        slot = jnp.sum(jnp.where(lane == idxs[k], slot_of, 0.0), axis=-1, keepdims=True)
        meta = jnp.where(lane == k, slot, meta)
        tw = jnp.where(lane == k, ex[k] / den, tw)
    meta = jnp.where(lane < TOP_K, meta, -1.0)
    meta_ref[...] = meta.astype(I32)
    slott_ref[0] = meta.T[0:8, :].astype(I32)
    tw_ref[...] = tw


def merge_router(y_hy, y_at, proj, x2d, whb, wab, wo, g_ffn, wr_pad, br_pad, gate_col, tm):
    n, d = x2d.shape
    ch = y_hy.shape[1]
    ca = y_at.shape[1]
    gb = gate_col // d
    body = functools.partial(_merge_body, tm=tm)
    full = lambda i: (0, 0)
    row = lambda i: (i, 0)
    return pl.pallas_call(
        body,
        grid=(n // tm,),
        in_specs=[
            pl.BlockSpec((tm, ch), row),
            pl.BlockSpec((tm, ca), row),
            pl.BlockSpec((tm, d), lambda i: (i, gb)),
            pl.BlockSpec((tm, d), lambda i: (i, gb + 1)),
            pl.BlockSpec((tm, d), row),
            pl.BlockSpec((ch, d), full),
            pl.BlockSpec((ca, d), full),
            pl.BlockSpec((d, d), full),
            pl.BlockSpec((1, d), full),
            pl.BlockSpec((d, LANES), full),
            pl.BlockSpec((1, LANES), full),
        ],
        out_specs=[
            pl.BlockSpec((tm, d), row),
            pl.BlockSpec((tm, d), row),
            pl.BlockSpec((tm, LANES), row),
            pl.BlockSpec((1, 8, tm), lambda i: (i, 0, 0)),
            pl.BlockSpec((tm, LANES), row),
            pl.BlockSpec((8, LANES), row),
        ],
        out_shape=[
            jax.ShapeDtypeStruct((n, d), F32),
            jax.ShapeDtypeStruct((n, d), BF16),
            jax.ShapeDtypeStruct((n, LANES), I32),
            jax.ShapeDtypeStruct((n // tm, 8, tm), I32),
            jax.ShapeDtypeStruct((n, LANES), F32),
            jax.ShapeDtypeStruct((n // tm * 8, LANES), F32),
        ],
        compiler_params=_cparams(("parallel",)),
        name="merge_router",
    )(y_hy, y_at, proj, proj, x2d, whb, wab, wo, g_ffn, wr_pad, br_pad)


RUN_ALIGN = 8
RUN_BITS = ROUTE_BLOCK.bit_length()
SORT_ROWS = TOP_K * ROUTE_BLOCK + N_EXPERTS * RUN_ALIGN


def _run_copies(i, cnt_ref, dst_ref, make_copy, op):
    def per_expert(e, local):
        c = cnt_ref[i * N_EXPERTS + e]
        d = dst_ref[i * N_EXPERTS + e]
        for bit in range(RUN_ALIGN.bit_length() - 1, RUN_BITS):
            size = 1 << bit

            @pl.when((c & size) != 0)
            def _(size=size):
                done = c & (size - 1)
                op(make_copy(pl.multiple_of(local + done, RUN_ALIGN), pl.multiple_of(d + done, RUN_ALIGN), size))

        return local + c

    lax.fori_loop(0, N_EXPERTS, per_expert, 0)


def _dispatch_body(cnt_ref, dst_ref, pend_ref, padded_ref, nused_ref, slot_ref, xn_ref, xs_hbm,
                   sort_scr, zero_scr, sems, zsem, *, tm, nblk):
    i = pl.program_id(0)
    last = pl.num_programs(0) - 1
    buf = i % 2

    @pl.when(i == 0)
    def _():
        zero_scr[...] = jnp.zeros(zero_scr.shape, F32)

        def zero_block(start):
            cp = pltpu.make_async_copy(zero_scr, xs_hbm.at[pl.ds(start, EXPERT_BLOCK), :], zsem)
            cp.start()
            cp.wait()

        def pad_rows(e, carry):
            @pl.when(padded_ref[e] > 0)
            def _():
                zero_block(pl.multiple_of(pend_ref[e] - EXPERT_BLOCK, EXPERT_BLOCK))

            return carry

        lax.fori_loop(0, N_EXPERTS, pad_rows, 0)

        def tail(bk, carry):
            zero_block(pl.multiple_of(bk * EXPERT_BLOCK, EXPERT_BLOCK))
            return carry

        lax.fori_loop(nused_ref[0], nblk, tail, 0)

    slots = slot_ref[0]
    j = lax.broadcasted_iota(I32, (SORT_ROWS, tm), 0)
    perm = jnp.zeros((SORT_ROWS, tm), F32)
    for k in range(TOP_K):
        perm = perm + (j == slots[k:k + 1, :]).astype(F32)
    sort_scr[buf] = jnp.dot(perm.astype(BF16), xn_ref[...], preferred_element_type=F32)

    def copier(b):
        def make_copy(local, d, size):
            return pltpu.make_async_copy(sort_scr.at[b, pl.ds(local, size), :], xs_hbm.at[pl.ds(d, size), :],
                                         sems.at[b])
        return make_copy

    _run_copies(i, cnt_ref, dst_ref, copier(buf), lambda cp: cp.start())

    @pl.when(i > 0)
    def _():
        _run_copies(i - 1, cnt_ref, dst_ref, copier(1 - buf), lambda cp: cp.wait())

    @pl.when(i == last)
    def _():
        _run_copies(i, cnt_ref, dst_ref, copier(buf), lambda cp: cp.wait())


def dispatch(xn, slots_t, cnt_flat, dst_flat, pad_end, padded, nused, cap, tm):
    n, d = xn.shape
    nblk = cap // EXPERT_BLOCK
    body = functools.partial(_dispatch_body, tm=tm, nblk=nblk)
    return pl.pallas_call(
        body,
        grid_spec=pltpu.PrefetchScalarGridSpec(
            num_scalar_prefetch=5,
            grid=(n // tm,),
            in_specs=[
                pl.BlockSpec((1, 8, tm), lambda i, *_: (i, 0, 0)),
                pl.BlockSpec((tm, d), lambda i, *_: (i, 0)),
            ],
            out_specs=pl.BlockSpec(memory_space=pl.ANY),
            scratch_shapes=[
                pltpu.VMEM((2, SORT_ROWS, d), F32),
                pltpu.VMEM((EXPERT_BLOCK, d), F32),
                pltpu.SemaphoreType.DMA((2,)),
                pltpu.SemaphoreType.DMA(()),
            ],
        ),
        out_shape=jax.ShapeDtypeStruct((cap, d), F32),
        compiler_params=_cparams(("arbitrary",)),
        name="dispatch",
    )(cnt_flat, dst_flat, pad_end, padded, nused, slots_t, xn)


def _experts_body(be_ref, nu_ref, x_ref, wg_ref, bg_ref, wu_ref, bu_ref, wd_ref, bd_ref, o_ref, xb_scr, acc_scr):
    i = pl.program_id(0)
    f = pl.program_id(1)
    nf = pl.num_programs(1)
    used = i < nu_ref[0]

    @pl.when(jnp.logical_and(used, f == 0))
    def _():
        xb_scr[...] = x_ref[...].astype(BF16)
        acc_scr[...] = jnp.zeros(acc_scr.shape, F32)

    @pl.when(used)
    def _():
        xb = xb_scr[...]
        g = jnp.dot(xb, wg_ref[0], preferred_element_type=F32) + bg_ref[0]
        u = jnp.dot(xb, wu_ref[0], preferred_element_type=F32) + bu_ref[0]
        g = jnp.minimum(g, SWIGLU_LIMIT)
        u = jnp.clip(u, -SWIGLU_LIMIT, SWIGLU_LIMIT)
        a = g * jax.nn.sigmoid(SWIGLU_ALPHA * g) * (u + 1.0)
        acc_scr[...] += jnp.dot(a.astype(BF16), wd_ref[0], preferred_element_type=F32)

    @pl.when(f == nf - 1)
    def _():
        o_ref[...] = jnp.where(used, acc_scr[...] + bd_ref[0], 0.0)


def experts(xs, blk_e, nused, wg, bg, wu, bu, wd, bd, tf=1024):
    cap, d = xs.shape
    dff = wg.shape[2]
    nblk = cap // EXPERT_BLOCK
    nf = dff // tf

    def eidx(i, be, nu):
        return be[jnp.minimum(i, nu[0] - 1)]

    def fidx(i, f, nu):
        return jnp.where(i < nu[0], f, nf - 1)

    return pl.pallas_call(
        _experts_body,
        grid_spec=pltpu.PrefetchScalarGridSpec(
            num_scalar_prefetch=2,
            grid=(nblk, nf),
            in_specs=[
                pl.BlockSpec((EXPERT_BLOCK, d), lambda i, f, be, nu: (jnp.minimum(i, nu[0] - 1), 0)),
                pl.BlockSpec((1, d, tf), lambda i, f, be, nu: (eidx(i, be, nu), 0, fidx(i, f, nu))),
                pl.BlockSpec((1, 1, tf), lambda i, f, be, nu: (eidx(i, be, nu), 0, fidx(i, f, nu))),
                pl.BlockSpec((1, d, tf), lambda i, f, be, nu: (eidx(i, be, nu), 0, fidx(i, f, nu))),
                pl.BlockSpec((1, 1, tf), lambda i, f, be, nu: (eidx(i, be, nu), 0, fidx(i, f, nu))),
                pl.BlockSpec((1, tf, d), lambda i, f, be, nu: (eidx(i, be, nu), fidx(i, f, nu), 0)),
                pl.BlockSpec((1, 1, d), lambda i, f, be, nu: (eidx(i, be, nu), 0, 0)),
            ],
            out_specs=pl.BlockSpec((EXPERT_BLOCK, d), lambda i, f, be, nu: (i, 0)),
            scratch_shapes=[pltpu.VMEM((EXPERT_BLOCK, d), BF16), pltpu.VMEM((EXPERT_BLOCK, d), F32)],
        ),
        out_shape=jax.ShapeDtypeStruct((cap, d), F32),
        compiler_params=_cparams(("arbitrary", "arbitrary")),
        name="experts",
    )(blk_e, nused, xs, wg, bg, wu, bu, wd, bd)


def _combine_body(cnt_ref, dst_ref, slot_ref, h_ref, tw_ref, p_ref, y_hbm, wpg_ref, wpp_ref, gp_ref, gfin_ref,
                  o_ref, gath, sems, *, tm):
    i = pl.program_id(0)
    last = pl.num_programs(0) - 1
    buf = i % 2

    def copier(b):
        def make_copy(local, d, size):
            return pltpu.make_async_copy(y_hbm.at[pl.ds(d, size), :], gath.at[b, pl.ds(local, size), :],
                                         sems.at[b])
        return make_copy

    def fetch(step, b):
        gath[b, TOP_K * tm:, :] = jnp.zeros((SORT_ROWS - TOP_K * tm, gath.shape[2]), F32)
        _run_copies(step, cnt_ref, dst_ref, copier(b), lambda cp: cp.start())

    @pl.when(i == 0)
    def _():
        fetch(0, 0)

    @pl.when(i < last)
    def _():
        fetch(i + 1, 1 - buf)

    _run_copies(i, cnt_ref, dst_ref, copier(buf), lambda cp: cp.wait())

    tw = tw_ref[...]
    slots = slot_ref[...]
    lane = lax.broadcasted_iota(I32, (tm, SORT_ROWS), 1)
    wmat = jnp.zeros((tm, SORT_ROWS), F32)
    for k in range(TOP_K):
        wmat = wmat + jnp.where(lane == slots[:, k:k + 1], tw[:, k:k + 1], 0.0)
    w_hi = wmat.astype(BF16)
    w_lo = (wmat - w_hi.astype(F32)).astype(BF16)
    rows = gath[buf].astype(BF16)
    moe = jnp.dot(w_hi, rows, preferred_element_type=F32) + jnp.dot(w_lo, rows, preferred_element_type=F32)
    h2 = h_ref[...] + moe
    a = _rms(h2, gp_ref[...]).astype(BF16)
    gate = jax.nn.sigmoid(jnp.dot(a, wpg_ref[...], preferred_element_type=F32))
    pp = jnp.dot(p_ref[...].astype(BF16), wpp_ref[...], preferred_element_type=F32)
    h3 = h2 + gate * pp
    o_ref[...] = _rms(h3, gfin_ref[...])


def combine_ple(h1, slots, tw, p2d, y, cnt_flat, dst_flat, wpg, wpp, g_ple, g_final, tm):
    n, d = h1.shape
    pd = p2d.shape[1]
    body = functools.partial(_combine_body, tm=tm)
    full = lambda i, *_: (0, 0)
    row = lambda i, *_: (i, 0)
    return pl.pallas_call(
        body,
        grid_spec=pltpu.PrefetchScalarGridSpec(
            num_scalar_prefetch=2,
            grid=(n // tm,),
            in_specs=[
                pl.BlockSpec((tm, LANES), row),
                pl.BlockSpec((tm, d), row),
                pl.BlockSpec((tm, LANES), row),
                pl.BlockSpec((tm, pd), row),
                pl.BlockSpec(memory_space=pl.ANY),
                pl.BlockSpec((d, d), full, pipeline_mode=pl.Buffered(1)),
                pl.BlockSpec((pd, d), full, pipeline_mode=pl.Buffered(1)),
                pl.BlockSpec((1, d), full),
                pl.BlockSpec((1, d), full),
            ],
            out_specs=pl.BlockSpec((tm, d), row),
            scratch_shapes=[pltpu.VMEM((2, SORT_ROWS, d), F32), pltpu.SemaphoreType.DMA((2,))],
        ),
        out_shape=jax.ShapeDtypeStruct((n, d), F32),
        compiler_params=_cparams(("arbitrary",)),
        name="combine_ple",
    )(cnt_flat, dst_flat, slots, h1, tw, p2d, y, wpg, wpp, g_ple, g_final)


def _rope_tables(l):
    rows = l // GRID_W
    row = jnp.repeat(jnp.arange(rows, dtype=F32), GRID_W)
    col = jnp.tile(jnp.arange(GRID_W, dtype=F32), rows)
    axis_dim = HEAD_DIM // 2
    inv = 1.0 / (ROPE_THETA ** (jnp.arange(0, axis_dim, 2, dtype=F32) / axis_dim))
    ang = jnp.concatenate([row[:, None] * inv, col[:, None] * inv], axis=-1)
    c, s = jnp.cos(ang), jnp.sin(ang)
    return jnp.concatenate([c, c], axis=-1), jnp.concatenate([-s, s], axis=-1)


def _filter_features(l):
    t = jnp.linspace(0.0, 1.0, l, dtype=F32)[:, None]
    w = 2.0 * math.pi * jnp.arange(l, dtype=F32)[:, None] / l
    f = jnp.linspace(1e-4, FILTER_BANDS - 1, FILTER_BANDS, dtype=F32)[None, :]
    z = jnp.concatenate([t, jnp.cos(f * w), -jnp.sin(f * w)], axis=-1)
    pos = jnp.concatenate([jnp.arange(l), jnp.zeros((1,), jnp.int32), jnp.arange(l - 1, 0, -1)])
    mask = jnp.ones((2 * l, 1), F32).at[l, 0].set(0.0)
    z2 = jnp.concatenate([z[pos], mask], axis=-1)
    return jnp.pad(z2, ((0, 0), (0, LANES - z2.shape[1])))


def kernel(x_prompt, x_sample, p_prompt, p_sample, g_mix, w_in, w_short, b_short, w_f1, b_f1, w_f2, b_f2,
           w_f3, b_f3, w_f4, filter_freq, hy_bias, g_q, g_k, w_hy_br, w_at_br, w_out, g_ffn, w_router,
           b_router, w_gate, b_gate, w_up, b_up, w_down, b_down, g_ple, w_ple_gate, w_ple_proj, g_final):
    assert w_in.shape[0] == 1, "single layer"
    l, d = x_prompt.shape[1], x_prompt.shape[2]
    c = w_hy_br.shape[1]
    aw = w_at_br.shape[1]
    n_kv = aw // (GROUP * HEAD_DIM)
    kvw = n_kv * HEAD_DIM
    assert 2 * l == FFT_R * FFT_R and x_sample.shape[1] == l

    x = jnp.concatenate([x_prompt, x_sample], axis=0)
    p = jnp.concatenate([p_prompt[0], p_sample[0]], axis=0)
    b = x.shape[0]
    n = b * l
    x2d = x.reshape(n, d)

    o_q, o_k, o_v, o_g = 3 * c, 3 * c + aw, 3 * c + aw + kvw, 3 * c + aw + 2 * kvw
    wi = w_in[0]
    w_perm = jnp.concatenate([wi[:, :o_k], wi[:, o_g:], wi[:, o_k:o_g]], axis=1).astype(BF16)
    q_col, gate_col = o_q, o_k
    k_col = gate_col + 2 * d
    v_col = k_col + kvw

    proj = in_proj(x2d, g_mix, w_perm)
    proj3 = proj.reshape(b, l, proj.shape[1])

    consts = _dft_constants()
    z, x0c = hy_prep(proj3, w_short[0], b_short, c)
    w1p = jnp.pad(w_f1[0], ((0, LANES - FILTER_EMB), (0, 0)))
    deltas = jnp.linspace(math.log(DECAY_TARGET) / DECAY_SLOW, math.log(DECAY_TARGET) / DECAY_FAST, c, dtype=F32)
    kt = hy_filter(_filter_features(l), w1p, b_f1, w_f2[0], b_f2, w_f3[0], b_f3, filter_freq, w_f4[0],
                   jnp.abs(deltas)[None, :], c)
    kf = hy_spectrum(kt, consts)
    y_hy = hy_conv(z, x0c, kf, hy_bias, consts)

    cos2, sin2 = _rope_tables(l)
    y_at = attention(proj3, cos2, sin2, g_q, g_k, q_col, k_col, v_col, n_kv)

    out = _merge_moe_ple(y_hy.reshape(n, c), y_at.reshape(n, aw), proj, gate_col, x2d, p.reshape(n, p.shape[-1]),
                         w_hy_br[0], w_at_br[0], w_out[0], g_ffn, w_router[0], b_router, w_gate[0], b_gate[0],
                         w_up[0], b_up[0], w_down[0], b_down[0], g_ple, w_ple_gate[0], w_ple_proj[0], g_final)
    out = out.reshape(b, l, d)
    nb = x_prompt.shape[0]
    return out[:nb], out[nb:]


def _merge_moe_ple(y_hy, y_at, proj, gate_col, x2d, p2d, w_hy_br, w_at_br, w_out, g_ffn, w_router, b_router,
                   w_gate, b_gate, w_up, b_up, w_down, b_down, g_ple, w_ple_gate, w_ple_proj, g_final):
    n = x2d.shape[0]
    wr_pad = jnp.pad(w_router, ((0, 0), (0, LANES - N_EXPERTS)))
    br_pad = jnp.pad(b_router, ((0, 0), (0, LANES - N_EXPERTS)), constant_values=-1e30)
    h1, xn, slots, slots_t, tw, cnt = merge_router(
        y_hy, y_at, proj, x2d, w_hy_br.astype(BF16), w_at_br.astype(BF16),
        w_out.astype(BF16), g_ffn, wr_pad, br_pad, gate_col, ROUTE_BLOCK)

    nrb = n // ROUTE_BLOCK
    cnt_be = cnt.reshape(nrb, 8, LANES)[:, 0, :N_EXPERTS].astype(I32)
    counts = jnp.sum(cnt_be, axis=0)
    padded = (counts + EXPERT_BLOCK - 1) // EXPERT_BLOCK * EXPERT_BLOCK
    pad_end = jnp.cumsum(padded).astype(I32)
    pad_start = pad_end - padded
    before = jnp.cumsum(cnt_be, axis=0) - cnt_be
    dst_be = pad_start[None, :] + before
    cap = nrb * SORT_ROWS + N_EXPERTS * EXPERT_BLOCK
    nblk = cap // EXPERT_BLOCK
    nused = (pad_end[-1:] // EXPERT_BLOCK).astype(I32)
    blk_e = jnp.minimum(jnp.searchsorted(pad_end, jnp.arange(nblk, dtype=I32) * EXPERT_BLOCK, side="right"),
                        N_EXPERTS - 1).astype(I32)
    cnt_flat = cnt_be.reshape(-1)
    dst_flat = dst_be.reshape(-1).astype(I32)

    xs = dispatch(xn, slots_t, cnt_flat, dst_flat, pad_end, padded, nused, cap, ROUTE_BLOCK)
    ys = experts(xs, blk_e, nused, w_gate.astype(BF16), b_gate[:, None, :], w_up.astype(BF16),
                 b_up[:, None, :], w_down.astype(BF16), b_down[:, None, :])
    return combine_ple(h1, slots, tw, p2d, ys, cnt_flat, dst_flat, w_ple_gate.astype(BF16),
                       w_ple_proj.astype(BF16), g_ple, g_final[None, :], ROUTE_BLOCK)
```

```python
import functools
import math

import jax
import jax.numpy as jnp
import numpy as np
from jax import lax
from jax.experimental import pallas as pl
from jax.experimental.pallas import tpu as pltpu

F32 = jnp.float32
BF16 = jnp.bfloat16
I32 = jnp.int32

EPS = 1e-6
HEAD_DIM = 128
GROUP = 4
GRID_W = 64
ROPE_THETA = 10000.0
HY_SHORT = 3
FILTER_EMB = 33
FILTER_BANDS = 16
DECAY_FAST = 0.3
DECAY_SLOW = 1.5
DECAY_TARGET = 1e-2
N_EXPERTS = 32
TOP_K = 4
SWIGLU_LIMIT = 7.0
SWIGLU_ALPHA = 1.702

LANES = 128
FFT_R = 128
EXPERT_BLOCK = 512
ROUTE_BLOCK = 256
INPROJ_TN = 1024
VMEM_LIMIT = 56 * 1024 * 1024


def _cparams(sem, vmem=VMEM_LIMIT):
    return pltpu.CompilerParams(dimension_semantics=sem, vmem_limit_bytes=vmem)


def _rms(x, g):
    return x * lax.rsqrt(jnp.mean(x * x, axis=-1, keepdims=True) + EPS) * g


def _pair_specs(tm, width, nba, **kw):
    return [
        pl.BlockSpec((tm, width), lambda i, *_: (jnp.minimum(i, nba - 1), 0), **kw),
        pl.BlockSpec((tm, width), lambda i, *_: (jnp.maximum(i - nba, 0), 0), **kw),
    ]


def _inproj_body(xa_ref, xb_ref, g_ref, w_ref, o_ref, a_scr, *, nba):
    @pl.when(pl.program_id(1) == 0)
    def _():
        x = jnp.where(pl.program_id(0) < nba, xa_ref[...], xb_ref[...])
        a_scr[...] = _rms(x, g_ref[...]).astype(BF16)

    o_ref[...] = jnp.dot(a_scr[...], w_ref[...], preferred_element_type=F32).astype(o_ref.dtype)


def in_proj(xa, xb, g, w_bf16, tm=1024, tn=1024):
    d = xa.shape[1]
    n = xa.shape[0] + xb.shape[0]
    nba = xa.shape[0] // tm
    nout = w_bf16.shape[1]
    return pl.pallas_call(
        functools.partial(_inproj_body, nba=nba),
        grid=(n // tm, nout // tn),
        in_specs=_pair_specs(tm, d, nba, pipeline_mode=pl.Buffered(1)) + [
            pl.BlockSpec((1, d), lambda i, j: (0, 0)),
            pl.BlockSpec((d, tn), lambda i, j: (0, j)),
        ],
        out_specs=pl.BlockSpec((tm, tn), lambda i, j: (i, j)),
        out_shape=jax.ShapeDtypeStruct((n, nout), BF16),
        scratch_shapes=[pltpu.VMEM((tm, d), BF16)],
        compiler_params=_cparams(("parallel", "arbitrary")),
        name="in_proj",
    )(xa, xb, g, w_bf16)


def _hyprep_body(u_ref, up_ref, un_ref, w_ref, b_ref, z_ref, x0_ref, *, tr, c):
    r = pl.program_id(1)
    nr = pl.num_programs(1)
    u = u_ref[0].astype(F32)
    hp = up_ref[0][15:16, :].astype(F32)
    hn = un_ref[0][0:1, :].astype(F32)
    hp = jnp.where(r == 0, 0.0, hp)
    hn = jnp.where(r == nr - 1, 0.0, hn)
    row = lax.broadcasted_iota(I32, u.shape, 0)
    prev = jnp.where(row == 0, hp, pltpu.roll(u, 1, axis=0))
    nxt = jnp.where(row == tr - 1, hn, pltpu.roll(u, tr - 1, axis=0))
    w = w_ref[...]
    uc = b_ref[...] + prev * w[0:1] + u * w[1:2] + nxt * w[2:3]
    x0 = uc[:, :c]
    x1 = uc[:, c:2 * c]
    v = uc[:, 2 * c:]
    z_ref[0] = (v * x1).astype(z_ref.dtype)
    x0_ref[0] = x0.astype(x0_ref.dtype)


def hy_prep(proj3, w_short, b_short, c, tr=256):
    b, l, _ = proj3.shape
    hb = tr // 16
    nh = l // 16
    body = functools.partial(_hyprep_body, tr=tr, c=c)
    return pl.pallas_call(
        body,
        grid=(b, l // tr),
        in_specs=[
            pl.BlockSpec((1, tr, 3 * c), lambda i, r: (i, r, 0)),
            pl.BlockSpec((1, 16, 3 * c), lambda i, r: (i, jnp.maximum(r * hb - 1, 0), 0)),
            pl.BlockSpec((1, 16, 3 * c), lambda i, r: (i, jnp.minimum((r + 1) * hb, nh - 1), 0)),
            pl.BlockSpec((HY_SHORT, 3 * c), lambda i, r: (0, 0)),
            pl.BlockSpec((1, 3 * c), lambda i, r: (0, 0)),
        ],
        out_specs=[
            pl.BlockSpec((1, tr, c), lambda i, r: (i, r, 0)),
            pl.BlockSpec((1, tr, c), lambda i, r: (i, r, 0)),
        ],
        out_shape=[jax.ShapeDtypeStruct((b, l, c), BF16), jax.ShapeDtypeStruct((b, l, c), BF16)],
        compiler_params=_cparams(("parallel", "parallel")),
        name="hy_prep",
    )(proj3, proj3, proj3, w_short, b_short)


def _hyfilter_body(z_ref, w1_ref, b1_ref, w2_ref, b2_ref, w3_ref, b3_ref, fr_ref, w4_ref, dl_ref, o_ref):
    hi = lax.Precision.HIGHEST
    zb = z_ref[...]
    fr = fr_ref[...]
    h = jnp.sin(fr * (jnp.dot(zb, w1_ref[...], precision=hi, preferred_element_type=F32) + b1_ref[...]))
    h = jnp.sin(fr * (jnp.dot(h, w2_ref[...], precision=hi, preferred_element_type=F32) + b2_ref[...]))
    h = jnp.sin(fr * (jnp.dot(h, w3_ref[...], precision=hi, preferred_element_type=F32) + b3_ref[...]))
    h4 = jnp.dot(h, w4_ref[...], precision=hi, preferred_element_type=F32)
    t = zb[:, 0:1]
    mask = zb[:, FILTER_EMB:FILTER_EMB + 1]
    o_ref[...] = h4 * jnp.exp(-t * dl_ref[...]) * mask


def hy_filter(zfeat, w1p, b1, w2, b2, w3, b3, freq, w4, absdelta, c, tr=1024):
    rows = zfeat.shape[0]
    half_blocks = rows // 2 // tr
    fo = w2.shape[0]
    full = lambda i: (0, 0)
    return pl.pallas_call(
        _hyfilter_body,
        grid=(rows // tr,),
        in_specs=[
            pl.BlockSpec((tr, LANES), lambda i: (i, 0)),
            pl.BlockSpec((LANES, fo), full),
            pl.BlockSpec((1, fo), full),
            pl.BlockSpec((fo, fo), full),
            pl.BlockSpec((1, fo), full),
            pl.BlockSpec((fo, fo), full),
            pl.BlockSpec((1, fo), full),
            pl.BlockSpec((1, fo), full),
            pl.BlockSpec((fo, c), lambda i: (0, i // half_blocks)),
            pl.BlockSpec((1, c), full),
        ],
        out_specs=pl.BlockSpec((tr, c), lambda i: (i, 0)),
        out_shape=jax.ShapeDtypeStruct((rows, c), F32),
        compiler_params=_cparams(("parallel",)),
        name="hy_filter",
    )(zfeat, w1p, b1, w2, b2, w3, b3, freq, w4, absdelta)


def _dft_constants():
    r = FFT_R
    n = r * r
    k = np.arange(r)
    ang = -2.0 * np.pi * np.outer(k, k) / r
    fr, fi = np.cos(ang), np.sin(ang)
    m1 = np.concatenate([fr, fi], axis=0)
    tw = -2.0 * np.pi * np.outer(k, k) / n
    m2 = np.concatenate([fr, fi], axis=1)
    gr, gi = fr, -fi
    g = np.block([[gr, gi], [-gi, gr]])
    h2 = np.concatenate([gr[: r // 2], gi[: r // 2]], axis=0) / n
    return dict(
        m1=jnp.asarray(m1, BF16), twr=jnp.asarray(np.cos(tw), F32), twi=jnp.asarray(np.sin(tw), F32),
        m2=jnp.asarray(m2, BF16), g=jnp.asarray(g, BF16), h2=jnp.asarray(h2, BF16))


CH_HALF = 64
CH_CHUNK = 8
CH_UNROLL = 8


def _dft_forward(zs, s1, m1_ref, twr, twi, m2_ref, half, n1_rows, emit):
    r = FFT_R

    def stage1(c, carry):
        zc = zs[pl.ds(pl.multiple_of((half * CH_HALF + c) * n1_rows, n1_rows), n1_rows), :].astype(BF16)
        a = jnp.dot(m1_ref[:, :n1_rows], zc, preferred_element_type=F32)
        ar, ai = a[:r], a[r:]
        s1[pl.ds(pl.multiple_of(c * 2 * r, 2 * r), r), :] = (ar * twr - ai * twi).astype(BF16)
        s1[pl.ds(pl.multiple_of(c * 2 * r + r, r), r), :] = (ar * twi + ai * twr).astype(BF16)
        return carry

    lax.fori_loop(0, CH_HALF, stage1, 0, unroll=CH_UNROLL)

    def stage2(j, carry):
        rows = CH_CHUNK * 2 * r
        lhs = s1[pl.ds(pl.multiple_of(j * rows, rows), rows), :]
        o = jnp.dot(lhs, m2_ref[...], preferred_element_type=F32).reshape(CH_CHUNK, 2 * r, 2 * r)
        xr = o[:, :r, :r] - o[:, r:, r:]
        xi = o[:, :r, r:] + o[:, r:, :r]
        emit(j, xr, xi)
        return carry

    lax.fori_loop(0, CH_HALF // CH_CHUNK, stage2, 0)


def _hyspec_body(k_ref, m1_ref, twr_ref, twi_ref, m2_ref, o_ref, zs, s1):
    r = FFT_R
    for n1 in range(r):
        blk = k_ref[n1 * r:(n1 + 1) * r, :]
        zs[pl.ds(n1, LANES, stride=r), :] = blk.T
    twr = twr_ref[...]
    twi = twi_ref[...]
    for half in range(LANES // CH_HALF):
        def emit(j, xr, xi, half=half):
            c0 = pl.multiple_of(half * CH_HALF + j * CH_CHUNK, CH_CHUNK)
            o_ref[pl.ds(c0, CH_CHUNK)] = jnp.concatenate([xr, xi], axis=-1).astype(o_ref.dtype)

        _dft_forward(zs, s1, m1_ref, twr, twi, m2_ref, half, r, emit)


def hy_spectrum(kt, consts):
    rows, c = kt.shape
    r = FFT_R
    full2 = lambda i: (0, 0)
    return pl.pallas_call(
        _hyspec_body,
        grid=(c // LANES,),
        in_specs=[
            pl.BlockSpec((rows, LANES), lambda i: (0, i)),
            pl.BlockSpec((2 * r, r), full2),
            pl.BlockSpec((r, r), full2),
            pl.BlockSpec((r, r), full2),
            pl.BlockSpec((r, 2 * r), full2),
        ],
        out_specs=pl.BlockSpec((LANES, r, 2 * r), lambda i: (i, 0, 0)),
        out_shape=jax.ShapeDtypeStruct((c, r, 2 * r), BF16),
        scratch_shapes=[pltpu.VMEM((LANES * r, r), F32), pltpu.VMEM((CH_HALF * 2 * r, r), BF16)],
        compiler_params=_cparams(("parallel",)),
        name="hy_spectrum",
    )(kt, consts["m1"], consts["twr"], consts["twi"], consts["m2"])


def _hyconv_body(z_ref, x0_ref, kf_ref, m1_ref, twr_ref, twi_ref, m2_ref, g_ref, h2_ref, bias_ref,
                 o_ref, zs, s1, s2):
    r = FFT_R
    n1_rows = r // 2
    for n1 in range(n1_rows):
        blk = z_ref[0, n1 * r:(n1 + 1) * r, :].astype(F32)
        zs[pl.ds(n1, LANES, stride=n1_rows), :] = blk.T
    twr = twr_ref[...]
    twi = twi_ref[...]
    for half in range(LANES // CH_HALF):
        def emit(j, xr, xi, half=half):
            c0 = pl.multiple_of(half * CH_HALF + j * CH_CHUNK, CH_CHUNK)
            kf = kf_ref[pl.ds(c0, CH_CHUNK)].astype(F32)
            kr, ki = kf[..., :r], kf[..., r:]
            y = jnp.concatenate([xr * kr - xi * ki, xr * ki + xi * kr], axis=-1).astype(BF16)
            rows = CH_CHUNK * r
            s2[pl.ds(pl.multiple_of(j * rows, rows), rows), :] = y.reshape(rows, 2 * r)

        _dft_forward(zs, s1, m1_ref, twr, twi, m2_ref, half, n1_rows, emit)

        def inv1(j, carry):
            rows = CH_CHUNK * r
            sl = pl.ds(pl.multiple_of(j * rows, rows), rows)
            bm = jnp.dot(s2[sl, :], g_ref[...], preferred_element_type=F32).reshape(CH_CHUNK, r, 2 * r)
            br, bi = bm[..., :r], bm[..., r:]
            b2 = jnp.concatenate([br * twr + bi * twi, bi * twr - br * twi], axis=-1).astype(BF16)
            s2[sl, :] = b2.reshape(rows, 2 * r)
            return carry

        lax.fori_loop(0, CH_HALF // CH_CHUNK, inv1, 0)

        def inv2(c, carry, half=half):
            q = jnp.dot(h2_ref[...], s2[pl.ds(pl.multiple_of(c * r, r), r), :], preferred_element_type=F32)
            yc = q[:n1_rows, :r] - q[n1_rows:, r:]
            zs[pl.ds(pl.multiple_of((half * CH_HALF + c) * n1_rows, n1_rows), n1_rows), :] = yc
            return carry

        lax.fori_loop(0, CH_HALF, inv2, 0, unroll=CH_UNROLL)

    bias = bias_ref[...]
    for n1 in range(n1_rows):
        y = zs[pl.ds(n1, LANES, stride=n1_rows), :].T
        sl = slice(n1 * r, (n1 + 1) * r)
        zz = z_ref[0, sl, :].astype(F32)
        x0 = x0_ref[0, sl, :].astype(F32)
        o_ref[0, sl, :] = ((y + zz * bias) * x0).astype(o_ref.dtype)


def hy_conv(z, x0c, kf, hy_bias, consts):
    b, l, c = z.shape
    r = FFT_R
    full2 = lambda i, j: (0, 0)
    return pl.pallas_call(
        _hyconv_body,
        grid=(c // LANES, b),
        in_specs=[
            pl.BlockSpec((1, l, LANES), lambda j, i: (i, 0, j)),
            pl.BlockSpec((1, l, LANES), lambda j, i: (i, 0, j)),
            pl.BlockSpec((LANES, r, 2 * r), lambda j, i: (j, 0, 0)),
            pl.BlockSpec((2 * r, r), full2),
            pl.BlockSpec((r, r), full2),
            pl.BlockSpec((r, r), full2),
            pl.BlockSpec((r, 2 * r), full2),
            pl.BlockSpec((2 * r, 2 * r), full2),
            pl.BlockSpec((r, r), full2),
            pl.BlockSpec((1, LANES), lambda j, i: (0, j)),
        ],
        out_specs=pl.BlockSpec((1, l, LANES), lambda j, i: (i, 0, j)),
        out_shape=jax.ShapeDtypeStruct((b, l, c), BF16),
        scratch_shapes=[
            pltpu.VMEM((LANES * r // 2, r), F32),
            pltpu.VMEM((CH_HALF * 2 * r, r), BF16),
            pltpu.VMEM((CH_HALF * r, 2 * r), BF16),
        ],
        compiler_params=_cparams(("parallel", "parallel")),
        name="hy_conv",
    )(z, x0c, kf, consts["m1"], consts["twr"], consts["twi"], consts["m2"], consts["g"], consts["h2"], hy_bias)


def _rope(x, c, s):
    return x * c + pltpu.roll(x, HEAD_DIM // 2, axis=1) * s


def _attn_body(q_ref, k_ref, v_ref, cos_ref, sin_ref, gq_ref, gk_ref, o_ref,
               kn_scr, vt_scr, qt_scr, sa_scr, sb_scr, m_scr, l_scr, acc_scr, *, tq, tk, l):
    qi = pl.program_id(2)
    nk = l // tk

    @pl.when(qi == 0)
    def _():
        for r in range(nk):
            sl = slice(r * tk, (r + 1) * tk)
            kn = _rms(k_ref[0, sl, :].astype(F32), gk_ref[...])
            kn_scr[sl, :] = _rope(kn, cos_ref[sl, :], sin_ref[sl, :]).astype(BF16)
            vt_scr[r] = v_ref[0, sl, :].astype(F32).T.astype(BF16)

    row0 = pl.multiple_of(qi * tq, tq)
    c = cos_ref[pl.ds(row0, tq), :]
    s = sin_ref[pl.ds(row0, tq), :]
    scale = HEAD_DIM ** -0.5 * math.log2(math.e)
    for g in range(GROUP):
        qn = _rms(q_ref[0, :, g * HEAD_DIM:(g + 1) * HEAD_DIM].astype(F32), gq_ref[...])
        qt_scr[:, g * tq:(g + 1) * tq] = (_rope(qn, c, s) * scale).T.astype(BF16)

    m_scr[...] = jnp.full(m_scr.shape, -jnp.inf, F32)
    l_scr[...] = jnp.zeros(l_scr.shape, F32)
    acc_scr[...] = jnp.zeros(acc_scr.shape, F32)
    nq = GROUP * tq

    def scores(j):
        kc = kn_scr[pl.ds(pl.multiple_of(j * tk, tk), tk), :]
        return jnp.dot(kc, qt_scr[...], preferred_element_type=F32)

    sa_scr[...] = scores(0)

    def half_step(j, cur, nxt):
        nxt[...] = scores(jnp.minimum(j + 1, nk - 1))
        st = cur[...]
        m_prev = m_scr[...]
        m_new = jnp.maximum(m_prev, jnp.max(st, axis=0, keepdims=True))
        alpha = jnp.exp2(m_prev - m_new)
        p = jnp.exp2(st - m_new)
        l_scr[...] = alpha * l_scr[...] + jnp.sum(p, axis=0, keepdims=True)
        acc_scr[...] = alpha * acc_scr[...] + jnp.dot(vt_scr[j], p.astype(BF16), preferred_element_type=F32)
        m_scr[...] = m_new

    def kv_pair(jj, carry):
        half_step(2 * jj, sa_scr, sb_scr)
        half_step(2 * jj + 1, sb_scr, sa_scr)
        return carry

    lax.fori_loop(0, nk // 2, kv_pair, 0)
    o = acc_scr[...] / l_scr[...]
    for g in range(GROUP):
        o_ref[0, :, g * HEAD_DIM:(g + 1) * HEAD_DIM] = o[:, g * tq:(g + 1) * tq].T.astype(o_ref.dtype)


def attention(proj3, cos2, sin2, g_q, g_k, q_col, k_col, v_col, n_kv, tq=256, tk=512):
    b, l, _ = proj3.shape
    gw = GROUP * HEAD_DIM
    body = functools.partial(_attn_body, tq=tq, tk=tk, l=l)
    qb, kb, vb = q_col // gw, k_col // HEAD_DIM, v_col // HEAD_DIM
    return pl.pallas_call(
        body,
        grid=(b, n_kv, l // tq),
        in_specs=[
            pl.BlockSpec((1, tq, gw), lambda i, h, q: (i, q, qb + h)),
            pl.BlockSpec((1, l, HEAD_DIM), lambda i, h, q: (i, 0, kb + h)),
            pl.BlockSpec((1, l, HEAD_DIM), lambda i, h, q: (i, 0, vb + h)),
            pl.BlockSpec((l, HEAD_DIM), lambda i, h, q: (0, 0)),
            pl.BlockSpec((l, HEAD_DIM), lambda i, h, q: (0, 0)),
            pl.BlockSpec((1, HEAD_DIM), lambda i, h, q: (0, 0)),
            pl.BlockSpec((1, HEAD_DIM), lambda i, h, q: (0, 0)),
        ],
        out_specs=pl.BlockSpec((1, tq, gw), lambda i, h, q: (i, q, h)),
        out_shape=jax.ShapeDtypeStruct((b, l, n_kv * gw), BF16),
        scratch_shapes=[
            pltpu.VMEM((l, HEAD_DIM), BF16),
            pltpu.VMEM((l // tk, HEAD_DIM, tk), BF16),
            pltpu.VMEM((HEAD_DIM, GROUP * tq), BF16),
            pltpu.VMEM((tk, GROUP * tq), F32),
            pltpu.VMEM((tk, GROUP * tq), F32),
            pltpu.VMEM((1, GROUP * tq), F32),
            pltpu.VMEM((1, GROUP * tq), F32),
            pltpu.VMEM((HEAD_DIM, GROUP * tq), F32),
        ],
        compiler_params=_cparams(("parallel", "parallel", "arbitrary")),
        name="attention",
    )(proj3, proj3, proj3, cos2, sin2, g_q, g_k)


def _merge_body(yh_ref, ya_ref, gh_ref, ga_ref, ha_ref, hb_ref, whb_ref, wab_ref, wo_ref, gf_ref,
                wrh_ref, wrl_ref, br_ref, h1_ref, xn_ref, meta_ref, slott_ref, tw_ref, cnt_ref, *, tm, nba):
    a = jnp.dot(yh_ref[...], whb_ref[...], preferred_element_type=F32)
    bmat = jnp.dot(ya_ref[...], wab_ref[...], preferred_element_type=F32)
    mix = jax.nn.sigmoid(gh_ref[...].astype(F32)) * a + jax.nn.sigmoid(ga_ref[...].astype(F32)) * bmat
    h0 = jnp.where(pl.program_id(0) < nba, ha_ref[...], hb_ref[...])
    h1 = h0 + jnp.dot(mix.astype(BF16), wo_ref[...], preferred_element_type=F32)
    h1_ref[...] = h1
    xn = _rms(h1, gf_ref[...])
    xn_ref[...] = xn.astype(xn_ref.dtype)

    x_hi = xn.astype(BF16)
    x_lo = (xn - x_hi.astype(F32)).astype(BF16)
    logits = (jnp.dot(x_hi, wrh_ref[...], preferred_element_type=F32)
              + jnp.dot(x_lo, wrh_ref[...], preferred_element_type=F32)
              + jnp.dot(x_hi, wrl_ref[...], preferred_element_type=F32)) + br_ref[...]
    lane = lax.broadcasted_iota(I32, logits.shape, 1)
    work = logits
    vals, idxs = [], []
    sel = jnp.zeros(logits.shape, F32)
    for _ in range(TOP_K):
        m = jnp.max(work, axis=-1, keepdims=True)
        idx = jnp.min(jnp.where(work == m, lane, LANES), axis=-1, keepdims=True)
        hit = lane == idx
        vals.append(m)
        idxs.append(idx)
        sel = sel + hit.astype(F32)
        work = jnp.where(hit, -jnp.inf, work)
    ex = [jnp.exp(v - vals[0]) for v in vals]
    den = ex[0] + ex[1] + ex[2] + ex[3]

    rr = lax.broadcasted_iota(I32, (tm, tm), 0)
    cc = lax.broadcasted_iota(I32, (tm, tm), 1)
    tri = (cc < rr).astype(BF16)
    prefix = jnp.dot(tri, sel.astype(BF16), preferred_element_type=F32)
    units = jnp.floor((jnp.sum(sel, axis=0, keepdims=True) + (RUN_ALIGN - 1)) * (1.0 / RUN_ALIGN))
    cnt_ref[...] = jnp.broadcast_to(units * RUN_ALIGN, cnt_ref.shape)
    er = lax.broadcasted_iota(I32, (LANES, LANES), 0)
    ec = lax.broadcasted_iota(I32, (LANES, LANES), 1)
    upper = (er < ec).astype(BF16)
    eoff = RUN_ALIGN * jnp.dot(jnp.broadcast_to(units, (8, LANES)).astype(BF16), upper,
                               preferred_element_type=F32)[0:1, :]
    slot_of = prefix + eoff

    meta = jnp.zeros(logits.shape, F32)
    tw = jnp.zeros(logits.shape, F32)
    for k in range(TOP_K):
        slot = jnp.sum(jnp.where(lane == idxs[k], slot_of, 0.0), axis=-1, keepdims=True)
        meta = jnp.where(lane == k, slot, meta)
        tw = jnp.where(lane == k, ex[k] / den, tw)
    meta = jnp.where(lane < TOP_K, meta, -1.0)
    meta_ref[...] = meta.astype(I32)
    slott_ref[0] = meta.T[0:8, :].astype(I32)
    tw_ref[...] = tw


def merge_router(y_hy, y_at, proj, xa, xb, whb, wab, wo, g_ffn, wr_hi, wr_lo, br_pad, gate_col, tm):
    d = xa.shape[1]
    n = xa.shape[0] + xb.shape[0]
    nba = xa.shape[0] // tm
    ch = y_hy.shape[1]
    ca = y_at.shape[1]
    gb = gate_col // d
    body = functools.partial(_merge_body, tm=tm, nba=nba)
    full = lambda i: (0, 0)
    row = lambda i: (i, 0)
    return pl.pallas_call(
        body,
        grid=(n // tm,),
        in_specs=[
            pl.BlockSpec((tm, ch), row),
            pl.BlockSpec((tm, ca), row),
            pl.BlockSpec((tm, d), lambda i: (i, gb)),
            pl.BlockSpec((tm, d), lambda i: (i, gb + 1)),
        ] + _pair_specs(tm, d, nba) + [
            pl.BlockSpec((ch, d), full),
            pl.BlockSpec((ca, d), full),
            pl.BlockSpec((d, d), full),
            pl.BlockSpec((1, d), full),
            pl.BlockSpec((d, LANES), full),
            pl.BlockSpec((d, LANES), full),
            pl.BlockSpec((1, LANES), full),
        ],
        out_specs=[
            pl.BlockSpec((tm, d), row),
            pl.BlockSpec((tm, d), row),
            pl.BlockSpec((tm, LANES), row),
            pl.BlockSpec((1, 8, tm), lambda i: (i, 0, 0)),
            pl.BlockSpec((tm, LANES), row),
            pl.BlockSpec((8, LANES), row),
        ],
        out_shape=[
            jax.ShapeDtypeStruct((n, d), F32),
            jax.ShapeDtypeStruct((n, d), BF16),
            jax.ShapeDtypeStruct((n, LANES), I32),
            jax.ShapeDtypeStruct((n // tm, 8, tm), I32),
            jax.ShapeDtypeStruct((n, LANES), F32),
            jax.ShapeDtypeStruct((n // tm * 8, LANES), F32),
        ],
        compiler_params=_cparams(("parallel",)),
        name="merge_router",
    )(y_hy, y_at, proj, proj, xa, xb, whb, wab, wo, g_ffn, wr_hi, wr_lo, br_pad)


RUN_ALIGN = 8
RUN_BITS = ROUTE_BLOCK.bit_length()
SORT_ROWS = TOP_K * ROUTE_BLOCK + N_EXPERTS * RUN_ALIGN


def _run_copies(i, cnt_ref, dst_ref, make_copy, op):
    def per_expert(e, local):
        c = cnt_ref[i * N_EXPERTS + e]
        d = dst_ref[i * N_EXPERTS + e]
        for bit in range(RUN_ALIGN.bit_length() - 1, RUN_BITS):
            size = 1 << bit

            @pl.when((c & size) != 0)
            def _(size=size):
                done = c & (size - 1)
                op(make_copy(pl.multiple_of(local + done, RUN_ALIGN), pl.multiple_of(d + done, RUN_ALIGN), size))

        return local + c

    lax.fori_loop(0, N_EXPERTS, per_expert, 0)


def _dispatch_body(cnt_ref, dst_ref, pend_ref, padded_ref, nused_ref, slot_ref, xn_ref, xs_hbm,
                   sort_scr, zero_scr, sems, zsem, *, tm, nblk):
    i = pl.program_id(0)
    last = pl.num_programs(0) - 1
    buf = i % 2

    @pl.when(i == 0)
    def _():
        zero_scr[...] = jnp.zeros(zero_scr.shape, F32)

        def zero_block(start):
            cp = pltpu.make_async_copy(zero_scr, xs_hbm.at[pl.ds(start, EXPERT_BLOCK), :], zsem)
            cp.start()
            cp.wait()

        def pad_rows(e, carry):
            @pl.when(padded_ref[e] > 0)
            def _():
                zero_block(pl.multiple_of(pend_ref[e] - EXPERT_BLOCK, EXPERT_BLOCK))

            return carry

        lax.fori_loop(0, N_EXPERTS, pad_rows, 0)

        def tail(bk, carry):
            zero_block(pl.multiple_of(bk * EXPERT_BLOCK, EXPERT_BLOCK))
            return carry

        lax.fori_loop(nused_ref[0], nblk, tail, 0)

    slots = slot_ref[0]
    j = lax.broadcasted_iota(I32, (SORT_ROWS, tm), 0)
    perm = jnp.zeros((SORT_ROWS, tm), F32)
    for k in range(TOP_K):
        perm = perm + (j == slots[k:k + 1, :]).astype(F32)
    sort_scr[buf] = jnp.dot(perm.astype(BF16), xn_ref[...], preferred_element_type=F32)

    def copier(b):
        def make_copy(local, d, size):
            return pltpu.make_async_copy(sort_scr.at[b, pl.ds(local, size), :], xs_hbm.at[pl.ds(d, size), :],
                                         sems.at[b])
        return make_copy

    _run_copies(i, cnt_ref, dst_ref, copier(buf), lambda cp: cp.start())

    @pl.when(i > 0)
    def _():
        _run_copies(i - 1, cnt_ref, dst_ref, copier(1 - buf), lambda cp: cp.wait())

    @pl.when(i == last)
    def _():
        _run_copies(i, cnt_ref, dst_ref, copier(buf), lambda cp: cp.wait())


def dispatch(xn, slots_t, cnt_flat, dst_flat, pad_end, padded, nused, cap, tm):
    n, d = xn.shape
    nblk = cap // EXPERT_BLOCK
    body = functools.partial(_dispatch_body, tm=tm, nblk=nblk)
    return pl.pallas_call(
        body,
        grid_spec=pltpu.PrefetchScalarGridSpec(
            num_scalar_prefetch=5,
            grid=(n // tm,),
            in_specs=[
                pl.BlockSpec((1, 8, tm), lambda i, *_: (i, 0, 0)),
                pl.BlockSpec((tm, d), lambda i, *_: (i, 0)),
            ],
            out_specs=pl.BlockSpec(memory_space=pl.ANY),
            scratch_shapes=[
                pltpu.VMEM((2, SORT_ROWS, d), F32),
                pltpu.VMEM((EXPERT_BLOCK, d), F32),
                pltpu.SemaphoreType.DMA((2,)),
                pltpu.SemaphoreType.DMA(()),
            ],
        ),
        out_shape=jax.ShapeDtypeStruct((cap, d), F32),
        compiler_params=_cparams(("arbitrary",)),
        name="dispatch",
    )(cnt_flat, dst_flat, pad_end, padded, nused, slots_t, xn)


def _experts_body(be_ref, nu_ref, x_ref, wg_ref, bg_ref, wu_ref, bu_ref, wd_ref, bd_ref, o_ref, xb_scr, acc_scr):
    i = pl.program_id(0)
    f = pl.program_id(1)
    nf = pl.num_programs(1)
    used = i < nu_ref[0]

    @pl.when(jnp.logical_and(used, f == 0))
    def _():
        xb_scr[...] = x_ref[...].astype(BF16)
        acc_scr[...] = jnp.zeros(acc_scr.shape, F32)

    @pl.when(used)
    def _():
        xb = xb_scr[...]
        g = jnp.dot(xb, wg_ref[0], preferred_element_type=F32) + bg_ref[0]
        u = jnp.dot(xb, wu_ref[0], preferred_element_type=F32) + bu_ref[0]
        g = jnp.minimum(g, SWIGLU_LIMIT)
        u = jnp.clip(u, -SWIGLU_LIMIT, SWIGLU_LIMIT)
        a = g * jax.nn.sigmoid(SWIGLU_ALPHA * g) * (u + 1.0)
        acc_scr[...] += jnp.dot(a.astype(BF16), wd_ref[0], preferred_element_type=F32)

    @pl.when(f == nf - 1)
    def _():
        o_ref[...] = jnp.where(used, acc_scr[...] + bd_ref[0], 0.0)


def experts(xs, blk_e, nused, wg, bg, wu, bu, wd, bd, tf=1024):
    cap, d = xs.shape
    dff = wg.shape[2]
    nblk = cap // EXPERT_BLOCK
    nf = dff // tf

    def last_used(i, nu):
        return jnp.maximum(jnp.minimum(i, nu[0] - 1), 0)

    def eidx(i, be, nu):
        return be[last_used(i, nu)]

    def fidx(i, f, nu):
        return jnp.where(i < nu[0], f, nf - 1)

    return pl.pallas_call(
        _experts_body,
        grid_spec=pltpu.PrefetchScalarGridSpec(
            num_scalar_prefetch=2,
            grid=(nblk, nf),
            in_specs=[
                pl.BlockSpec((EXPERT_BLOCK, d), lambda i, f, be, nu: (last_used(i, nu), 0)),
                pl.BlockSpec((1, d, tf), lambda i, f, be, nu: (eidx(i, be, nu), 0, fidx(i, f, nu))),
                pl.BlockSpec((1, 1, tf), lambda i, f, be, nu: (eidx(i, be, nu), 0, fidx(i, f, nu))),
                pl.BlockSpec((1, d, tf), lambda i, f, be, nu: (eidx(i, be, nu), 0, fidx(i, f, nu))),
                pl.BlockSpec((1, 1, tf), lambda i, f, be, nu: (eidx(i, be, nu), 0, fidx(i, f, nu))),
                pl.BlockSpec((1, tf, d), lambda i, f, be, nu: (eidx(i, be, nu), fidx(i, f, nu), 0)),
                pl.BlockSpec((1, 1, d), lambda i, f, be, nu: (eidx(i, be, nu), 0, 0)),
            ],
            out_specs=pl.BlockSpec((EXPERT_BLOCK, d), lambda i, f, be, nu: (i, 0)),
            scratch_shapes=[pltpu.VMEM((EXPERT_BLOCK, d), BF16), pltpu.VMEM((EXPERT_BLOCK, d), F32)],
        ),
        out_shape=jax.ShapeDtypeStruct((cap, d), F32),
        compiler_params=_cparams(("arbitrary", "arbitrary")),
        name="experts",
    )(blk_e, nused, xs, wg, bg, wu, bu, wd, bd)


def _combine_body(cnt_ref, dst_ref, slot_ref, h_ref, tw_ref, pa_ref, pb_ref, y_hbm, wpg_ref, wpp_ref, gp_ref,
                  gfin_ref, oa_ref, ob_ref, gath, sems, *, tm, nba):
    i = pl.program_id(0)
    last = pl.num_programs(0) - 1
    buf = i % 2

    def copier(b):
        def make_copy(local, d, size):
            return pltpu.make_async_copy(y_hbm.at[pl.ds(d, size), :], gath.at[b, pl.ds(local, size), :],
                                         sems.at[b])
        return make_copy

    def fetch(step, b):
        gath[b, TOP_K * tm:, :] = jnp.zeros((SORT_ROWS - TOP_K * tm, gath.shape[2]), F32)
        _run_copies(step, cnt_ref, dst_ref, copier(b), lambda cp: cp.start())

    @pl.when(i == 0)
    def _():
        fetch(0, 0)

    @pl.when(i < last)
    def _():
        fetch(i + 1, 1 - buf)

    _run_copies(i, cnt_ref, dst_ref, copier(buf), lambda cp: cp.wait())

    tw = tw_ref[...]
    slots = slot_ref[...]
    lane = lax.broadcasted_iota(I32, (tm, SORT_ROWS), 1)
    wmat = jnp.zeros((tm, SORT_ROWS), F32)
    for k in range(TOP_K):
        wmat = wmat + jnp.where(lane == slots[:, k:k + 1], tw[:, k:k + 1], 0.0)
    w_hi = wmat.astype(BF16)
    w_lo = (wmat - w_hi.astype(F32)).astype(BF16)
    rows = gath[buf].astype(BF16)
    moe = jnp.dot(w_hi, rows, preferred_element_type=F32) + jnp.dot(w_lo, rows, preferred_element_type=F32)
    h2 = h_ref[...] + moe
    a = _rms(h2, gp_ref[...]).astype(BF16)
    gate = jax.nn.sigmoid(jnp.dot(a, wpg_ref[...], preferred_element_type=F32))
    pblk = jnp.where(i < nba, pa_ref[...], pb_ref[...])
    pp = jnp.dot(pblk.astype(BF16), wpp_ref[...], preferred_element_type=F32)
    h3 = h2 + gate * pp
    res = _rms(h3, gfin_ref[...])

    @pl.when(i < nba)
    def _():
        oa_ref[...] = res

    @pl.when(i >= nba)
    def _():
        ob_ref[...] = res


def combine_ple(h1, slots, tw, pa, pb, y, cnt_flat, dst_flat, wpg, wpp, g_ple, g_final, tm):
    n, d = h1.shape
    pd = pa.shape[1]
    nba = pa.shape[0] // tm
    body = functools.partial(_combine_body, tm=tm, nba=nba)
    full = lambda i, *_: (0, 0)
    row = lambda i, *_: (i, 0)
    return pl.pallas_call(
        body,
        grid_spec=pltpu.PrefetchScalarGridSpec(
            num_scalar_prefetch=2,
            grid=(n // tm,),
            in_specs=[
                pl.BlockSpec((tm, LANES), row),
                pl.BlockSpec((tm, d), row),
                pl.BlockSpec((tm, LANES), row),
            ] + _pair_specs(tm, pd, nba) + [
                pl.BlockSpec(memory_space=pl.ANY),
                pl.BlockSpec((d, d), full, pipeline_mode=pl.Buffered(1)),
                pl.BlockSpec((pd, d), full, pipeline_mode=pl.Buffered(1)),
                pl.BlockSpec((1, d), full),
                pl.BlockSpec((1, d), full),
            ],
            out_specs=_pair_specs(tm, d, nba),
            scratch_shapes=[pltpu.VMEM((2, SORT_ROWS, d), F32), pltpu.SemaphoreType.DMA((2,))],
        ),
        out_shape=[jax.ShapeDtypeStruct((pa.shape[0], d), F32), jax.ShapeDtypeStruct((pb.shape[0], d), F32)],
        compiler_params=_cparams(("arbitrary",)),
        name="combine_ple",
    )(cnt_flat, dst_flat, slots, h1, tw, pa, pb, y, wpg, wpp, g_ple, g_final)


def _rope_tables(l):
    rows = l // GRID_W
    row = jnp.repeat(jnp.arange(rows, dtype=F32), GRID_W)
    col = jnp.tile(jnp.arange(GRID_W, dtype=F32), rows)
    axis_dim = HEAD_DIM // 2
    inv = 1.0 / (ROPE_THETA ** (jnp.arange(0, axis_dim, 2, dtype=F32) / axis_dim))
    ang = jnp.concatenate([row[:, None] * inv, col[:, None] * inv], axis=-1)
    c, s = jnp.cos(ang), jnp.sin(ang)
    return jnp.concatenate([c, c], axis=-1), jnp.concatenate([-s, s], axis=-1)


def _filter_features(l):
    t = jnp.linspace(0.0, 1.0, l, dtype=F32)[:, None]
    w = 2.0 * math.pi * jnp.arange(l, dtype=F32)[:, None] / l
    f = jnp.linspace(1e-4, FILTER_BANDS - 1, FILTER_BANDS, dtype=F32)[None, :]
    z = jnp.concatenate([t, jnp.cos(f * w), -jnp.sin(f * w)], axis=-1)
    pos = jnp.concatenate([jnp.arange(l), jnp.zeros((1,), jnp.int32), jnp.arange(l - 1, 0, -1)])
    mask = jnp.ones((2 * l, 1), F32).at[l, 0].set(0.0)
    z2 = jnp.concatenate([z[pos], mask], axis=-1)
    return jnp.pad(z2, ((0, 0), (0, LANES - z2.shape[1])))


def kernel(x_prompt, x_sample, p_prompt, p_sample, g_mix, w_in, w_short, b_short, w_f1, b_f1, w_f2, b_f2,
           w_f3, b_f3, w_f4, filter_freq, hy_bias, g_q, g_k, w_hy_br, w_at_br, w_out, g_ffn, w_router,
           b_router, w_gate, b_gate, w_up, b_up, w_down, b_down, g_ple, w_ple_gate, w_ple_proj, g_final):
    assert w_in.shape[0] == 1, "single layer"
    l, d = x_prompt.shape[1], x_prompt.shape[2]
    c = w_hy_br.shape[1]
    aw = w_at_br.shape[1]
    n_kv = aw // (GROUP * HEAD_DIM)
    kvw = n_kv * HEAD_DIM
    assert 2 * l == FFT_R * FFT_R and x_sample.shape[1] == l

    ba, bb = x_prompt.shape[0], x_sample.shape[0]
    b = ba + bb
    n = b * l
    xa, xb = x_prompt.reshape(ba * l, d), x_sample.reshape(bb * l, d)
    pa, pb = p_prompt[0].reshape(ba * l, -1), p_sample[0].reshape(bb * l, -1)

    o_q, o_k, o_v, o_g = 3 * c, 3 * c + aw, 3 * c + aw + kvw, 3 * c + aw + 2 * kvw
    wi = w_in[0]
    w_perm = jnp.concatenate([wi[:, :o_k], wi[:, o_g:], wi[:, o_k:o_g]], axis=1).astype(BF16)
    w_perm = jnp.pad(w_perm, ((0, 0), (0, -w_perm.shape[1] % INPROJ_TN)))
    q_col, gate_col = o_q, o_k
    k_col = gate_col + 2 * d
    v_col = k_col + kvw

    proj = in_proj(xa, xb, g_mix, w_perm, tn=INPROJ_TN)
    proj3 = proj.reshape(b, l, proj.shape[1])

    consts = _dft_constants()
    z, x0c = hy_prep(proj3, w_short[0], b_short, c)
    w1p = jnp.pad(w_f1[0], ((0, LANES - FILTER_EMB), (0, 0)))
    deltas = jnp.linspace(math.log(DECAY_TARGET) / DECAY_SLOW, math.log(DECAY_TARGET) / DECAY_FAST, c, dtype=F32)
    kt = hy_filter(_filter_features(l), w1p, b_f1, w_f2[0], b_f2, w_f3[0], b_f3, filter_freq, w_f4[0],
                   jnp.abs(deltas)[None, :], c)
    kf = hy_spectrum(kt, consts)
    y_hy = hy_conv(z, x0c, kf, hy_bias, consts)

    cos2, sin2 = _rope_tables(l)
    y_at = attention(proj3, cos2, sin2, g_q, g_k, q_col, k_col, v_col, n_kv)

    oa, ob = _merge_moe_ple(y_hy.reshape(n, c), y_at.reshape(n, aw), proj, gate_col, xa, xb, pa, pb,
                            w_hy_br[0], w_at_br[0], w_out[0], g_ffn, w_router[0], b_router, w_gate[0], b_gate[0],
                            w_up[0], b_up[0], w_down[0], b_down[0], g_ple, w_ple_gate[0], w_ple_proj[0], g_final)
    return oa.reshape(ba, l, d), ob.reshape(bb, l, d)


def _merge_moe_ple(y_hy, y_at, proj, gate_col, xa, xb, pa, pb, w_hy_br, w_at_br, w_out, g_ffn, w_router, b_router,
                   w_gate, b_gate, w_up, b_up, w_down, b_down, g_ple, w_ple_gate, w_ple_proj, g_final):
    n = xa.shape[0] + xb.shape[0]
    wr_pad = jnp.pad(w_router, ((0, 0), (0, LANES - N_EXPERTS)))
    wr_hi = wr_pad.astype(BF16)
    wr_lo = (wr_pad - wr_hi.astype(F32)).astype(BF16)
    br_pad = jnp.pad(b_router, ((0, 0), (0, LANES - N_EXPERTS)), constant_values=-1e30)
    h1, xn, slots, slots_t, tw, cnt = merge_router(
        y_hy, y_at, proj, xa, xb, w_hy_br.astype(BF16), w_at_br.astype(BF16),
        w_out.astype(BF16), g_ffn, wr_hi, wr_lo, br_pad, gate_col, ROUTE_BLOCK)

    nrb = n // ROUTE_BLOCK
    cnt_be = cnt.reshape(nrb, 8, LANES)[:, 0, :N_EXPERTS].astype(I32)
    counts = jnp.sum(cnt_be, axis=0)
    padded = (counts + EXPERT_BLOCK - 1) // EXPERT_BLOCK * EXPERT_BLOCK
    pad_end = jnp.cumsum(padded).astype(I32)
    pad_start = pad_end - padded
    before = jnp.cumsum(cnt_be, axis=0) - cnt_be
    dst_be = pad_start[None, :] + before
    cap = nrb * SORT_ROWS + N_EXPERTS * EXPERT_BLOCK
    nblk = cap // EXPERT_BLOCK
    nused = (pad_end[-1:] // EXPERT_BLOCK).astype(I32)
    blk_start = jnp.arange(nblk, dtype=I32) * EXPERT_BLOCK
    blk_e = jnp.minimum(jnp.sum((pad_end[None, :] <= blk_start[:, None]).astype(I32), axis=1), N_EXPERTS - 1)
    cnt_flat = cnt_be.reshape(-1)
    dst_flat = dst_be.reshape(-1).astype(I32)

    xs = dispatch(xn, slots_t, cnt_flat, dst_flat, pad_end, padded, nused, cap, ROUTE_BLOCK)
    ys = experts(xs, blk_e, nused, w_gate.astype(BF16), b_gate[:, None, :], w_up.astype(BF16),
                 b_up[:, None, :], w_down.astype(BF16), b_down[:, None, :])
    return combine_ple(h1, slots, tw, pa, pb, ys, cnt_flat, dst_flat, w_ple_gate.astype(BF16),
                       w_ple_proj.astype(BF16), g_ple, g_final[None, :], ROUTE_BLOCK)
```

```python
import functools
import math

import jax
import jax.numpy as jnp
import numpy as np
from jax import lax
from jax.experimental import pallas as pl
from jax.experimental.pallas import tpu as pltpu

F32 = jnp.float32
BF16 = jnp.bfloat16
I32 = jnp.int32

EPS = 1e-6
HEAD_DIM = 128
GROUP = 4
GRID_W = 64
ROPE_THETA = 10000.0
HY_SHORT = 3
FILTER_EMB = 33
FILTER_BANDS = 16
DECAY_FAST = 0.3
DECAY_SLOW = 1.5
DECAY_TARGET = 1e-2
N_EXPERTS = 32
TOP_K = 4
SWIGLU_LIMIT = 7.0
SWIGLU_ALPHA = 1.702

LANES = 128
FFT_R = 128
EXPERT_BLOCK = 512
ROUTE_BLOCK = 256
INPROJ_TN = 1024
VMEM_LIMIT = 56 * 1024 * 1024


def _cparams(sem, vmem=VMEM_LIMIT):
    return pltpu.CompilerParams(dimension_semantics=sem, vmem_limit_bytes=vmem)


def _rms(x, g):
    return x * lax.rsqrt(jnp.mean(x * x, axis=-1, keepdims=True) + EPS) * g


def _pair_specs(tm, width, nba, **kw):
    return [
        pl.BlockSpec((tm, width), lambda i, *_: (jnp.minimum(i, nba - 1), 0), **kw),
        pl.BlockSpec((tm, width), lambda i, *_: (jnp.maximum(i - nba, 0), 0), **kw),
    ]


def _inproj_body(xa_ref, xb_ref, g_ref, w_ref, o_ref, a_scr, *, nba):
    @pl.when(pl.program_id(1) == 0)
    def _():
        x = jnp.where(pl.program_id(0) < nba, xa_ref[...], xb_ref[...])
        a_scr[...] = _rms(x, g_ref[...]).astype(BF16)

    o_ref[...] = jnp.dot(a_scr[...], w_ref[...], preferred_element_type=F32).astype(o_ref.dtype)


def in_proj(xa, xb, g, w_bf16, tm=1024, tn=1024):
    d = xa.shape[1]
    n = xa.shape[0] + xb.shape[0]
    nba = xa.shape[0] // tm
    nout = w_bf16.shape[1]
    return pl.pallas_call(
        functools.partial(_inproj_body, nba=nba),
        grid=(n // tm, nout // tn),
        in_specs=_pair_specs(tm, d, nba, pipeline_mode=pl.Buffered(1)) + [
            pl.BlockSpec((1, d), lambda i, j: (0, 0)),
            pl.BlockSpec((d, tn), lambda i, j: (0, j)),
        ],
        out_specs=pl.BlockSpec((tm, tn), lambda i, j: (i, j)),
        out_shape=jax.ShapeDtypeStruct((n, nout), BF16),
        scratch_shapes=[pltpu.VMEM((tm, d), BF16)],
        compiler_params=_cparams(("parallel", "arbitrary")),
        name="in_proj",
    )(xa, xb, g, w_bf16)


def _hyprep_body(u_ref, up_ref, un_ref, w_ref, b_ref, z_ref, x0_ref, *, tr, c):
    r = pl.program_id(1)
    nr = pl.num_programs(1)
    u = u_ref[0].astype(F32)
    hp = up_ref[0][15:16, :].astype(F32)
    hn = un_ref[0][0:1, :].astype(F32)
    hp = jnp.where(r == 0, 0.0, hp)
    hn = jnp.where(r == nr - 1, 0.0, hn)
    row = lax.broadcasted_iota(I32, u.shape, 0)
    prev = jnp.where(row == 0, hp, pltpu.roll(u, 1, axis=0))
    nxt = jnp.where(row == tr - 1, hn, pltpu.roll(u, tr - 1, axis=0))
    w = w_ref[...]
    uc = b_ref[...] + prev * w[0:1] + u * w[1:2] + nxt * w[2:3]
    x0 = uc[:, :c]
    x1 = uc[:, c:2 * c]
    v = uc[:, 2 * c:]
    z_ref[0] = (v * x1).astype(z_ref.dtype)
    x0_ref[0] = x0.astype(x0_ref.dtype)


def hy_prep(proj3, w_short, b_short, c, tr=256):
    b, l, _ = proj3.shape
    hb = tr // 16
    nh = l // 16
    body = functools.partial(_hyprep_body, tr=tr, c=c)
    return pl.pallas_call(
        body,
        grid=(b, l // tr),
        in_specs=[
            pl.BlockSpec((1, tr, 3 * c), lambda i, r: (i, r, 0)),
            pl.BlockSpec((1, 16, 3 * c), lambda i, r: (i, jnp.maximum(r * hb - 1, 0), 0)),
            pl.BlockSpec((1, 16, 3 * c), lambda i, r: (i, jnp.minimum((r + 1) * hb, nh - 1), 0)),
            pl.BlockSpec((HY_SHORT, 3 * c), lambda i, r: (0, 0)),
            pl.BlockSpec((1, 3 * c), lambda i, r: (0, 0)),
        ],
        out_specs=[
            pl.BlockSpec((1, tr, c), lambda i, r: (i, r, 0)),
            pl.BlockSpec((1, tr, c), lambda i, r: (i, r, 0)),
        ],
        out_shape=[jax.ShapeDtypeStruct((b, l, c), BF16), jax.ShapeDtypeStruct((b, l, c), BF16)],
        compiler_params=_cparams(("parallel", "parallel")),
        name="hy_prep",
    )(proj3, proj3, proj3, w_short, b_short)


def _hyfilter_body(z_ref, w1_ref, b1_ref, w2_ref, b2_ref, w3_ref, b3_ref, fr_ref, w4_ref, dl_ref, o_ref):
    hi = lax.Precision.HIGHEST
    zb = z_ref[...]
    fr = fr_ref[...]
    h = jnp.sin(fr * (jnp.dot(zb, w1_ref[...], precision=hi, preferred_element_type=F32) + b1_ref[...]))
    h = jnp.sin(fr * (jnp.dot(h, w2_ref[...], precision=hi, preferred_element_type=F32) + b2_ref[...]))
    h = jnp.sin(fr * (jnp.dot(h, w3_ref[...], precision=hi, preferred_element_type=F32) + b3_ref[...]))
    h4 = jnp.dot(h, w4_ref[...], precision=hi, preferred_element_type=F32)
    t = zb[:, 0:1]
    mask = zb[:, FILTER_EMB:FILTER_EMB + 1]
    o_ref[...] = h4 * jnp.exp(-t * dl_ref[...]) * mask


def hy_filter(zfeat, w1p, b1, w2, b2, w3, b3, freq, w4, absdelta, c, tr=1024):
    rows = zfeat.shape[0]
    half_blocks = rows // 2 // tr
    fo = w2.shape[0]
    full = lambda i: (0, 0)
    return pl.pallas_call(
        _hyfilter_body,
        grid=(rows // tr,),
        in_specs=[
            pl.BlockSpec((tr, LANES), lambda i: (i, 0)),
            pl.BlockSpec((LANES, fo), full),
            pl.BlockSpec((1, fo), full),
            pl.BlockSpec((fo, fo), full),
            pl.BlockSpec((1, fo), full),
            pl.BlockSpec((fo, fo), full),
            pl.BlockSpec((1, fo), full),
            pl.BlockSpec((1, fo), full),
            pl.BlockSpec((fo, c), lambda i: (0, i // half_blocks)),
            pl.BlockSpec((1, c), full),
        ],
        out_specs=pl.BlockSpec((tr, c), lambda i: (i, 0)),
        out_shape=jax.ShapeDtypeStruct((rows, c), F32),
        compiler_params=_cparams(("parallel",)),
        name="hy_filter",
    )(zfeat, w1p, b1, w2, b2, w3, b3, freq, w4, absdelta)


def _dft_constants():
    r = FFT_R
    n = r * r
    k = np.arange(r)
    ang = -2.0 * np.pi * np.outer(k, k) / r
    fr, fi = np.cos(ang), np.sin(ang)
    m1 = np.concatenate([fr, fi], axis=0)
    tw = -2.0 * np.pi * np.outer(k, k) / n
    m2 = np.concatenate([fr, fi], axis=1)
    gr, gi = fr, -fi
    g = np.block([[gr, gi], [-gi, gr]])
    h2 = np.concatenate([gr[: r // 2], gi[: r // 2]], axis=0) / n
    return dict(
        m1=jnp.asarray(m1, BF16), twr=jnp.asarray(np.cos(tw), F32), twi=jnp.asarray(np.sin(tw), F32),
        m2=jnp.asarray(m2, BF16), g=jnp.asarray(g, BF16), h2=jnp.asarray(h2, BF16))


CH_HALF = 64
CH_CHUNK = 8
CH_UNROLL = 8


def _dft_forward(zs, s1, m1_ref, twr, twi, m2_ref, half, n1_rows, emit):
    r = FFT_R

    def stage1(c, carry):
        zc = zs[pl.ds(pl.multiple_of((half * CH_HALF + c) * n1_rows, n1_rows), n1_rows), :].astype(BF16)
        a = jnp.dot(m1_ref[:, :n1_rows], zc, preferred_element_type=F32)
        ar, ai = a[:r], a[r:]
        s1[pl.ds(pl.multiple_of(c * 2 * r, 2 * r), r), :] = (ar * twr - ai * twi).astype(BF16)
        s1[pl.ds(pl.multiple_of(c * 2 * r + r, r), r), :] = (ar * twi + ai * twr).astype(BF16)
        return carry

    lax.fori_loop(0, CH_HALF, stage1, 0, unroll=CH_UNROLL)

    def stage2(j, carry):
        rows = CH_CHUNK * 2 * r
        lhs = s1[pl.ds(pl.multiple_of(j * rows, rows), rows), :]
        o = jnp.dot(lhs, m2_ref[...], preferred_element_type=F32).reshape(CH_CHUNK, 2 * r, 2 * r)
        xr = o[:, :r, :r] - o[:, r:, r:]
        xi = o[:, :r, r:] + o[:, r:, :r]
        emit(j, xr, xi)
        return carry

    lax.fori_loop(0, CH_HALF // CH_CHUNK, stage2, 0)


def _hyspec_body(k_ref, m1_ref, twr_ref, twi_ref, m2_ref, o_ref, zs, s1):
    r = FFT_R
    for n1 in range(r):
        blk = k_ref[n1 * r:(n1 + 1) * r, :]
        zs[pl.ds(n1, LANES, stride=r), :] = blk.T
    twr = twr_ref[...]
    twi = twi_ref[...]
    for half in range(LANES // CH_HALF):
        def emit(j, xr, xi, half=half):
            c0 = pl.multiple_of(half * CH_HALF + j * CH_CHUNK, CH_CHUNK)
            o_ref[pl.ds(c0, CH_CHUNK)] = jnp.concatenate([xr, xi], axis=-1).astype(o_ref.dtype)

        _dft_forward(zs, s1, m1_ref, twr, twi, m2_ref, half, r, emit)


def hy_spectrum(kt, consts):
    rows, c = kt.shape
    r = FFT_R
    full2 = lambda i: (0, 0)
    return pl.pallas_call(
        _hyspec_body,
        grid=(c // LANES,),
        in_specs=[
            pl.BlockSpec((rows, LANES), lambda i: (0, i)),
            pl.BlockSpec((2 * r, r), full2),
            pl.BlockSpec((r, r), full2),
            pl.BlockSpec((r, r), full2),
            pl.BlockSpec((r, 2 * r), full2),
        ],
        out_specs=pl.BlockSpec((LANES, r, 2 * r), lambda i: (i, 0, 0)),
        out_shape=jax.ShapeDtypeStruct((c, r, 2 * r), BF16),
        scratch_shapes=[pltpu.VMEM((LANES * r, r), F32), pltpu.VMEM((CH_HALF * 2 * r, r), BF16)],
        compiler_params=_cparams(("parallel",)),
        name="hy_spectrum",
    )(kt, consts["m1"], consts["twr"], consts["twi"], consts["m2"])


def _hyconv_body(z_ref, x0_ref, kf_ref, m1_ref, twr_ref, twi_ref, m2_ref, g_ref, h2_ref, bias_ref,
                 o_ref, zs, s1, s2):
    r = FFT_R
    n1_rows = r // 2
    for n1 in range(n1_rows):
        blk = z_ref[0, n1 * r:(n1 + 1) * r, :].astype(F32)
        zs[pl.ds(n1, LANES, stride=n1_rows), :] = blk.T
    twr = twr_ref[...]
    twi = twi_ref[...]
    for half in range(LANES // CH_HALF):
        def emit(j, xr, xi, half=half):
            c0 = pl.multiple_of(half * CH_HALF + j * CH_CHUNK, CH_CHUNK)
            kf = kf_ref[pl.ds(c0, CH_CHUNK)].astype(F32)
            kr, ki = kf[..., :r], kf[..., r:]
            y = jnp.concatenate([xr * kr - xi * ki, xr * ki + xi * kr], axis=-1).astype(BF16)
            rows = CH_CHUNK * r
            s2[pl.ds(pl.multiple_of(j * rows, rows), rows), :] = y.reshape(rows, 2 * r)

        _dft_forward(zs, s1, m1_ref, twr, twi, m2_ref, half, n1_rows, emit)

        def inv1(j, carry):
            rows = CH_CHUNK * r
            sl = pl.ds(pl.multiple_of(j * rows, rows), rows)
            bm = jnp.dot(s2[sl, :], g_ref[...], preferred_element_type=F32).reshape(CH_CHUNK, r, 2 * r)
            br, bi = bm[..., :r], bm[..., r:]
            b2 = jnp.concatenate([br * twr + bi * twi, bi * twr - br * twi], axis=-1).astype(BF16)
            s2[sl, :] = b2.reshape(rows, 2 * r)
            return carry

        lax.fori_loop(0, CH_HALF // CH_CHUNK, inv1, 0)

        def inv2(c, carry, half=half):
            q = jnp.dot(h2_ref[...], s2[pl.ds(pl.multiple_of(c * r, r), r), :], preferred_element_type=F32)
            yc = q[:n1_rows, :r] - q[n1_rows:, r:]
            zs[pl.ds(pl.multiple_of((half * CH_HALF + c) * n1_rows, n1_rows), n1_rows), :] = yc
            return carry

        lax.fori_loop(0, CH_HALF, inv2, 0, unroll=CH_UNROLL)

    bias = bias_ref[...]
    for n1 in range(n1_rows):
        y = zs[pl.ds(n1, LANES, stride=n1_rows), :].T
        sl = slice(n1 * r, (n1 + 1) * r)
        zz = z_ref[0, sl, :].astype(F32)
        x0 = x0_ref[0, sl, :].astype(F32)
        o_ref[0, sl, :] = ((y + zz * bias) * x0).astype(o_ref.dtype)


def hy_conv(z, x0c, kf, hy_bias, consts):
    b, l, c = z.shape
    r = FFT_R
    full2 = lambda i, j: (0, 0)
    return pl.pallas_call(
        _hyconv_body,
        grid=(c // LANES, b),
        in_specs=[
            pl.BlockSpec((1, l, LANES), lambda j, i: (i, 0, j)),
            pl.BlockSpec((1, l, LANES), lambda j, i: (i, 0, j)),
            pl.BlockSpec((LANES, r, 2 * r), lambda j, i: (j, 0, 0)),
            pl.BlockSpec((2 * r, r), full2),
            pl.BlockSpec((r, r), full2),
            pl.BlockSpec((r, r), full2),
            pl.BlockSpec((r, 2 * r), full2),
            pl.BlockSpec((2 * r, 2 * r), full2),
            pl.BlockSpec((r, r), full2),
            pl.BlockSpec((1, LANES), lambda j, i: (0, j)),
        ],
        out_specs=pl.BlockSpec((1, l, LANES), lambda j, i: (i, 0, j)),
        out_shape=jax.ShapeDtypeStruct((b, l, c), BF16),
        scratch_shapes=[
            pltpu.VMEM((LANES * r // 2, r), F32),
            pltpu.VMEM((CH_HALF * 2 * r, r), BF16),
            pltpu.VMEM((CH_HALF * r, 2 * r), BF16),
        ],
        compiler_params=_cparams(("parallel", "parallel")),
        name="hy_conv",
    )(z, x0c, kf, consts["m1"], consts["twr"], consts["twi"], consts["m2"], consts["g"], consts["h2"], hy_bias)


DEN_ROWS = 16


def _rope(x, c, s):
    return x * c + pltpu.roll(x, HEAD_DIM // 2, axis=1) * s


def _attn_body(q_ref, k_ref, v_ref, cos_ref, sin_ref, gq_ref, gk_ref, o_ref,
               kn_scr, vt_scr, qt_scr, sa_scr, sb_scr, m_scr, acc_scr, *, tq, tk, l):
    qi = pl.program_id(2)
    nk = l // tk

    @pl.when(qi == 0)
    def _():
        for r in range(nk):
            sl = slice(r * tk, (r + 1) * tk)
            kn = _rms(k_ref[0, sl, :].astype(F32), gk_ref[...])
            kn_scr[sl, :] = _rope(kn, cos_ref[sl, :], sin_ref[sl, :]).astype(BF16)
            vt_scr[r, :HEAD_DIM, :] = v_ref[0, sl, :].astype(F32).T.astype(BF16)
            vt_scr[r, HEAD_DIM:, :] = jnp.ones((DEN_ROWS, tk), BF16)

    row0 = pl.multiple_of(qi * tq, tq)
    c = cos_ref[pl.ds(row0, tq), :]
    s = sin_ref[pl.ds(row0, tq), :]
    scale = HEAD_DIM ** -0.5 * math.log2(math.e)
    for g in range(GROUP):
        qn = _rms(q_ref[0, :, g * HEAD_DIM:(g + 1) * HEAD_DIM].astype(F32), gq_ref[...])
        qt_scr[:, g * tq:(g + 1) * tq] = (_rope(qn, c, s) * scale).T.astype(BF16)

    m_scr[...] = jnp.full(m_scr.shape, -jnp.inf, F32)
    acc_scr[...] = jnp.zeros(acc_scr.shape, F32)
    nq = GROUP * tq

    def scores(j):
        kc = kn_scr[pl.ds(pl.multiple_of(j * tk, tk), tk), :]
        return jnp.dot(kc, qt_scr[...], preferred_element_type=F32)

    sa_scr[...] = scores(0)

    def half_step(j, cur, nxt):
        nxt[...] = scores(jnp.minimum(j + 1, nk - 1))
        st = cur[...]
        m_prev = m_scr[...]
        m_new = jnp.maximum(m_prev, jnp.max(st, axis=0, keepdims=True))
        alpha = jnp.exp2(m_prev - m_new)
        p = jnp.exp2(st - m_new)
        acc_scr[...] = alpha * acc_scr[...] + jnp.dot(vt_scr[j], p.astype(BF16), preferred_element_type=F32)
        m_scr[...] = m_new

    def kv_pair(jj, carry):
        half_step(2 * jj, sa_scr, sb_scr)
        half_step(2 * jj + 1, sb_scr, sa_scr)
        return carry

    lax.fori_loop(0, nk // 2, kv_pair, 0)
    o = acc_scr[:HEAD_DIM, :] / acc_scr[HEAD_DIM:HEAD_DIM + 1, :]
    for g in range(GROUP):
        o_ref[0, :, g * HEAD_DIM:(g + 1) * HEAD_DIM] = o[:, g * tq:(g + 1) * tq].T.astype(o_ref.dtype)


def attention(proj3, cos2, sin2, g_q, g_k, q_col, k_col, v_col, n_kv, tq=256, tk=512):
    b, l, _ = proj3.shape
    gw = GROUP * HEAD_DIM
    body = functools.partial(_attn_body, tq=tq, tk=tk, l=l)
    qb, kb, vb = q_col // gw, k_col // HEAD_DIM, v_col // HEAD_DIM
    return pl.pallas_call(
        body,
        grid=(b, n_kv, l // tq),
        in_specs=[
            pl.BlockSpec((1, tq, gw), lambda i, h, q: (i, q, qb + h)),
            pl.BlockSpec((1, l, HEAD_DIM), lambda i, h, q: (i, 0, kb + h)),
            pl.BlockSpec((1, l, HEAD_DIM), lambda i, h, q: (i, 0, vb + h)),
            pl.BlockSpec((l, HEAD_DIM), lambda i, h, q: (0, 0)),
            pl.BlockSpec((l, HEAD_DIM), lambda i, h, q: (0, 0)),
            pl.BlockSpec((1, HEAD_DIM), lambda i, h, q: (0, 0)),
            pl.BlockSpec((1, HEAD_DIM), lambda i, h, q: (0, 0)),
        ],
        out_specs=pl.BlockSpec((1, tq, gw), lambda i, h, q: (i, q, h)),
        out_shape=jax.ShapeDtypeStruct((b, l, n_kv * gw), BF16),
        scratch_shapes=[
            pltpu.VMEM((l, HEAD_DIM), BF16),
            pltpu.VMEM((l // tk, HEAD_DIM + DEN_ROWS, tk), BF16),
            pltpu.VMEM((HEAD_DIM, GROUP * tq), BF16),
            pltpu.VMEM((tk, GROUP * tq), F32),
            pltpu.VMEM((tk, GROUP * tq), F32),
            pltpu.VMEM((1, GROUP * tq), F32),
            pltpu.VMEM((HEAD_DIM + DEN_ROWS, GROUP * tq), F32),
        ],
        compiler_params=_cparams(("parallel", "parallel", "arbitrary")),
        name="attention",
    )(proj3, proj3, proj3, cos2, sin2, g_q, g_k)


def _merge_body(yh_ref, ya_ref, gh_ref, ga_ref, ha_ref, hb_ref, whb_ref, wab_ref, wo_ref, gf_ref,
                wrh_ref, wrl_ref, br_ref, h1_ref, xn_ref, meta_ref, slott_ref, tw_ref, cnt_ref, *, tm, nba):
    a = jnp.dot(yh_ref[...], whb_ref[...], preferred_element_type=F32)
    bmat = jnp.dot(ya_ref[...], wab_ref[...], preferred_element_type=F32)
    mix = jax.nn.sigmoid(gh_ref[...].astype(F32)) * a + jax.nn.sigmoid(ga_ref[...].astype(F32)) * bmat
    h0 = jnp.where(pl.program_id(0) < nba, ha_ref[...], hb_ref[...])
    h1 = h0 + jnp.dot(mix.astype(BF16), wo_ref[...], preferred_element_type=F32)
    h1_ref[...] = h1
    xn = _rms(h1, gf_ref[...])
    xn_ref[...] = xn.astype(xn_ref.dtype)

    x_hi = xn.astype(BF16)
    x_lo = (xn - x_hi.astype(F32)).astype(BF16)
    logits = (jnp.dot(x_hi, wrh_ref[...], preferred_element_type=F32)
              + jnp.dot(x_lo, wrh_ref[...], preferred_element_type=F32)
              + jnp.dot(x_hi, wrl_ref[...], preferred_element_type=F32)) + br_ref[...]
    lane = lax.broadcasted_iota(I32, logits.shape, 1)
    work = logits
    vals, idxs = [], []
    sel = jnp.zeros(logits.shape, F32)
    for _ in range(TOP_K):
        m = jnp.max(work, axis=-1, keepdims=True)
        idx = jnp.min(jnp.where(work == m, lane, LANES), axis=-1, keepdims=True)
        hit = lane == idx
        vals.append(m)
        idxs.append(idx)
        sel = sel + hit.astype(F32)
        work = jnp.where(hit, -jnp.inf, work)
    ex = [jnp.exp(v - vals[0]) for v in vals]
    den = ex[0] + ex[1] + ex[2] + ex[3]

    rr = lax.broadcasted_iota(I32, (tm, tm), 0)
    cc = lax.broadcasted_iota(I32, (tm, tm), 1)
    tri = (cc < rr).astype(BF16)
    prefix = jnp.dot(tri, sel.astype(BF16), preferred_element_type=F32)
    units = jnp.floor((jnp.sum(sel, axis=0, keepdims=True) + (RUN_ALIGN - 1)) * (1.0 / RUN_ALIGN))
    cnt_ref[...] = jnp.broadcast_to(units * RUN_ALIGN, cnt_ref.shape)
    er = lax.broadcasted_iota(I32, (LANES, LANES), 0)
    ec = lax.broadcasted_iota(I32, (LANES, LANES), 1)
    upper = (er < ec).astype(BF16)
    eoff = RUN_ALIGN * jnp.dot(jnp.broadcast_to(units, (8, LANES)).astype(BF16), upper,
                               preferred_element_type=F32)[0:1, :]
    slot_of = prefix + eoff

    meta = jnp.zeros(logits.shape, F32)
    tw = jnp.zeros(logits.shape, F32)
    for k in range(TOP_K):
        slot = jnp.sum(jnp.where(lane == idxs[k], slot_of, 0.0), axis=-1, keepdims=True)
        meta = jnp.where(lane == k, slot, meta)
        tw = jnp.where(lane == k, ex[k] / den, tw)
    meta = jnp.where(lane < TOP_K, meta, -1.0)
    meta_ref[...] = meta.astype(I32)
    slott_ref[0] = meta.T[0:8, :].astype(I32)
    tw_ref[...] = tw


def merge_router(y_hy, y_at, proj, xa, xb, whb, wab, wo, g_ffn, wr_hi, wr_lo, br_pad, gate_col, tm):
    d = xa.shape[1]
    n = xa.shape[0] + xb.shape[0]
    nba = xa.shape[0] // tm
    ch = y_hy.shape[1]
    ca = y_at.shape[1]
    gb = gate_col // d
    body = functools.partial(_merge_body, tm=tm, nba=nba)
    full = lambda i: (0, 0)
    row = lambda i: (i, 0)
    return pl.pallas_call(
        body,
        grid=(n // tm,),
        in_specs=[
            pl.BlockSpec((tm, ch), row),
            pl.BlockSpec((tm, ca), row),
            pl.BlockSpec((tm, d), lambda i: (i, gb)),
            pl.BlockSpec((tm, d), lambda i: (i, gb + 1)),
        ] + _pair_specs(tm, d, nba) + [
            pl.BlockSpec((ch, d), full),
            pl.BlockSpec((ca, d), full),
            pl.BlockSpec((d, d), full),
            pl.BlockSpec((1, d), full),
            pl.BlockSpec((d, LANES), full),
            pl.BlockSpec((d, LANES), full),
            pl.BlockSpec((1, LANES), full),
        ],
        out_specs=[
            pl.BlockSpec((tm, d), row),
            pl.BlockSpec((tm, d), row),
            pl.BlockSpec((tm, LANES), row),
            pl.BlockSpec((1, 8, tm), lambda i: (i, 0, 0)),
            pl.BlockSpec((tm, LANES), row),
            pl.BlockSpec((8, LANES), row),
        ],
        out_shape=[
            jax.ShapeDtypeStruct((n, d), F32),
            jax.ShapeDtypeStruct((n, d), BF16),
            jax.ShapeDtypeStruct((n, LANES), I32),
            jax.ShapeDtypeStruct((n // tm, 8, tm), I32),
            jax.ShapeDtypeStruct((n, LANES), F32),
            jax.ShapeDtypeStruct((n // tm * 8, LANES), F32),
        ],
        compiler_params=_cparams(("parallel",)),
        name="merge_router",
    )(y_hy, y_at, proj, proj, xa, xb, whb, wab, wo, g_ffn, wr_hi, wr_lo, br_pad)


RUN_ALIGN = 8
RUN_BITS = ROUTE_BLOCK.bit_length()
SORT_ROWS = TOP_K * ROUTE_BLOCK + N_EXPERTS * RUN_ALIGN


def _run_copies(i, cnt_ref, dst_ref, make_copy, op):
    def per_expert(e, local):
        c = cnt_ref[i * N_EXPERTS + e]
        d = dst_ref[i * N_EXPERTS + e]
        for bit in range(RUN_ALIGN.bit_length() - 1, RUN_BITS):
            size = 1 << bit

            @pl.when((c & size) != 0)
            def _(size=size):
                done = c & (size - 1)
                op(make_copy(pl.multiple_of(local + done, RUN_ALIGN), pl.multiple_of(d + done, RUN_ALIGN), size))

        return local + c

    lax.fori_loop(0, N_EXPERTS, per_expert, 0)


def _dispatch_body(cnt_ref, dst_ref, pend_ref, padded_ref, nused_ref, slot_ref, xn_ref, xs_hbm,
                   sort_scr, zero_scr, sems, zsem, *, tm, nblk):
    i = pl.program_id(0)
    last = pl.num_programs(0) - 1
    buf = i % 2

    @pl.when(i == 0)
    def _():
        zero_scr[...] = jnp.zeros(zero_scr.shape, F32)

        def zero_block(start):
            cp = pltpu.make_async_copy(zero_scr, xs_hbm.at[pl.ds(start, EXPERT_BLOCK), :], zsem)
            cp.start()
            cp.wait()

        def pad_rows(e, carry):
            @pl.when(padded_ref[e] > 0)
            def _():
                zero_block(pl.multiple_of(pend_ref[e] - EXPERT_BLOCK, EXPERT_BLOCK))

            return carry

        lax.fori_loop(0, N_EXPERTS, pad_rows, 0)

        def tail(bk, carry):
            zero_block(pl.multiple_of(bk * EXPERT_BLOCK, EXPERT_BLOCK))
            return carry

        lax.fori_loop(nused_ref[0], nblk, tail, 0)

    slots = slot_ref[0]
    j = lax.broadcasted_iota(I32, (SORT_ROWS, tm), 0)
    perm = jnp.zeros((SORT_ROWS, tm), F32)
    for k in range(TOP_K):
        perm = perm + (j == slots[k:k + 1, :]).astype(F32)
    sort_scr[buf] = jnp.dot(perm.astype(BF16), xn_ref[...], preferred_element_type=F32)

    def copier(b):
        def make_copy(local, d, size):
            return pltpu.make_async_copy(sort_scr.at[b, pl.ds(local, size), :], xs_hbm.at[pl.ds(d, size), :],
                                         sems.at[b])
        return make_copy

    _run_copies(i, cnt_ref, dst_ref, copier(buf), lambda cp: cp.start())

    @pl.when(i > 0)
    def _():
        _run_copies(i - 1, cnt_ref, dst_ref, copier(1 - buf), lambda cp: cp.wait())

    @pl.when(i == last)
    def _():
        _run_copies(i, cnt_ref, dst_ref, copier(buf), lambda cp: cp.wait())


def dispatch(xn, slots_t, cnt_flat, dst_flat, pad_end, padded, nused, cap, tm):
    n, d = xn.shape
    nblk = cap // EXPERT_BLOCK
    body = functools.partial(_dispatch_body, tm=tm, nblk=nblk)
    return pl.pallas_call(
        body,
        grid_spec=pltpu.PrefetchScalarGridSpec(
            num_scalar_prefetch=5,
            grid=(n // tm,),
            in_specs=[
                pl.BlockSpec((1, 8, tm), lambda i, *_: (i, 0, 0)),
                pl.BlockSpec((tm, d), lambda i, *_: (i, 0)),
            ],
            out_specs=pl.BlockSpec(memory_space=pl.ANY),
            scratch_shapes=[
                pltpu.VMEM((2, SORT_ROWS, d), F32),
                pltpu.VMEM((EXPERT_BLOCK, d), F32),
                pltpu.SemaphoreType.DMA((2,)),
                pltpu.SemaphoreType.DMA(()),
            ],
        ),
        out_shape=jax.ShapeDtypeStruct((cap, d), F32),
        compiler_params=_cparams(("arbitrary",)),
        name="dispatch",
    )(cnt_flat, dst_flat, pad_end, padded, nused, slots_t, xn)


def _experts_body(be_ref, nu_ref, x_ref, wg_ref, bg_ref, wu_ref, bu_ref, wd_ref, bd_ref, o_ref):
    i = pl.program_id(0)
    f = pl.program_id(1)
    used = i < nu_ref[0]

    def down_proj():
        xb = x_ref[...].astype(BF16)
        g = jnp.dot(xb, wg_ref[0], preferred_element_type=F32) + bg_ref[0]
        u = jnp.dot(xb, wu_ref[0], preferred_element_type=F32) + bu_ref[0]
        g = jnp.minimum(g, SWIGLU_LIMIT)
        u = jnp.clip(u, -SWIGLU_LIMIT, SWIGLU_LIMIT)
        a = g * jax.nn.sigmoid(SWIGLU_ALPHA * g) * (u + 1.0)
        return jnp.dot(a.astype(BF16), wd_ref[0], preferred_element_type=F32)

    @pl.when(jnp.logical_and(used, f == 0))
    def _():
        o_ref[...] = down_proj() + bd_ref[0]

    @pl.when(jnp.logical_and(used, f > 0))
    def _():
        o_ref[...] = o_ref[...] + down_proj()

    @pl.when(jnp.logical_not(used))
    def _():
        o_ref[...] = jnp.zeros(o_ref.shape, F32)


def experts(xs, blk_e, nused, wg, bg, wu, bu, wd, bd, tf=1024):
    cap, d = xs.shape
    dff = wg.shape[2]
    nblk = cap // EXPERT_BLOCK
    nf = dff // tf

    def last_used(i, nu):
        return jnp.maximum(jnp.minimum(i, nu[0] - 1), 0)

    def eidx(i, be, nu):
        return be[last_used(i, nu)]

    def fidx(i, f, nu):
        return jnp.where(i < nu[0], f, nf - 1)

    return pl.pallas_call(
        _experts_body,
        grid_spec=pltpu.PrefetchScalarGridSpec(
            num_scalar_prefetch=2,
            grid=(nblk, nf),
            in_specs=[
                pl.BlockSpec((EXPERT_BLOCK, d), lambda i, f, be, nu: (last_used(i, nu), 0)),
                pl.BlockSpec((1, d, tf), lambda i, f, be, nu: (eidx(i, be, nu), 0, fidx(i, f, nu))),
                pl.BlockSpec((1, 1, tf), lambda i, f, be, nu: (eidx(i, be, nu), 0, fidx(i, f, nu))),
                pl.BlockSpec((1, d, tf), lambda i, f, be, nu: (eidx(i, be, nu), 0, fidx(i, f, nu))),
                pl.BlockSpec((1, 1, tf), lambda i, f, be, nu: (eidx(i, be, nu), 0, fidx(i, f, nu))),
                pl.BlockSpec((1, tf, d), lambda i, f, be, nu: (eidx(i, be, nu), fidx(i, f, nu), 0)),
                pl.BlockSpec((1, 1, d), lambda i, f, be, nu: (eidx(i, be, nu), 0, 0)),
            ],
            out_specs=pl.BlockSpec((EXPERT_BLOCK, d), lambda i, f, be, nu: (i, 0)),
        ),
        out_shape=jax.ShapeDtypeStruct((cap, d), F32),
        compiler_params=_cparams(("arbitrary", "arbitrary")),
        name="experts",
    )(blk_e, nused, xs, wg, bg, wu, bu, wd, bd)


def _combine_body(cnt_ref, dst_ref, slot_ref, h_ref, tw_ref, pa_ref, pb_ref, y_hbm, wpg_ref, wpp_ref, gp_ref,
                  gfin_ref, oa_ref, ob_ref, gath, sems, *, tm, nba):
    i = pl.program_id(0)
    last = pl.num_programs(0) - 1
    buf = i % 2

    def copier(b):
        def make_copy(local, d, size):
            return pltpu.make_async_copy(y_hbm.at[pl.ds(d, size), :], gath.at[b, pl.ds(local, size), :],
                                         sems.at[b])
        return make_copy

    def fetch(step, b):
        gath[b, TOP_K * tm:, :] = jnp.zeros((SORT_ROWS - TOP_K * tm, gath.shape[2]), F32)
        _run_copies(step, cnt_ref, dst_ref, copier(b), lambda cp: cp.start())

    @pl.when(i == 0)
    def _():
        fetch(0, 0)

    @pl.when(i < last)
    def _():
        fetch(i + 1, 1 - buf)

    _run_copies(i, cnt_ref, dst_ref, copier(buf), lambda cp: cp.wait())

    tw = tw_ref[...]
    slots = slot_ref[...]
    lane = lax.broadcasted_iota(I32, (tm, SORT_ROWS), 1)
    wmat = jnp.zeros((tm, SORT_ROWS), F32)
    for k in range(TOP_K):
        wmat = wmat + jnp.where(lane == slots[:, k:k + 1], tw[:, k:k + 1], 0.0)
    w_hi = wmat.astype(BF16)
    w_lo = (wmat - w_hi.astype(F32)).astype(BF16)
    rows = gath[buf].astype(BF16)
    moe = jnp.dot(w_hi, rows, preferred_element_type=F32) + jnp.dot(w_lo, rows, preferred_element_type=F32)
    h2 = h_ref[...] + moe
    a = _rms(h2, gp_ref[...]).astype(BF16)
    gate = jax.nn.sigmoid(jnp.dot(a, wpg_ref[...], preferred_element_type=F32))
    pblk = jnp.where(i < nba, pa_ref[...], pb_ref[...])
    pp = jnp.dot(pblk.astype(BF16), wpp_ref[...], preferred_element_type=F32)
    h3 = h2 + gate * pp
    res = _rms(h3, gfin_ref[...])

    @pl.when(i < nba)
    def _():
        oa_ref[...] = res

    @pl.when(i >= nba)
    def _():
        ob_ref[...] = res


def combine_ple(h1, slots, tw, pa, pb, y, cnt_flat, dst_flat, wpg, wpp, g_ple, g_final, tm):
    n, d = h1.shape
    pd = pa.shape[1]
    nba = pa.shape[0] // tm
    body = functools.partial(_combine_body, tm=tm, nba=nba)
    full = lambda i, *_: (0, 0)
    row = lambda i, *_: (i, 0)
    return pl.pallas_call(
        body,
        grid_spec=pltpu.PrefetchScalarGridSpec(
            num_scalar_prefetch=2,
            grid=(n // tm,),
            in_specs=[
                pl.BlockSpec((tm, LANES), row),
                pl.BlockSpec((tm, d), row),
                pl.BlockSpec((tm, LANES), row),
            ] + _pair_specs(tm, pd, nba) + [
                pl.BlockSpec(memory_space=pl.ANY),
                pl.BlockSpec((d, d), full, pipeline_mode=pl.Buffered(1)),
                pl.BlockSpec((pd, d), full, pipeline_mode=pl.Buffered(1)),
                pl.BlockSpec((1, d), full),
                pl.BlockSpec((1, d), full),
            ],
            out_specs=_pair_specs(tm, d, nba),
            scratch_shapes=[pltpu.VMEM((2, SORT_ROWS, d), F32), pltpu.SemaphoreType.DMA((2,))],
        ),
        out_shape=[jax.ShapeDtypeStruct((pa.shape[0], d), F32), jax.ShapeDtypeStruct((pb.shape[0], d), F32)],
        compiler_params=_cparams(("arbitrary",)),
        name="combine_ple",
    )(cnt_flat, dst_flat, slots, h1, tw, pa, pb, y, wpg, wpp, g_ple, g_final)


def _rope_tables(l):
    rows = l // GRID_W
    row = jnp.repeat(jnp.arange(rows, dtype=F32), GRID_W)
    col = jnp.tile(jnp.arange(GRID_W, dtype=F32), rows)
    axis_dim = HEAD_DIM // 2
    inv = 1.0 / (ROPE_THETA ** (jnp.arange(0, axis_dim, 2, dtype=F32) / axis_dim))
    ang = jnp.concatenate([row[:, None] * inv, col[:, None] * inv], axis=-1)
    c, s = jnp.cos(ang), jnp.sin(ang)
    return jnp.concatenate([c, c], axis=-1), jnp.concatenate([-s, s], axis=-1)


def _filter_features(l):
    t = jnp.linspace(0.0, 1.0, l, dtype=F32)[:, None]
    w = 2.0 * math.pi * jnp.arange(l, dtype=F32)[:, None] / l
    f = jnp.linspace(1e-4, FILTER_BANDS - 1, FILTER_BANDS, dtype=F32)[None, :]
    z = jnp.concatenate([t, jnp.cos(f * w), -jnp.sin(f * w)], axis=-1)
    pos = jnp.concatenate([jnp.arange(l), jnp.zeros((1,), jnp.int32), jnp.arange(l - 1, 0, -1)])
    mask = jnp.ones((2 * l, 1), F32).at[l, 0].set(0.0)
    z2 = jnp.concatenate([z[pos], mask], axis=-1)
    return jnp.pad(z2, ((0, 0), (0, LANES - z2.shape[1])))


def kernel(x_prompt, x_sample, p_prompt, p_sample, g_mix, w_in, w_short, b_short, w_f1, b_f1, w_f2, b_f2,
           w_f3, b_f3, w_f4, filter_freq, hy_bias, g_q, g_k, w_hy_br, w_at_br, w_out, g_ffn, w_router,
           b_router, w_gate, b_gate, w_up, b_up, w_down, b_down, g_ple, w_ple_gate, w_ple_proj, g_final):
    assert w_in.shape[0] == 1, "single layer"
    l, d = x_prompt.shape[1], x_prompt.shape[2]
    c = w_hy_br.shape[1]
    aw = w_at_br.shape[1]
    n_kv = aw // (GROUP * HEAD_DIM)
    kvw = n_kv * HEAD_DIM
    assert 2 * l == FFT_R * FFT_R and x_sample.shape[1] == l

    ba, bb = x_prompt.shape[0], x_sample.shape[0]
    b = ba + bb
    n = b * l
    xa, xb = x_prompt.reshape(ba * l, d), x_sample.reshape(bb * l, d)
    pa, pb = p_prompt[0].reshape(ba * l, -1), p_sample[0].reshape(bb * l, -1)

    o_q, o_k, o_v, o_g = 3 * c, 3 * c + aw, 3 * c + aw + kvw, 3 * c + aw + 2 * kvw
    wi = w_in[0]
    w_perm = jnp.concatenate([wi[:, :o_k], wi[:, o_g:], wi[:, o_k:o_g]], axis=1).astype(BF16)
    w_perm = jnp.pad(w_perm, ((0, 0), (0, -w_perm.shape[1] % INPROJ_TN)))
    q_col, gate_col = o_q, o_k
    k_col = gate_col + 2 * d
    v_col = k_col + kvw

    proj = in_proj(xa, xb, g_mix, w_perm, tn=INPROJ_TN)
    proj3 = proj.reshape(b, l, proj.shape[1])

    consts = _dft_constants()
    z, x0c = hy_prep(proj3, w_short[0], b_short, c)
    w1p = jnp.pad(w_f1[0], ((0, LANES - FILTER_EMB), (0, 0)))
    deltas = jnp.linspace(math.log(DECAY_TARGET) / DECAY_SLOW, math.log(DECAY_TARGET) / DECAY_FAST, c, dtype=F32)
    kt = hy_filter(_filter_features(l), w1p, b_f1, w_f2[0], b_f2, w_f3[0], b_f3, filter_freq, w_f4[0],
                   jnp.abs(deltas)[None, :], c)
    kf = hy_spectrum(kt, consts)
    y_hy = hy_conv(z, x0c, kf, hy_bias, consts)

    cos2, sin2 = _rope_tables(l)
    y_at = attention(proj3, cos2, sin2, g_q, g_k, q_col, k_col, v_col, n_kv)

    oa, ob = _merge_moe_ple(y_hy.reshape(n, c), y_at.reshape(n, aw), proj, gate_col, xa, xb, pa, pb,
                            w_hy_br[0], w_at_br[0], w_out[0], g_ffn, w_router[0], b_router, w_gate[0], b_gate[0],
                            w_up[0], b_up[0], w_down[0], b_down[0], g_ple, w_ple_gate[0], w_ple_proj[0], g_final)
    return oa.reshape(ba, l, d), ob.reshape(bb, l, d)


def _merge_moe_ple(y_hy, y_at, proj, gate_col, xa, xb, pa, pb, w_hy_br, w_at_br, w_out, g_ffn, w_router, b_router,
                   w_gate, b_gate, w_up, b_up, w_down, b_down, g_ple, w_ple_gate, w_ple_proj, g_final):
    n = xa.shape[0] + xb.shape[0]
    wr_pad = jnp.pad(w_router, ((0, 0), (0, LANES - N_EXPERTS)))
    wr_hi = wr_pad.astype(BF16)
    wr_lo = (wr_pad - wr_hi.astype(F32)).astype(BF16)
    br_pad = jnp.pad(b_router, ((0, 0), (0, LANES - N_EXPERTS)), constant_values=-1e30)
    h1, xn, slots, slots_t, tw, cnt = merge_router(
        y_hy, y_at, proj, xa, xb, w_hy_br.astype(BF16), w_at_br.astype(BF16),
        w_out.astype(BF16), g_ffn, wr_hi, wr_lo, br_pad, gate_col, ROUTE_BLOCK)

    nrb = n // ROUTE_BLOCK
    cnt_be = cnt.reshape(nrb, 8, LANES)[:, 0, :N_EXPERTS].astype(I32)
    counts = jnp.sum(cnt_be, axis=0)
    padded = (counts + EXPERT_BLOCK - 1) // EXPERT_BLOCK * EXPERT_BLOCK
    pad_end = jnp.cumsum(padded).astype(I32)
    pad_start = pad_end - padded
    before = jnp.cumsum(cnt_be, axis=0) - cnt_be
    dst_be = pad_start[None, :] + before
    cap = nrb * SORT_ROWS + N_EXPERTS * EXPERT_BLOCK
    nblk = cap // EXPERT_BLOCK
    nused = (pad_end[-1:] // EXPERT_BLOCK).astype(I32)
    blk_start = jnp.arange(nblk, dtype=I32) * EXPERT_BLOCK
    blk_e = jnp.minimum(jnp.sum((pad_end[None, :] <= blk_start[:, None]).astype(I32), axis=1), N_EXPERTS - 1)
    cnt_flat = cnt_be.reshape(-1)
    dst_flat = dst_be.reshape(-1).astype(I32)

    xs = dispatch(xn, slots_t, cnt_flat, dst_flat, pad_end, padded, nused, cap, ROUTE_BLOCK)
    ys = experts(xs, blk_e, nused, w_gate.astype(BF16), b_gate[:, None, :], w_up.astype(BF16),
                 b_up[:, None, :], w_down.astype(BF16), b_down[:, None, :])
    return combine_ple(h1, slots, tw, pa, pb, ys, cnt_flat, dst_flat, w_ple_gate.astype(BF16),
                       w_ple_proj.astype(BF16), g_ple, g_final[None, :], ROUTE_BLOCK)
```

```python
import functools
import math

import jax
import jax.numpy as jnp
import numpy as np
from jax import lax
from jax.experimental import pallas as pl
from jax.experimental.pallas import tpu as pltpu

F32 = jnp.float32
BF16 = jnp.bfloat16
I32 = jnp.int32
U32 = jnp.uint32

EPS = 1e-6
HEAD_DIM = 128
GROUP = 4
GRID_W = 64
ROPE_THETA = 10000.0
HY_SHORT = 3
FILTER_EMB = 33
FILTER_BANDS = 16
DECAY_FAST = 0.3
DECAY_SLOW = 1.5
DECAY_TARGET = 1e-2
N_EXPERTS = 32
TOP_K = 4
SWIGLU_LIMIT = 7.0
SWIGLU_ALPHA = 1.702

LANES = 128
FFT_R = 128
EXPERT_BLOCK = 512
ROUTE_BLOCK = 256
INPROJ_TN = 1024
VMEM_LIMIT = 56 * 1024 * 1024


def _cparams(sem, vmem=VMEM_LIMIT):
    return pltpu.CompilerParams(dimension_semantics=sem, vmem_limit_bytes=vmem)


def _rms(x, g):
    return x * lax.rsqrt(jnp.mean(x * x, axis=-1, keepdims=True) + EPS) * g


def _pair_specs(tm, width, nba, **kw):
    return [
        pl.BlockSpec((tm, width), lambda i, *_: (jnp.minimum(i, nba - 1), 0), **kw),
        pl.BlockSpec((tm, width), lambda i, *_: (jnp.maximum(i - nba, 0), 0), **kw),
    ]


def _inproj_body(xa_ref, xb_ref, g_ref, w_ref, o_ref, a_scr, *, nba):
    @pl.when(pl.program_id(1) == 0)
    def _():
        x = jnp.where(pl.program_id(0) < nba, xa_ref[...], xb_ref[...])
        a_scr[...] = _rms(x, g_ref[...]).astype(BF16)

    o_ref[...] = jnp.dot(a_scr[...], w_ref[...], preferred_element_type=F32).astype(o_ref.dtype)


def in_proj(xa, xb, g, w_bf16, tm=1024, tn=1024):
    d = xa.shape[1]
    n = xa.shape[0] + xb.shape[0]
    nba = xa.shape[0] // tm
    nout = w_bf16.shape[1]
    return pl.pallas_call(
        functools.partial(_inproj_body, nba=nba),
        grid=(n // tm, nout // tn),
        in_specs=_pair_specs(tm, d, nba, pipeline_mode=pl.Buffered(1)) + [
            pl.BlockSpec((1, d), lambda i, j: (0, 0)),
            pl.BlockSpec((d, tn), lambda i, j: (0, j)),
        ],
        out_specs=pl.BlockSpec((tm, tn), lambda i, j: (i, j)),
        out_shape=jax.ShapeDtypeStruct((n, nout), BF16),
        scratch_shapes=[pltpu.VMEM((tm, d), BF16)],
        compiler_params=_cparams(("parallel", "arbitrary")),
        name="in_proj",
    )(xa, xb, g, w_bf16)


def _hyprep_body(u_ref, up_ref, un_ref, w_ref, b_ref, z_ref, x0_ref, *, tr, c):
    r = pl.program_id(1)
    nr = pl.num_programs(1)
    u = u_ref[0].astype(F32)
    hp = up_ref[0][15:16, :].astype(F32)
    hn = un_ref[0][0:1, :].astype(F32)
    hp = jnp.where(r == 0, 0.0, hp)
    hn = jnp.where(r == nr - 1, 0.0, hn)
    row = lax.broadcasted_iota(I32, u.shape, 0)
    prev = jnp.where(row == 0, hp, pltpu.roll(u, 1, axis=0))
    nxt = jnp.where(row == tr - 1, hn, pltpu.roll(u, tr - 1, axis=0))
    w = w_ref[...]
    uc = b_ref[...] + prev * w[0:1] + u * w[1:2] + nxt * w[2:3]
    x0 = uc[:, :c]
    x1 = uc[:, c:2 * c]
    v = uc[:, 2 * c:]
    z_ref[0] = (v * x1).astype(z_ref.dtype)
    x0_ref[0] = x0.astype(x0_ref.dtype)


def hy_prep(proj3, w_short, b_short, c, tr=256):
    b, l, _ = proj3.shape
    hb = tr // 16
    nh = l // 16
    body = functools.partial(_hyprep_body, tr=tr, c=c)
    return pl.pallas_call(
        body,
        grid=(b, l // tr),
        in_specs=[
            pl.BlockSpec((1, tr, 3 * c), lambda i, r: (i, r, 0)),
            pl.BlockSpec((1, 16, 3 * c), lambda i, r: (i, jnp.maximum(r * hb - 1, 0), 0)),
            pl.BlockSpec((1, 16, 3 * c), lambda i, r: (i, jnp.minimum((r + 1) * hb, nh - 1), 0)),
            pl.BlockSpec((HY_SHORT, 3 * c), lambda i, r: (0, 0)),
            pl.BlockSpec((1, 3 * c), lambda i, r: (0, 0)),
        ],
        out_specs=[
            pl.BlockSpec((1, tr, c), lambda i, r: (i, r, 0)),
            pl.BlockSpec((1, tr, c), lambda i, r: (i, r, 0)),
        ],
        out_shape=[jax.ShapeDtypeStruct((b, l, c), BF16), jax.ShapeDtypeStruct((b, l, c), BF16)],
        compiler_params=_cparams(("parallel", "parallel")),
        name="hy_prep",
    )(proj3, proj3, proj3, w_short, b_short)


def _hyfilter_body(z_ref, w1_ref, b1_ref, w2_ref, b2_ref, w3_ref, b3_ref, fr_ref, w4_ref, dl_ref, o_ref):
    hi = lax.Precision.HIGHEST
    zb = z_ref[...]
    fr = fr_ref[...]
    h = jnp.sin(fr * (jnp.dot(zb, w1_ref[...], precision=hi, preferred_element_type=F32) + b1_ref[...]))
    h = jnp.sin(fr * (jnp.dot(h, w2_ref[...], precision=hi, preferred_element_type=F32) + b2_ref[...]))
    h = jnp.sin(fr * (jnp.dot(h, w3_ref[...], precision=hi, preferred_element_type=F32) + b3_ref[...]))
    h4 = jnp.dot(h, w4_ref[...], precision=hi, preferred_element_type=F32)
    t = zb[:, 0:1]
    mask = zb[:, FILTER_EMB:FILTER_EMB + 1]
    o_ref[...] = h4 * jnp.exp(-t * dl_ref[...]) * mask


def hy_filter(zfeat, w1p, b1, w2, b2, w3, b3, freq, w4, absdelta, c, tr=1024):
    rows = zfeat.shape[0]
    half_blocks = rows // 2 // tr
    fo = w2.shape[0]
    full = lambda i: (0, 0)
    return pl.pallas_call(
        _hyfilter_body,
        grid=(rows // tr,),
        in_specs=[
            pl.BlockSpec((tr, LANES), lambda i: (i, 0)),
            pl.BlockSpec((LANES, fo), full),
            pl.BlockSpec((1, fo), full),
            pl.BlockSpec((fo, fo), full),
            pl.BlockSpec((1, fo), full),
            pl.BlockSpec((fo, fo), full),
            pl.BlockSpec((1, fo), full),
            pl.BlockSpec((1, fo), full),
            pl.BlockSpec((fo, c), lambda i: (0, i // half_blocks)),
            pl.BlockSpec((1, c), full),
        ],
        out_specs=pl.BlockSpec((tr, c), lambda i: (i, 0)),
        out_shape=jax.ShapeDtypeStruct((rows, c), F32),
        compiler_params=_cparams(("parallel",)),
        name="hy_filter",
    )(zfeat, w1p, b1, w2, b2, w3, b3, freq, w4, absdelta)


def _dft_constants():
    r = FFT_R
    n = r * r
    k = np.arange(r)
    ang = -2.0 * np.pi * np.outer(k, k) / r
    fr, fi = np.cos(ang), np.sin(ang)
    m1 = np.concatenate([fr, fi], axis=0)
    tw = -2.0 * np.pi * np.outer(k, k) / n
    m2 = np.concatenate([fr, fi], axis=1)
    gr, gi = fr, -fi
    g = np.block([[gr, gi], [-gi, gr]])
    h2 = np.concatenate([gr[: r // 2], gi[: r // 2]], axis=0) / n
    return dict(
        m1=jnp.asarray(m1, BF16), twr=jnp.asarray(np.cos(tw), F32), twi=jnp.asarray(np.sin(tw), F32),
        m2=jnp.asarray(m2, BF16), g=jnp.asarray(g, BF16), h2=jnp.asarray(h2, BF16))


CH_HALF = 64
CH_CHUNK = 8
CH_UNROLL = 8


def _dft_forward(zs, s1, m1_ref, twr, twi, m2_ref, half, n1_rows, emit):
    r = FFT_R

    def stage1(c, carry):
        zc = zs[pl.ds(pl.multiple_of((half * CH_HALF + c) * n1_rows, n1_rows), n1_rows), :].astype(BF16)
        a = jnp.dot(m1_ref[:, :n1_rows], zc, preferred_element_type=F32)
        ar, ai = a[:r], a[r:]
        s1[pl.ds(pl.multiple_of(c * 2 * r, 2 * r), r), :] = (ar * twr - ai * twi).astype(BF16)
        s1[pl.ds(pl.multiple_of(c * 2 * r + r, r), r), :] = (ar * twi + ai * twr).astype(BF16)
        return carry

    lax.fori_loop(0, CH_HALF, stage1, 0, unroll=CH_UNROLL)

    def stage2(j, carry):
        rows = CH_CHUNK * 2 * r
        lhs = s1[pl.ds(pl.multiple_of(j * rows, rows), rows), :]
        o = jnp.dot(lhs, m2_ref[...], preferred_element_type=F32).reshape(CH_CHUNK, 2 * r, 2 * r)
        xr = o[:, :r, :r] - o[:, r:, r:]
        xi = o[:, :r, r:] + o[:, r:, :r]
        emit(j, xr, xi)
        return carry

    lax.fori_loop(0, CH_HALF // CH_CHUNK, stage2, 0)


def _hyspec_body(k_ref, m1_ref, twr_ref, twi_ref, m2_ref, o_ref, zs, s1):
    r = FFT_R
    for n1 in range(r):
        blk = k_ref[n1 * r:(n1 + 1) * r, :]
        zs[pl.ds(n1, LANES, stride=r), :] = blk.T
    twr = twr_ref[...]
    twi = twi_ref[...]
    for half in range(LANES // CH_HALF):
        def emit(j, xr, xi, half=half):
            c0 = pl.multiple_of(half * CH_HALF + j * CH_CHUNK, CH_CHUNK)
            o_ref[pl.ds(c0, CH_CHUNK)] = jnp.concatenate([xr, xi], axis=-1).astype(o_ref.dtype)

        _dft_forward(zs, s1, m1_ref, twr, twi, m2_ref, half, r, emit)


def hy_spectrum(kt, consts):
    rows, c = kt.shape
    r = FFT_R
    full2 = lambda i: (0, 0)
    return pl.pallas_call(
        _hyspec_body,
        grid=(c // LANES,),
        in_specs=[
            pl.BlockSpec((rows, LANES), lambda i: (0, i)),
            pl.BlockSpec((2 * r, r), full2),
            pl.BlockSpec((r, r), full2),
            pl.BlockSpec((r, r), full2),
            pl.BlockSpec((r, 2 * r), full2),
        ],
        out_specs=pl.BlockSpec((LANES, r, 2 * r), lambda i: (i, 0, 0)),
        out_shape=jax.ShapeDtypeStruct((c, r, 2 * r), BF16),
        scratch_shapes=[pltpu.VMEM((LANES * r, r), F32), pltpu.VMEM((CH_HALF * 2 * r, r), BF16)],
        compiler_params=_cparams(("parallel",)),
        name="hy_spectrum",
    )(kt, consts["m1"], consts["twr"], consts["twi"], consts["m2"])


def _hyconv_body(z_ref, x0_ref, kf_ref, m1_ref, twr_ref, twi_ref, m2_ref, g_ref, h2_ref, bias_ref,
                 o_ref, zs, s1, s2):
    r = FFT_R
    n1_rows = r // 2
    for n1 in range(n1_rows):
        blk = z_ref[0, n1 * r:(n1 + 1) * r, :].astype(F32)
        zs[pl.ds(n1, LANES, stride=n1_rows), :] = blk.T
    twr = twr_ref[...]
    twi = twi_ref[...]
    for half in range(LANES // CH_HALF):
        def emit(j, xr, xi, half=half):
            c0 = pl.multiple_of(half * CH_HALF + j * CH_CHUNK, CH_CHUNK)
            kf = kf_ref[pl.ds(c0, CH_CHUNK)].astype(F32)
            kr, ki = kf[..., :r], kf[..., r:]
            y = jnp.concatenate([xr * kr - xi * ki, xr * ki + xi * kr], axis=-1).astype(BF16)
            rows = CH_CHUNK * r
            s2[pl.ds(pl.multiple_of(j * rows, rows), rows), :] = y.reshape(rows, 2 * r)

        _dft_forward(zs, s1, m1_ref, twr, twi, m2_ref, half, n1_rows, emit)

        def inv1(j, carry):
            rows = CH_CHUNK * r
            sl = pl.ds(pl.multiple_of(j * rows, rows), rows)
            bm = jnp.dot(s2[sl, :], g_ref[...], preferred_element_type=F32).reshape(CH_CHUNK, r, 2 * r)
            br, bi = bm[..., :r], bm[..., r:]
            b2 = jnp.concatenate([br * twr + bi * twi, bi * twr - br * twi], axis=-1).astype(BF16)
            s2[sl, :] = b2.reshape(rows, 2 * r)
            return carry

        lax.fori_loop(0, CH_HALF // CH_CHUNK, inv1, 0)

        def inv2(c, carry, half=half):
            q = jnp.dot(h2_ref[...], s2[pl.ds(pl.multiple_of(c * r, r), r), :], preferred_element_type=F32)
            yc = q[:n1_rows, :r] - q[n1_rows:, r:]
            zs[pl.ds(pl.multiple_of((half * CH_HALF + c) * n1_rows, n1_rows), n1_rows), :] = yc
            return carry

        lax.fori_loop(0, CH_HALF, inv2, 0, unroll=CH_UNROLL)

    bias = bias_ref[...]
    for n1 in range(n1_rows):
        y = zs[pl.ds(n1, LANES, stride=n1_rows), :].T
        sl = slice(n1 * r, (n1 + 1) * r)
        zz = z_ref[0, sl, :].astype(F32)
        x0 = x0_ref[0, sl, :].astype(F32)
        o_ref[0, sl, :] = ((y + zz * bias) * x0).astype(o_ref.dtype)


def hy_conv(z, x0c, kf, hy_bias, consts):
    b, l, c = z.shape
    r = FFT_R
    full2 = lambda i, j: (0, 0)
    return pl.pallas_call(
        _hyconv_body,
        grid=(c // LANES, b),
        in_specs=[
            pl.BlockSpec((1, l, LANES), lambda j, i: (i, 0, j)),
            pl.BlockSpec((1, l, LANES), lambda j, i: (i, 0, j)),
            pl.BlockSpec((LANES, r, 2 * r), lambda j, i: (j, 0, 0)),
            pl.BlockSpec((2 * r, r), full2),
            pl.BlockSpec((r, r), full2),
            pl.BlockSpec((r, r), full2),
            pl.BlockSpec((r, 2 * r), full2),
            pl.BlockSpec((2 * r, 2 * r), full2),
            pl.BlockSpec((r, r), full2),
            pl.BlockSpec((1, LANES), lambda j, i: (0, j)),
        ],
        out_specs=pl.BlockSpec((1, l, LANES), lambda j, i: (i, 0, j)),
        out_shape=jax.ShapeDtypeStruct((b, l, c), BF16),
        scratch_shapes=[
            pltpu.VMEM((LANES * r // 2, r), F32),
            pltpu.VMEM((CH_HALF * 2 * r, r), BF16),
            pltpu.VMEM((CH_HALF * r, 2 * r), BF16),
        ],
        compiler_params=_cparams(("parallel", "parallel")),
        name="hy_conv",
    )(z, x0c, kf, consts["m1"], consts["twr"], consts["twi"], consts["m2"], consts["g"], consts["h2"], hy_bias)


DEN_ROWS = 16


def _rope(x, c, s):
    return x * c + pltpu.roll(x, HEAD_DIM // 2, axis=1) * s


def _attn_body(q_ref, k_ref, v_ref, cos_ref, sin_ref, gq_ref, gk_ref, o_ref,
               kn_scr, vt_scr, qt_scr, sa_scr, sb_scr, m_scr, acc_scr, *, tq, tk, l):
    qi = pl.program_id(2)
    nk = l // tk

    @pl.when(qi == 0)
    def _():
        for r in range(nk):
            sl = slice(r * tk, (r + 1) * tk)
            kn = _rms(k_ref[0, sl, :].astype(F32), gk_ref[...])
            kn_scr[sl, :] = _rope(kn, cos_ref[sl, :], sin_ref[sl, :]).astype(BF16)
            vt_scr[r, :HEAD_DIM, :] = v_ref[0, sl, :].astype(F32).T.astype(BF16)
            vt_scr[r, HEAD_DIM:, :] = jnp.ones((DEN_ROWS, tk), BF16)

    row0 = pl.multiple_of(qi * tq, tq)
    c = cos_ref[pl.ds(row0, tq), :]
    s = sin_ref[pl.ds(row0, tq), :]
    scale = HEAD_DIM ** -0.5 * math.log2(math.e)
    for g in range(GROUP):
        qn = _rms(q_ref[0, :, g * HEAD_DIM:(g + 1) * HEAD_DIM].astype(F32), gq_ref[...])
        qt_scr[:, g * tq:(g + 1) * tq] = (_rope(qn, c, s) * scale).T.astype(BF16)

    m_scr[...] = jnp.full(m_scr.shape, -jnp.inf, F32)
    acc_scr[...] = jnp.zeros(acc_scr.shape, F32)
    nq = GROUP * tq

    def scores(j):
        kc = kn_scr[pl.ds(pl.multiple_of(j * tk, tk), tk), :]
        return jnp.dot(kc, qt_scr[...], preferred_element_type=F32)

    sa_scr[...] = scores(0)

    def half_step(j, cur, nxt):
        nxt[...] = scores(jnp.minimum(j + 1, nk - 1))
        st = cur[...]
        m_prev = m_scr[...]
        m_new = jnp.maximum(m_prev, jnp.max(st, axis=0, keepdims=True))
        alpha = jnp.exp2(m_prev - m_new)
        p = jnp.exp2(st - m_new)
        acc_scr[...] = alpha * acc_scr[...] + jnp.dot(vt_scr[j], p.astype(BF16), preferred_element_type=F32)
        m_scr[...] = m_new

    def kv_pair(jj, carry):
        half_step(2 * jj, sa_scr, sb_scr)
        half_step(2 * jj + 1, sb_scr, sa_scr)
        return carry

    lax.fori_loop(0, nk // 2, kv_pair, 0)
    o = acc_scr[:HEAD_DIM, :] / acc_scr[HEAD_DIM:HEAD_DIM + 1, :]
    for g in range(GROUP):
        o_ref[0, :, g * HEAD_DIM:(g + 1) * HEAD_DIM] = o[:, g * tq:(g + 1) * tq].T.astype(o_ref.dtype)


def attention(proj3, cos2, sin2, g_q, g_k, q_col, k_col, v_col, n_kv, tq=256, tk=1024):
    b, l, _ = proj3.shape
    gw = GROUP * HEAD_DIM
    body = functools.partial(_attn_body, tq=tq, tk=tk, l=l)
    qb, kb, vb = q_col // gw, k_col // HEAD_DIM, v_col // HEAD_DIM
    return pl.pallas_call(
        body,
        grid=(b, n_kv, l // tq),
        in_specs=[
            pl.BlockSpec((1, tq, gw), lambda i, h, q: (i, q, qb + h)),
            pl.BlockSpec((1, l, HEAD_DIM), lambda i, h, q: (i, 0, kb + h)),
            pl.BlockSpec((1, l, HEAD_DIM), lambda i, h, q: (i, 0, vb + h)),
            pl.BlockSpec((l, HEAD_DIM), lambda i, h, q: (0, 0)),
            pl.BlockSpec((l, HEAD_DIM), lambda i, h, q: (0, 0)),
            pl.BlockSpec((1, HEAD_DIM), lambda i, h, q: (0, 0)),
            pl.BlockSpec((1, HEAD_DIM), lambda i, h, q: (0, 0)),
        ],
        out_specs=pl.BlockSpec((1, tq, gw), lambda i, h, q: (i, q, h)),
        out_shape=jax.ShapeDtypeStruct((b, l, n_kv * gw), BF16),
        scratch_shapes=[
            pltpu.VMEM((l, HEAD_DIM), BF16),
            pltpu.VMEM((l // tk, HEAD_DIM + DEN_ROWS, tk), BF16),
            pltpu.VMEM((HEAD_DIM, GROUP * tq), BF16),
            pltpu.VMEM((tk, GROUP * tq), F32),
            pltpu.VMEM((tk, GROUP * tq), F32),
            pltpu.VMEM((1, GROUP * tq), F32),
            pltpu.VMEM((HEAD_DIM + DEN_ROWS, GROUP * tq), F32),
        ],
        compiler_params=_cparams(("parallel", "parallel", "arbitrary")),
        name="attention",
    )(proj3, proj3, proj3, cos2, sin2, g_q, g_k)


def _merge_body(yh_ref, ya_ref, gh_ref, ga_ref, ha_ref, hb_ref, whb_ref, wab_ref, wo_ref, gf_ref,
                wrh_ref, wrl_ref, br_ref, h1_ref, xn_ref, meta_ref, slott_ref, tw_ref, cnt_ref, *, tm, nba):
    a = jnp.dot(yh_ref[...], whb_ref[...], preferred_element_type=F32)
    bmat = jnp.dot(ya_ref[...], wab_ref[...], preferred_element_type=F32)
    mix = jax.nn.sigmoid(gh_ref[...].astype(F32)) * a + jax.nn.sigmoid(ga_ref[...].astype(F32)) * bmat
    h0 = jnp.where(pl.program_id(0) < nba, ha_ref[...], hb_ref[...])
    h1 = h0 + jnp.dot(mix.astype(BF16), wo_ref[...], preferred_element_type=F32)
    h1_ref[...] = h1
    xn = _rms(h1, gf_ref[...])
    xn_ref[...] = xn.astype(xn_ref.dtype)

    x_hi = xn.astype(BF16)
    x_lo = (xn - x_hi.astype(F32)).astype(BF16)
    logits = (jnp.dot(x_hi, wrh_ref[...], preferred_element_type=F32)
              + jnp.dot(x_lo, wrh_ref[...], preferred_element_type=F32)
              + jnp.dot(x_hi, wrl_ref[...], preferred_element_type=F32)) + br_ref[...]
    lane = lax.broadcasted_iota(I32, logits.shape, 1)
    work = logits
    vals, idxs = [], []
    sel = jnp.zeros(logits.shape, F32)
    for _ in range(TOP_K):
        m = jnp.max(work, axis=-1, keepdims=True)
        idx = jnp.min(jnp.where(work == m, lane, LANES), axis=-1, keepdims=True)
        hit = lane == idx
        vals.append(m)
        idxs.append(idx)
        sel = sel + hit.astype(F32)
        work = jnp.where(hit, -jnp.inf, work)
    ex = [jnp.exp(v - vals[0]) for v in vals]
    den = ex[0] + ex[1] + ex[2] + ex[3]

    rr = lax.broadcasted_iota(I32, (tm, tm), 0)
    cc = lax.broadcasted_iota(I32, (tm, tm), 1)
    tri = (cc < rr).astype(BF16)
    prefix = jnp.dot(tri, sel.astype(BF16), preferred_element_type=F32)
    units = jnp.floor((jnp.sum(sel, axis=0, keepdims=True) + (RUN_ALIGN - 1)) * (1.0 / RUN_ALIGN))
    cnt_ref[...] = jnp.broadcast_to(units * RUN_ALIGN, cnt_ref.shape)
    er = lax.broadcasted_iota(I32, (LANES, LANES), 0)
    ec = lax.broadcasted_iota(I32, (LANES, LANES), 1)
    upper = (er < ec).astype(BF16)
    eoff = RUN_ALIGN * jnp.dot(jnp.broadcast_to(units, (8, LANES)).astype(BF16), upper,
                               preferred_element_type=F32)[0:1, :]
    slot_of = prefix + eoff

    meta = jnp.zeros(logits.shape, F32)
    tw = jnp.zeros(logits.shape, F32)
    for k in range(TOP_K):
        slot = jnp.sum(jnp.where(lane == idxs[k], slot_of, 0.0), axis=-1, keepdims=True)
        meta = jnp.where(lane == k, slot, meta)
        tw = jnp.where(lane == k, ex[k] / den, tw)
    meta = jnp.where(lane < TOP_K, meta, -1.0)
    meta_ref[...] = meta.astype(I32)
    slott_ref[0] = meta.T[0:8, :].astype(I32)
    tw_ref[...] = tw


def merge_router(y_hy, y_at, proj, xa, xb, whb, wab, wo, g_ffn, wr_hi, wr_lo, br_pad, gate_col, tm):
    d = xa.shape[1]
    n = xa.shape[0] + xb.shape[0]
    nba = xa.shape[0] // tm
    ch = y_hy.shape[1]
    ca = y_at.shape[1]
    gb = gate_col // d
    body = functools.partial(_merge_body, tm=tm, nba=nba)
    full = lambda i: (0, 0)
    row = lambda i: (i, 0)
    return pl.pallas_call(
        body,
        grid=(n // tm,),
        in_specs=[
            pl.BlockSpec((tm, ch), row),
            pl.BlockSpec((tm, ca), row),
            pl.BlockSpec((tm, d), lambda i: (i, gb)),
            pl.BlockSpec((tm, d), lambda i: (i, gb + 1)),
        ] + _pair_specs(tm, d, nba) + [
            pl.BlockSpec((ch, d), full),
            pl.BlockSpec((ca, d), full),
            pl.BlockSpec((d, d), full),
            pl.BlockSpec((1, d), full),
            pl.BlockSpec((d, LANES), full),
            pl.BlockSpec((d, LANES), full),
            pl.BlockSpec((1, LANES), full),
        ],
        out_specs=[
            pl.BlockSpec((tm, d), row),
            pl.BlockSpec((tm, d), row),
            pl.BlockSpec((tm, LANES), row),
            pl.BlockSpec((1, 8, tm), lambda i: (i, 0, 0)),
            pl.BlockSpec((tm, LANES), row),
            pl.BlockSpec((8, LANES), row),
        ],
        out_shape=[
            jax.ShapeDtypeStruct((n, d), F32),
            jax.ShapeDtypeStruct((n, d), BF16),
            jax.ShapeDtypeStruct((n, LANES), I32),
            jax.ShapeDtypeStruct((n // tm, 8, tm), I32),
            jax.ShapeDtypeStruct((n, LANES), F32),
            jax.ShapeDtypeStruct((n // tm * 8, LANES), F32),
        ],
        compiler_params=_cparams(("parallel",)),
        name="merge_router",
    )(y_hy, y_at, proj, proj, xa, xb, whb, wab, wo, g_ffn, wr_hi, wr_lo, br_pad)


RUN_ALIGN = 8
RUN_BITS = ROUTE_BLOCK.bit_length()
SORT_ROWS = TOP_K * ROUTE_BLOCK + N_EXPERTS * RUN_ALIGN


def _run_copies(i, cnt_ref, dst_ref, make_copy, op):
    def per_expert(e, local):
        c = cnt_ref[i * N_EXPERTS + e]
        d = dst_ref[i * N_EXPERTS + e]
        for bit in range(RUN_ALIGN.bit_length() - 1, RUN_BITS):
            size = 1 << bit

            @pl.when((c & size) != 0)
            def _(size=size):
                done = c & (size - 1)
                op(make_copy(pl.multiple_of(local + done, RUN_ALIGN), pl.multiple_of(d + done, RUN_ALIGN), size))

        return local + c

    lax.fori_loop(0, N_EXPERTS, per_expert, 0)


def _pack_bf16_pairs(x):
    half = x.shape[1] // 2
    hi = lax.bitcast_convert_type(x[:, :half], U32)
    lo = lax.bitcast_convert_type(x[:, half:], U32)
    return hi | (lo >> 16)


def _unpack_bf16_pairs(w):
    hi = lax.bitcast_convert_type(w & jnp.uint32(0xFFFF0000), F32).astype(BF16)
    lo = lax.bitcast_convert_type(w << 16, F32).astype(BF16)
    return jnp.concatenate([hi, lo], axis=1)


def _dispatch_body(cnt_ref, dst_ref, pend_ref, padded_ref, nused_ref, slot_ref, xn_ref, xs_hbm,
                   sort_scr, zero_scr, sems, zsem, *, tm, nblk):
    i = pl.program_id(0)
    last = pl.num_programs(0) - 1
    buf = i % 2

    @pl.when(i == 0)
    def _():
        zero_scr[...] = jnp.zeros(zero_scr.shape, U32)

        def zero_block(start):
            cp = pltpu.make_async_copy(zero_scr, xs_hbm.at[pl.ds(start, EXPERT_BLOCK), :], zsem)
            cp.start()
            cp.wait()

        def pad_rows(e, carry):
            @pl.when(padded_ref[e] > 0)
            def _():
                zero_block(pl.multiple_of(pend_ref[e] - EXPERT_BLOCK, EXPERT_BLOCK))

            return carry

        lax.fori_loop(0, N_EXPERTS, pad_rows, 0)

        def tail(bk, carry):
            zero_block(pl.multiple_of(bk * EXPERT_BLOCK, EXPERT_BLOCK))
            return carry

        lax.fori_loop(nused_ref[0], nblk, tail, 0)

    slots = slot_ref[0]
    j = lax.broadcasted_iota(I32, (SORT_ROWS, tm), 0)
    perm = jnp.zeros((SORT_ROWS, tm), F32)
    for k in range(TOP_K):
        perm = perm + (j == slots[k:k + 1, :]).astype(F32)
    srt = jnp.dot(perm.astype(BF16), xn_ref[...], preferred_element_type=F32)
    sort_scr[buf] = _pack_bf16_pairs(srt)

    def copier(b):
        def make_copy(local, d, size):
            return pltpu.make_async_copy(sort_scr.at[b, pl.ds(local, size), :], xs_hbm.at[pl.ds(d, size), :],
                                         sems.at[b])
        return make_copy

    _run_copies(i, cnt_ref, dst_ref, copier(buf), lambda cp: cp.start())

    @pl.when(i > 0)
    def _():
        _run_copies(i - 1, cnt_ref, dst_ref, copier(1 - buf), lambda cp: cp.wait())

    @pl.when(i == last)
    def _():
        _run_copies(i, cnt_ref, dst_ref, copier(buf), lambda cp: cp.wait())


def dispatch(xn, slots_t, cnt_flat, dst_flat, pad_end, padded, nused, cap, tm):
    n, d = xn.shape
    nblk = cap // EXPERT_BLOCK
    body = functools.partial(_dispatch_body, tm=tm, nblk=nblk)
    return pl.pallas_call(
        body,
        grid_spec=pltpu.PrefetchScalarGridSpec(
            num_scalar_prefetch=5,
            grid=(n // tm,),
            in_specs=[
                pl.BlockSpec((1, 8, tm), lambda i, *_: (i, 0, 0)),
                pl.BlockSpec((tm, d), lambda i, *_: (i, 0)),
            ],
            out_specs=pl.BlockSpec(memory_space=pl.ANY),
            scratch_shapes=[
                pltpu.VMEM((2, SORT_ROWS, d // 2), U32),
                pltpu.VMEM((EXPERT_BLOCK, d // 2), U32),
                pltpu.SemaphoreType.DMA((2,)),
                pltpu.SemaphoreType.DMA(()),
            ],
        ),
        out_shape=jax.ShapeDtypeStruct((cap, d // 2), U32),
        compiler_params=_cparams(("arbitrary",)),
        name="dispatch",
    )(cnt_flat, dst_flat, pad_end, padded, nused, slots_t, xn)


def _experts_body(be_ref, nu_ref, x_ref, wg_ref, bg_ref, wu_ref, bu_ref, wd_ref, bd_ref, o_ref):
    i = pl.program_id(0)
    f = pl.program_id(1)
    used = i < nu_ref[0]

    def down_proj():
        xb = _unpack_bf16_pairs(x_ref[...])
        g = jnp.dot(xb, wg_ref[0], preferred_element_type=F32) + bg_ref[0]
        u = jnp.dot(xb, wu_ref[0], preferred_element_type=F32) + bu_ref[0]
        g = jnp.minimum(g, SWIGLU_LIMIT)
        u = jnp.clip(u, -SWIGLU_LIMIT, SWIGLU_LIMIT)
        a = g * jax.nn.sigmoid(SWIGLU_ALPHA * g) * (u + 1.0)
        return jnp.dot(a.astype(BF16), wd_ref[0], preferred_element_type=F32)

    @pl.when(jnp.logical_and(used, f == 0))
    def _():
        o_ref[...] = down_proj() + bd_ref[0]

    @pl.when(jnp.logical_and(used, f > 0))
    def _():
        o_ref[...] = o_ref[...] + down_proj()

    @pl.when(jnp.logical_not(used))
    def _():
        o_ref[...] = jnp.zeros(o_ref.shape, F32)


def experts(xs, blk_e, nused, wg, bg, wu, bu, wd, bd, tf=1024):
    cap = xs.shape[0]
    d, dff = wg.shape[1], wg.shape[2]
    nblk = cap // EXPERT_BLOCK
    nf = dff // tf

    def last_used(i, nu):
        return jnp.maximum(jnp.minimum(i, nu[0] - 1), 0)

    def eidx(i, be, nu):
        return be[last_used(i, nu)]

    def fidx(i, f, nu):
        return jnp.where(i < nu[0], f, nf - 1)

    return pl.pallas_call(
        _experts_body,
        grid_spec=pltpu.PrefetchScalarGridSpec(
            num_scalar_prefetch=2,
            grid=(nblk, nf),
            in_specs=[
                pl.BlockSpec((EXPERT_BLOCK, d // 2), lambda i, f, be, nu: (last_used(i, nu), 0)),
                pl.BlockSpec((1, d, tf), lambda i, f, be, nu: (eidx(i, be, nu), 0, fidx(i, f, nu))),
                pl.BlockSpec((1, 1, tf), lambda i, f, be, nu: (eidx(i, be, nu), 0, fidx(i, f, nu))),
                pl.BlockSpec((1, d, tf), lambda i, f, be, nu: (eidx(i, be, nu), 0, fidx(i, f, nu))),
                pl.BlockSpec((1, 1, tf), lambda i, f, be, nu: (eidx(i, be, nu), 0, fidx(i, f, nu))),
                pl.BlockSpec((1, tf, d), lambda i, f, be, nu: (eidx(i, be, nu), fidx(i, f, nu), 0)),
                pl.BlockSpec((1, 1, d), lambda i, f, be, nu: (eidx(i, be, nu), 0, 0)),
            ],
            out_specs=pl.BlockSpec((EXPERT_BLOCK, d), lambda i, f, be, nu: (i, 0)),
        ),
        out_shape=jax.ShapeDtypeStruct((cap, d), F32),
        compiler_params=_cparams(("arbitrary", "arbitrary")),
        name="experts",
    )(blk_e, nused, xs, wg, bg, wu, bu, wd, bd)


def _combine_body(cnt_ref, dst_ref, slot_ref, h_ref, tw_ref, pa_ref, pb_ref, y_hbm, wpg_ref, wpp_ref, gp_ref,
                  gfin_ref, oa_ref, ob_ref, gath, sems, *, tm, nba):
    i = pl.program_id(0)
    last = pl.num_programs(0) - 1
    buf = i % 2

    def copier(b):
        def make_copy(local, d, size):
            return pltpu.make_async_copy(y_hbm.at[pl.ds(d, size), :], gath.at[b, pl.ds(local, size), :],
                                         sems.at[b])
        return make_copy

    def fetch(step, b):
        gath[b, TOP_K * tm:, :] = jnp.zeros((SORT_ROWS - TOP_K * tm, gath.shape[2]), F32)
        _run_copies(step, cnt_ref, dst_ref, copier(b), lambda cp: cp.start())

    @pl.when(i == 0)
    def _():
        fetch(0, 0)

    @pl.when(i < last)
    def _():
        fetch(i + 1, 1 - buf)

    _run_copies(i, cnt_ref, dst_ref, copier(buf), lambda cp: cp.wait())

    tw = tw_ref[...]
    slots = slot_ref[...]
    lane = lax.broadcasted_iota(I32, (tm, SORT_ROWS), 1)
    wmat = jnp.zeros((tm, SORT_ROWS), F32)
    for k in range(TOP_K):
        wmat = wmat + jnp.where(lane == slots[:, k:k + 1], tw[:, k:k + 1], 0.0)
    moe = jnp.dot(wmat.astype(BF16), gath[buf].astype(BF16), preferred_element_type=F32)
    h2 = h_ref[...] + moe
    a = _rms(h2, gp_ref[...]).astype(BF16)
    gate = jax.nn.sigmoid(jnp.dot(a, wpg_ref[...], preferred_element_type=F32))
    pblk = jnp.where(i < nba, pa_ref[...], pb_ref[...])
    pp = jnp.dot(pblk.astype(BF16), wpp_ref[...], preferred_element_type=F32)
    h3 = h2 + gate * pp
    res = _rms(h3, gfin_ref[...])

    @pl.when(i < nba)
    def _():
        oa_ref[...] = res

    @pl.when(i >= nba)
    def _():
        ob_ref[...] = res


def combine_ple(h1, slots, tw, pa, pb, y, cnt_flat, dst_flat, wpg, wpp, g_ple, g_final, tm):
    n, d = h1.shape
    pd = pa.shape[1]
    nba = pa.shape[0] // tm
    body = functools.partial(_combine_body, tm=tm, nba=nba)
    full = lambda i, *_: (0, 0)
    row = lambda i, *_: (i, 0)
    return pl.pallas_call(
        body,
        grid_spec=pltpu.PrefetchScalarGridSpec(
            num_scalar_prefetch=2,
            grid=(n // tm,),
            in_specs=[
                pl.BlockSpec((tm, LANES), row),
                pl.BlockSpec((tm, d), row),
                pl.BlockSpec((tm, LANES), row),
            ] + _pair_specs(tm, pd, nba) + [
                pl.BlockSpec(memory_space=pl.ANY),
                pl.BlockSpec((d, d), full, pipeline_mode=pl.Buffered(1)),
                pl.BlockSpec((pd, d), full, pipeline_mode=pl.Buffered(1)),
                pl.BlockSpec((1, d), full),
                pl.BlockSpec((1, d), full),
            ],
            out_specs=_pair_specs(tm, d, nba),
            scratch_shapes=[pltpu.VMEM((2, SORT_ROWS, d), F32), pltpu.SemaphoreType.DMA((2,))],
        ),
        out_shape=[jax.ShapeDtypeStruct((pa.shape[0], d), F32), jax.ShapeDtypeStruct((pb.shape[0], d), F32)],
        compiler_params=_cparams(("arbitrary",)),
        name="combine_ple",
    )(cnt_flat, dst_flat, slots, h1, tw, pa, pb, y, wpg, wpp, g_ple, g_final)


def _rope_tables(l):
    rows = l // GRID_W
    row = jnp.repeat(jnp.arange(rows, dtype=F32), GRID_W)
    col = jnp.tile(jnp.arange(GRID_W, dtype=F32), rows)
    axis_dim = HEAD_DIM // 2
    inv = 1.0 / (ROPE_THETA ** (jnp.arange(0, axis_dim, 2, dtype=F32) / axis_dim))
    ang = jnp.concatenate([row[:, None] * inv, col[:, None] * inv], axis=-1)
    c, s = jnp.cos(ang), jnp.sin(ang)
    return jnp.concatenate([c, c], axis=-1), jnp.concatenate([-s, s], axis=-1)


def _filter_features(l):
    t = jnp.linspace(0.0, 1.0, l, dtype=F32)[:, None]
    w = 2.0 * math.pi * jnp.arange(l, dtype=F32)[:, None] / l
    f = jnp.linspace(1e-4, FILTER_BANDS - 1, FILTER_BANDS, dtype=F32)[None, :]
    z = jnp.concatenate([t, jnp.cos(f * w), -jnp.sin(f * w)], axis=-1)
    pos = jnp.concatenate([jnp.arange(l), jnp.zeros((1,), jnp.int32), jnp.arange(l - 1, 0, -1)])
    mask = jnp.ones((2 * l, 1), F32).at[l, 0].set(0.0)
    z2 = jnp.concatenate([z[pos], mask], axis=-1)
    return jnp.pad(z2, ((0, 0), (0, LANES - z2.shape[1])))


def kernel(x_prompt, x_sample, p_prompt, p_sample, g_mix, w_in, w_short, b_short, w_f1, b_f1, w_f2, b_f2,
           w_f3, b_f3, w_f4, filter_freq, hy_bias, g_q, g_k, w_hy_br, w_at_br, w_out, g_ffn, w_router,
           b_router, w_gate, b_gate, w_up, b_up, w_down, b_down, g_ple, w_ple_gate, w_ple_proj, g_final):
    assert w_in.shape[0] == 1, "single layer"
    l, d = x_prompt.shape[1], x_prompt.shape[2]
    c = w_hy_br.shape[1]
    aw = w_at_br.shape[1]
    n_kv = aw // (GROUP * HEAD_DIM)
    kvw = n_kv * HEAD_DIM
    assert 2 * l == FFT_R * FFT_R and x_sample.shape[1] == l

    ba, bb = x_prompt.shape[0], x_sample.shape[0]
    b = ba + bb
    n = b * l
    xa, xb = x_prompt.reshape(ba * l, d), x_sample.reshape(bb * l, d)
    pa, pb = p_prompt[0].reshape(ba * l, -1), p_sample[0].reshape(bb * l, -1)

    o_q, o_k, o_v, o_g = 3 * c, 3 * c + aw, 3 * c + aw + kvw, 3 * c + aw + 2 * kvw
    wi = w_in[0]
    w_perm = jnp.concatenate([wi[:, :o_k], wi[:, o_g:], wi[:, o_k:o_g]], axis=1).astype(BF16)
    w_perm = jnp.pad(w_perm, ((0, 0), (0, -w_perm.shape[1] % INPROJ_TN)))
    q_col, gate_col = o_q, o_k
    k_col = gate_col + 2 * d
    v_col = k_col + kvw

    proj = in_proj(xa, xb, g_mix, w_perm, tn=INPROJ_TN)
    proj3 = proj.reshape(b, l, proj.shape[1])

    consts = _dft_constants()
    z, x0c = hy_prep(proj3, w_short[0], b_short, c)
    w1p = jnp.pad(w_f1[0], ((0, LANES - FILTER_EMB), (0, 0)))
    deltas = jnp.linspace(math.log(DECAY_TARGET) / DECAY_SLOW, math.log(DECAY_TARGET) / DECAY_FAST, c, dtype=F32)
    kt = hy_filter(_filter_features(l), w1p, b_f1, w_f2[0], b_f2, w_f3[0], b_f3, filter_freq, w_f4[0],
                   jnp.abs(deltas)[None, :], c)
    kf = hy_spectrum(kt, consts)
    y_hy = hy_conv(z, x0c, kf, hy_bias, consts)

    cos2, sin2 = _rope_tables(l)
    y_at = attention(proj3, cos2, sin2, g_q, g_k, q_col, k_col, v_col, n_kv)

    oa, ob = _merge_moe_ple(y_hy.reshape(n, c), y_at.reshape(n, aw), proj, gate_col, xa, xb, pa, pb,
                            w_hy_br[0], w_at_br[0], w_out[0], g_ffn, w_router[0], b_router, w_gate[0], b_gate[0],
                            w_up[0], b_up[0], w_down[0], b_down[0], g_ple, w_ple_gate[0], w_ple_proj[0], g_final)
    return oa.reshape(ba, l, d), ob.reshape(bb, l, d)


def _merge_moe_ple(y_hy, y_at, proj, gate_col, xa, xb, pa, pb, w_hy_br, w_at_br, w_out, g_ffn, w_router, b_router,
                   w_gate, b_gate, w_up, b_up, w_down, b_down, g_ple, w_ple_gate, w_ple_proj, g_final):
    n = xa.shape[0] + xb.shape[0]
    wr_pad = jnp.pad(w_router, ((0, 0), (0, LANES - N_EXPERTS)))
    wr_hi = wr_pad.astype(BF16)
    wr_lo = (wr_pad - wr_hi.astype(F32)).astype(BF16)
    br_pad = jnp.pad(b_router, ((0, 0), (0, LANES - N_EXPERTS)), constant_values=-1e30)
    h1, xn, slots, slots_t, tw, cnt = merge_router(
        y_hy, y_at, proj, xa, xb, w_hy_br.astype(BF16), w_at_br.astype(BF16),
        w_out.astype(BF16), g_ffn, wr_hi, wr_lo, br_pad, gate_col, ROUTE_BLOCK)

    nrb = n // ROUTE_BLOCK
    cnt_be = cnt.reshape(nrb, 8, LANES)[:, 0, :N_EXPERTS].astype(I32)
    counts = jnp.sum(cnt_be, axis=0)
    padded = (counts + EXPERT_BLOCK - 1) // EXPERT_BLOCK * EXPERT_BLOCK
    pad_end = jnp.cumsum(padded).astype(I32)
    pad_start = pad_end - padded
    before = jnp.cumsum(cnt_be, axis=0) - cnt_be
    dst_be = pad_start[None, :] + before
    cap = nrb * SORT_ROWS + N_EXPERTS * EXPERT_BLOCK
    nblk = cap // EXPERT_BLOCK
    nused = (pad_end[-1:] // EXPERT_BLOCK).astype(I32)
    blk_start = jnp.arange(nblk, dtype=I32) * EXPERT_BLOCK
    blk_e = jnp.minimum(jnp.sum((pad_end[None, :] <= blk_start[:, None]).astype(I32), axis=1), N_EXPERTS - 1)
    cnt_flat = cnt_be.reshape(-1)
    dst_flat = dst_be.reshape(-1).astype(I32)

    xs = dispatch(xn, slots_t, cnt_flat, dst_flat, pad_end, padded, nused, cap, ROUTE_BLOCK)
    ys = experts(xs, blk_e, nused, w_gate.astype(BF16), b_gate[:, None, :], w_up.astype(BF16),
                 b_up[:, None, :], w_down.astype(BF16), b_down[:, None, :])
    return combine_ple(h1, slots, tw, pa, pb, ys, cnt_flat, dst_flat, w_ple_gate.astype(BF16),
                       w_ple_proj.astype(BF16), g_ple, g_final[None, :], ROUTE_BLOCK)
```

```python
import functools
import math

import jax
import jax.numpy as jnp
import numpy as np
from jax import lax
from jax.experimental import pallas as pl
from jax.experimental.pallas import tpu as pltpu

F32 = jnp.float32
BF16 = jnp.bfloat16
I32 = jnp.int32
U32 = jnp.uint32

EPS = 1e-6
HEAD_DIM = 128
GROUP = 4
GRID_W = 64
ROPE_THETA = 10000.0
HY_SHORT = 3
FILTER_EMB = 33
FILTER_BANDS = 16
DECAY_FAST = 0.3
DECAY_SLOW = 1.5
DECAY_TARGET = 1e-2
N_EXPERTS = 32
TOP_K = 4
SWIGLU_LIMIT = 7.0
SWIGLU_ALPHA = 1.702

LANES = 128
FFT_R = 128
EXPERT_BLOCK = 512
ROUTE_BLOCK = 256
INPROJ_TN = 1024
VMEM_LIMIT = 56 * 1024 * 1024


def _cparams(sem, vmem=VMEM_LIMIT):
    return pltpu.CompilerParams(dimension_semantics=sem, vmem_limit_bytes=vmem)


def _rms(x, g):
    return x * lax.rsqrt(jnp.mean(x * x, axis=-1, keepdims=True) + EPS) * g


def _pair_specs(tm, width, nba, **kw):
    return [
        pl.BlockSpec((tm, width), lambda i, *_: (jnp.minimum(i, nba - 1), 0), **kw),
        pl.BlockSpec((tm, width), lambda i, *_: (jnp.maximum(i - nba, 0), 0), **kw),
    ]


def _inproj_body(xa_ref, xb_ref, g_ref, w_ref, o_ref, a_scr, *, nba):
    @pl.when(pl.program_id(1) == 0)
    def _():
        x = jnp.where(pl.program_id(0) < nba, xa_ref[...], xb_ref[...])
        a_scr[...] = _rms(x, g_ref[...]).astype(BF16)

    o_ref[...] = jnp.dot(a_scr[...], w_ref[...], preferred_element_type=F32).astype(o_ref.dtype)


def in_proj(xa, xb, g, w_bf16, tm=1024, tn=1024):
    d = xa.shape[1]
    n = xa.shape[0] + xb.shape[0]
    nba = xa.shape[0] // tm
    nout = w_bf16.shape[1]
    return pl.pallas_call(
        functools.partial(_inproj_body, nba=nba),
        grid=(n // tm, nout // tn),
        in_specs=_pair_specs(tm, d, nba, pipeline_mode=pl.Buffered(1)) + [
            pl.BlockSpec((1, d), lambda i, j: (0, 0)),
            pl.BlockSpec((d, tn), lambda i, j: (0, j)),
        ],
        out_specs=pl.BlockSpec((tm, tn), lambda i, j: (i, j)),
        out_shape=jax.ShapeDtypeStruct((n, nout), BF16),
        scratch_shapes=[pltpu.VMEM((tm, d), BF16)],
        compiler_params=_cparams(("parallel", "arbitrary")),
        name="in_proj",
    )(xa, xb, g, w_bf16)


def _hyprep_body(u_ref, up_ref, un_ref, w_ref, b_ref, z_ref, x0_ref, *, tr, c):
    r = pl.program_id(1)
    nr = pl.num_programs(1)
    u = u_ref[0].astype(F32)
    hp = up_ref[0][15:16, :].astype(F32)
    hn = un_ref[0][0:1, :].astype(F32)
    hp = jnp.where(r == 0, 0.0, hp)
    hn = jnp.where(r == nr - 1, 0.0, hn)
    row = lax.broadcasted_iota(I32, u.shape, 0)
    prev = jnp.where(row == 0, hp, pltpu.roll(u, 1, axis=0))
    nxt = jnp.where(row == tr - 1, hn, pltpu.roll(u, tr - 1, axis=0))
    w = w_ref[...]
    uc = b_ref[...] + prev * w[0:1] + u * w[1:2] + nxt * w[2:3]
    x0 = uc[:, :c]
    x1 = uc[:, c:2 * c]
    v = uc[:, 2 * c:]
    z_ref[0] = (v * x1).astype(z_ref.dtype)
    x0_ref[0] = x0.astype(x0_ref.dtype)


def hy_prep(proj3, w_short, b_short, c, tr=256):
    b, l, _ = proj3.shape
    hb = tr // 16
    nh = l // 16
    body = functools.partial(_hyprep_body, tr=tr, c=c)
    return pl.pallas_call(
        body,
        grid=(b, l // tr),
        in_specs=[
            pl.BlockSpec((1, tr, 3 * c), lambda i, r: (i, r, 0)),
            pl.BlockSpec((1, 16, 3 * c), lambda i, r: (i, jnp.maximum(r * hb - 1, 0), 0)),
            pl.BlockSpec((1, 16, 3 * c), lambda i, r: (i, jnp.minimum((r + 1) * hb, nh - 1), 0)),
            pl.BlockSpec((HY_SHORT, 3 * c), lambda i, r: (0, 0)),
            pl.BlockSpec((1, 3 * c), lambda i, r: (0, 0)),
        ],
        out_specs=[
            pl.BlockSpec((1, tr, c), lambda i, r: (i, r, 0)),
            pl.BlockSpec((1, tr, c), lambda i, r: (i, r, 0)),
        ],
        out_shape=[jax.ShapeDtypeStruct((b, l, c), BF16), jax.ShapeDtypeStruct((b, l, c), BF16)],
        compiler_params=_cparams(("parallel", "parallel")),
        name="hy_prep",
    )(proj3, proj3, proj3, w_short, b_short)


def _hyfilter_body(z_ref, w1_ref, b1_ref, w2_ref, b2_ref, w3_ref, b3_ref, fr_ref, w4_ref, dl_ref, o_ref):
    hi = lax.Precision.HIGHEST
    zb = z_ref[...]
    fr = fr_ref[...]
    h = jnp.sin(fr * (jnp.dot(zb, w1_ref[...], precision=hi, preferred_element_type=F32) + b1_ref[...]))
    h = jnp.sin(fr * (jnp.dot(h, w2_ref[...], precision=hi, preferred_element_type=F32) + b2_ref[...]))
    h = jnp.sin(fr * (jnp.dot(h, w3_ref[...], precision=hi, preferred_element_type=F32) + b3_ref[...]))
    h4 = jnp.dot(h, w4_ref[...], precision=hi, preferred_element_type=F32)
    t = zb[:, 0:1]
    mask = zb[:, FILTER_EMB:FILTER_EMB + 1]
    o_ref[...] = h4 * jnp.exp(-t * dl_ref[...]) * mask


def hy_filter(zfeat, w1p, b1, w2, b2, w3, b3, freq, w4, absdelta, c, tr=1024):
    rows = zfeat.shape[0]
    half_blocks = rows // 2 // tr
    fo = w2.shape[0]
    full = lambda i: (0, 0)
    return pl.pallas_call(
        _hyfilter_body,
        grid=(rows // tr,),
        in_specs=[
            pl.BlockSpec((tr, LANES), lambda i: (i, 0)),
            pl.BlockSpec((LANES, fo), full),
            pl.BlockSpec((1, fo), full),
            pl.BlockSpec((fo, fo), full),
            pl.BlockSpec((1, fo), full),
            pl.BlockSpec((fo, fo), full),
            pl.BlockSpec((1, fo), full),
            pl.BlockSpec((1, fo), full),
            pl.BlockSpec((fo, c), lambda i: (0, i // half_blocks)),
            pl.BlockSpec((1, c), full),
        ],
        out_specs=pl.BlockSpec((tr, c), lambda i: (i, 0)),
        out_shape=jax.ShapeDtypeStruct((rows, c), F32),
        compiler_params=_cparams(("parallel",)),
        name="hy_filter",
    )(zfeat, w1p, b1, w2, b2, w3, b3, freq, w4, absdelta)


def _dft_constants():
    r = FFT_R
    n = r * r
    k = np.arange(r)
    ang = -2.0 * np.pi * np.outer(k, k) / r
    fr, fi = np.cos(ang), np.sin(ang)
    m1 = np.concatenate([fr, fi], axis=0)
    tw = -2.0 * np.pi * np.outer(k, k) / n
    m2 = np.concatenate([fr, fi], axis=1)
    gr, gi = fr, -fi
    g = np.block([[gr, gi], [-gi, gr]])
    h2 = np.concatenate([gr[: r // 2], gi[: r // 2]], axis=0) / n
    return dict(
        m1=jnp.asarray(m1, BF16), twr=jnp.asarray(np.cos(tw), F32), twi=jnp.asarray(np.sin(tw), F32),
        m2=jnp.asarray(m2, BF16), g=jnp.asarray(g, BF16), h2=jnp.asarray(h2, BF16))


CH_HALF = 64
CH_CHUNK = 8
CH_UNROLL = 8


def _dft_forward(zs, s1, m1_ref, twr, twi, m2_ref, half, n1_rows, emit):
    r = FFT_R

    def stage1(c, carry):
        zc = zs[pl.ds(pl.multiple_of((half * CH_HALF + c) * n1_rows, n1_rows), n1_rows), :].astype(BF16)
        a = jnp.dot(m1_ref[:, :n1_rows], zc, preferred_element_type=F32)
        ar, ai = a[:r], a[r:]
        s1[pl.ds(pl.multiple_of(c * 2 * r, 2 * r), r), :] = (ar * twr - ai * twi).astype(BF16)
        s1[pl.ds(pl.multiple_of(c * 2 * r + r, r), r), :] = (ar * twi + ai * twr).astype(BF16)
        return carry

    lax.fori_loop(0, CH_HALF, stage1, 0, unroll=CH_UNROLL)

    def stage2(j, carry):
        rows = CH_CHUNK * 2 * r
        lhs = s1[pl.ds(pl.multiple_of(j * rows, rows), rows), :]
        o = jnp.dot(lhs, m2_ref[...], preferred_element_type=F32).reshape(CH_CHUNK, 2 * r, 2 * r)
        xr = o[:, :r, :r] - o[:, r:, r:]
        xi = o[:, :r, r:] + o[:, r:, :r]
        emit(j, xr, xi)
        return carry

    lax.fori_loop(0, CH_HALF // CH_CHUNK, stage2, 0)


def _hyspec_body(k_ref, m1_ref, twr_ref, twi_ref, m2_ref, o_ref, zs, s1):
    r = FFT_R
    for n1 in range(r):
        blk = k_ref[n1 * r:(n1 + 1) * r, :]
        zs[pl.ds(n1, LANES, stride=r), :] = blk.T
    twr = twr_ref[...]
    twi = twi_ref[...]
    for half in range(LANES // CH_HALF):
        def emit(j, xr, xi, half=half):
            c0 = pl.multiple_of(half * CH_HALF + j * CH_CHUNK, CH_CHUNK)
            o_ref[pl.ds(c0, CH_CHUNK)] = jnp.concatenate([xr, xi], axis=-1).astype(o_ref.dtype)

        _dft_forward(zs, s1, m1_ref, twr, twi, m2_ref, half, r, emit)


def hy_spectrum(kt, consts):
    rows, c = kt.shape
    r = FFT_R
    full2 = lambda i: (0, 0)
    return pl.pallas_call(
        _hyspec_body,
        grid=(c // LANES,),
        in_specs=[
            pl.BlockSpec((rows, LANES), lambda i: (0, i)),
            pl.BlockSpec((2 * r, r), full2),
            pl.BlockSpec((r, r), full2),
            pl.BlockSpec((r, r), full2),
            pl.BlockSpec((r, 2 * r), full2),
        ],
        out_specs=pl.BlockSpec((LANES, r, 2 * r), lambda i: (i, 0, 0)),
        out_shape=jax.ShapeDtypeStruct((c, r, 2 * r), BF16),
        scratch_shapes=[pltpu.VMEM((LANES * r, r), F32), pltpu.VMEM((CH_HALF * 2 * r, r), BF16)],
        compiler_params=_cparams(("parallel",)),
        name="hy_spectrum",
    )(kt, consts["m1"], consts["twr"], consts["twi"], consts["m2"])


def _hyconv_body(z_ref, x0_ref, kf_ref, m1_ref, twr_ref, twi_ref, m2_ref, g_ref, h2_ref, bias_ref,
                 o_ref, zs, s1, s2):
    r = FFT_R
    n1_rows = r // 2
    for n1 in range(n1_rows):
        blk = z_ref[0, n1 * r:(n1 + 1) * r, :].astype(F32)
        zs[pl.ds(n1, LANES, stride=n1_rows), :] = blk.T
    twr = twr_ref[...]
    twi = twi_ref[...]
    for half in range(LANES // CH_HALF):
        def emit(j, xr, xi, half=half):
            c0 = pl.multiple_of(half * CH_HALF + j * CH_CHUNK, CH_CHUNK)
            kf = kf_ref[pl.ds(c0, CH_CHUNK)].astype(F32)
            kr, ki = kf[..., :r], kf[..., r:]
            y = jnp.concatenate([xr * kr - xi * ki, xr * ki + xi * kr], axis=-1).astype(BF16)
            rows = CH_CHUNK * r
            s2[pl.ds(pl.multiple_of(j * rows, rows), rows), :] = y.reshape(rows, 2 * r)

        _dft_forward(zs, s1, m1_ref, twr, twi, m2_ref, half, n1_rows, emit)

        def inv1(j, carry):
            rows = CH_CHUNK * r
            sl = pl.ds(pl.multiple_of(j * rows, rows), rows)
            bm = jnp.dot(s2[sl, :], g_ref[...], preferred_element_type=F32).reshape(CH_CHUNK, r, 2 * r)
            br, bi = bm[..., :r], bm[..., r:]
            b2 = jnp.concatenate([br * twr + bi * twi, bi * twr - br * twi], axis=-1).astype(BF16)
            s2[sl, :] = b2.reshape(rows, 2 * r)
            return carry

        lax.fori_loop(0, CH_HALF // CH_CHUNK, inv1, 0)

        def inv2(c, carry, half=half):
            q = jnp.dot(h2_ref[...], s2[pl.ds(pl.multiple_of(c * r, r), r), :], preferred_element_type=F32)
            yc = q[:n1_rows, :r] - q[n1_rows:, r:]
            zs[pl.ds(pl.multiple_of((half * CH_HALF + c) * n1_rows, n1_rows), n1_rows), :] = yc
            return carry

        lax.fori_loop(0, CH_HALF, inv2, 0, unroll=CH_UNROLL)

    bias = bias_ref[...]
    for n1 in range(n1_rows):
        y = zs[pl.ds(n1, LANES, stride=n1_rows), :].T
        sl = slice(n1 * r, (n1 + 1) * r)
        zz = z_ref[0, sl, :].astype(F32)
        x0 = x0_ref[0, sl, :].astype(F32)
        o_ref[0, sl, :] = ((y + zz * bias) * x0).astype(o_ref.dtype)


def hy_conv(z, x0c, kf, hy_bias, consts):
    b, l, c = z.shape
    r = FFT_R
    full2 = lambda i, j: (0, 0)
    return pl.pallas_call(
        _hyconv_body,
        grid=(c // LANES, b),
        in_specs=[
            pl.BlockSpec((1, l, LANES), lambda j, i: (i, 0, j)),
            pl.BlockSpec((1, l, LANES), lambda j, i: (i, 0, j)),
            pl.BlockSpec((LANES, r, 2 * r), lambda j, i: (j, 0, 0)),
            pl.BlockSpec((2 * r, r), full2),
            pl.BlockSpec((r, r), full2),
            pl.BlockSpec((r, r), full2),
            pl.BlockSpec((r, 2 * r), full2),
            pl.BlockSpec((2 * r, 2 * r), full2),
            pl.BlockSpec((r, r), full2),
            pl.BlockSpec((1, LANES), lambda j, i: (0, j)),
        ],
        out_specs=pl.BlockSpec((1, l, LANES), lambda j, i: (i, 0, j)),
        out_shape=jax.ShapeDtypeStruct((b, l, c), BF16),
        scratch_shapes=[
            pltpu.VMEM((LANES * r // 2, r), F32),
            pltpu.VMEM((CH_HALF * 2 * r, r), BF16),
            pltpu.VMEM((CH_HALF * r, 2 * r), BF16),
        ],
        compiler_params=_cparams(("parallel", "parallel")),
        name="hy_conv",
    )(z, x0c, kf, consts["m1"], consts["twr"], consts["twi"], consts["m2"], consts["g"], consts["h2"], hy_bias)


DEN_ROWS = 16


def _rope(x, c, s):
    return x * c + pltpu.roll(x, HEAD_DIM // 2, axis=1) * s


def _attn_body(q_ref, k_ref, v_ref, cos_ref, sin_ref, gq_ref, gk_ref, o_ref,
               kn_scr, vt_scr, qt_scr, sa_scr, sb_scr, m_scr, acc_scr, *, tq, tk, l):
    qi = pl.program_id(2)
    nk = l // tk

    @pl.when(qi == 0)
    def _():
        for r in range(nk):
            sl = slice(r * tk, (r + 1) * tk)
            kn = _rms(k_ref[0, sl, :].astype(F32), gk_ref[...])
            kn_scr[sl, :] = _rope(kn, cos_ref[sl, :], sin_ref[sl, :]).astype(BF16)
            vt_scr[r, :HEAD_DIM, :] = v_ref[0, sl, :].astype(F32).T.astype(BF16)
            vt_scr[r, HEAD_DIM:, :] = jnp.ones((DEN_ROWS, tk), BF16)

    row0 = pl.multiple_of(qi * tq, tq)
    c = cos_ref[pl.ds(row0, tq), :]
    s = sin_ref[pl.ds(row0, tq), :]
    scale = HEAD_DIM ** -0.5 * math.log2(math.e)
    for g in range(GROUP):
        qn = _rms(q_ref[0, :, g * HEAD_DIM:(g + 1) * HEAD_DIM].astype(F32), gq_ref[...])
        qt_scr[:, g * tq:(g + 1) * tq] = (_rope(qn, c, s) * scale).T.astype(BF16)

    m_scr[...] = jnp.full(m_scr.shape, -jnp.inf, F32)
    acc_scr[...] = jnp.zeros(acc_scr.shape, F32)
    nq = GROUP * tq

    def scores(j):
        kc = kn_scr[pl.ds(pl.multiple_of(j * tk, tk), tk), :]
        return jnp.dot(kc, qt_scr[...], preferred_element_type=F32)

    sa_scr[...] = scores(0)

    def half_step(j, cur, nxt):
        nxt[...] = scores(jnp.minimum(j + 1, nk - 1))
        st = cur[...]
        m_prev = m_scr[...]
        m_new = jnp.maximum(m_prev, jnp.max(st, axis=0, keepdims=True))
        alpha = jnp.exp2(m_prev - m_new)
        p = jnp.exp2(st - m_new)
        acc_scr[...] = alpha * acc_scr[...] + jnp.dot(vt_scr[j], p.astype(BF16), preferred_element_type=F32)
        m_scr[...] = m_new

    def kv_pair(jj, carry):
        half_step(2 * jj, sa_scr, sb_scr)
        half_step(2 * jj + 1, sb_scr, sa_scr)
        return carry

    lax.fori_loop(0, nk // 2, kv_pair, 0)
    o = acc_scr[:HEAD_DIM, :] / acc_scr[HEAD_DIM:HEAD_DIM + 1, :]
    for g in range(GROUP):
        o_ref[0, :, g * HEAD_DIM:(g + 1) * HEAD_DIM] = o[:, g * tq:(g + 1) * tq].T.astype(o_ref.dtype)


def attention(proj3, cos2, sin2, g_q, g_k, q_col, k_col, v_col, n_kv, tq=256, tk=1024):
    b, l, _ = proj3.shape
    gw = GROUP * HEAD_DIM
    body = functools.partial(_attn_body, tq=tq, tk=tk, l=l)
    qb, kb, vb = q_col // gw, k_col // HEAD_DIM, v_col // HEAD_DIM
    return pl.pallas_call(
        body,
        grid=(b, n_kv, l // tq),
        in_specs=[
            pl.BlockSpec((1, tq, gw), lambda i, h, q: (i, q, qb + h)),
            pl.BlockSpec((1, l, HEAD_DIM), lambda i, h, q: (i, 0, kb + h)),
            pl.BlockSpec((1, l, HEAD_DIM), lambda i, h, q: (i, 0, vb + h)),
            pl.BlockSpec((l, HEAD_DIM), lambda i, h, q: (0, 0)),
            pl.BlockSpec((l, HEAD_DIM), lambda i, h, q: (0, 0)),
            pl.BlockSpec((1, HEAD_DIM), lambda i, h, q: (0, 0)),
            pl.BlockSpec((1, HEAD_DIM), lambda i, h, q: (0, 0)),
        ],
        out_specs=pl.BlockSpec((1, tq, gw), lambda i, h, q: (i, q, h)),
        out_shape=jax.ShapeDtypeStruct((b, l, n_kv * gw), BF16),
        scratch_shapes=[
            pltpu.VMEM((l, HEAD_DIM), BF16),
            pltpu.VMEM((l // tk, HEAD_DIM + DEN_ROWS, tk), BF16),
            pltpu.VMEM((HEAD_DIM, GROUP * tq), BF16),
            pltpu.VMEM((tk, GROUP * tq), F32),
            pltpu.VMEM((tk, GROUP * tq), F32),
            pltpu.VMEM((1, GROUP * tq), F32),
            pltpu.VMEM((HEAD_DIM + DEN_ROWS, GROUP * tq), F32),
        ],
        compiler_params=_cparams(("parallel", "parallel", "arbitrary")),
        name="attention",
    )(proj3, proj3, proj3, cos2, sin2, g_q, g_k)


def _merge_body(yh_ref, ya_ref, gh_ref, ga_ref, ha_ref, hb_ref, whb_ref, wab_ref, wo_ref, gf_ref,
                wrh_ref, wrl_ref, br_ref, h1_ref, xn_ref, meta_ref, slott_ref, tw_ref, cnt_ref, *, tm, nba):
    a = jnp.dot(yh_ref[...], whb_ref[...], preferred_element_type=F32)
    bmat = jnp.dot(ya_ref[...], wab_ref[...], preferred_element_type=F32)
    mix = jax.nn.sigmoid(gh_ref[...].astype(F32)) * a + jax.nn.sigmoid(ga_ref[...].astype(F32)) * bmat
    h0 = jnp.where(pl.program_id(0) < nba, ha_ref[...], hb_ref[...])
    h1 = h0 + jnp.dot(mix.astype(BF16), wo_ref[...], preferred_element_type=F32)
    h1_ref[...] = h1
    xn = _rms(h1, gf_ref[...])
    xn_ref[...] = xn.astype(xn_ref.dtype)

    x_hi = xn.astype(BF16)
    x_lo = (xn - x_hi.astype(F32)).astype(BF16)
    logits = (jnp.dot(x_hi, wrh_ref[...], preferred_element_type=F32)
              + jnp.dot(x_lo, wrh_ref[...], preferred_element_type=F32)
              + jnp.dot(x_hi, wrl_ref[...], preferred_element_type=F32)) + br_ref[...]
    lt = logits.T
    row = lax.broadcasted_iota(I32, lt.shape, 0)
    work = lt
    vals, idxs = [], []
    sel = jnp.zeros(lt.shape, F32)
    for _ in range(TOP_K):
        m = jnp.max(work, axis=0, keepdims=True)
        idx = jnp.min(jnp.where(work == m, row, LANES), axis=0, keepdims=True)
        hit = row == idx
        vals.append(m)
        idxs.append(idx)
        sel = sel + hit.astype(F32)
        work = jnp.where(hit, -jnp.inf, work)
    ex = [jnp.exp(v - vals[0]) for v in vals]
    den = ex[0] + ex[1] + ex[2] + ex[3]

    rr = lax.broadcasted_iota(I32, (tm, tm), 0)
    cc = lax.broadcasted_iota(I32, (tm, tm), 1)
    before = (rr < cc).astype(BF16)
    prefix = jnp.dot(sel.astype(BF16), before, preferred_element_type=F32)
    units = jnp.floor((jnp.sum(sel, axis=1, keepdims=True) + (RUN_ALIGN - 1)) * (1.0 / RUN_ALIGN))
    units_b = jnp.broadcast_to(units, (LANES, LANES))
    cnt_ref[...] = (units_b.T * RUN_ALIGN)[0:8, :]
    er = lax.broadcasted_iota(I32, (LANES, LANES), 0)
    ec = lax.broadcasted_iota(I32, (LANES, LANES), 1)
    lower = (ec < er).astype(BF16)
    eoff = RUN_ALIGN * jnp.dot(lower, units_b.astype(BF16), preferred_element_type=F32)[:, 0:1]
    slot_of = prefix + eoff

    meta_t = jnp.full(lt.shape, -1.0, F32)
    tw_t = jnp.zeros(lt.shape, F32)
    for k in range(TOP_K):
        slot = jnp.sum(jnp.where(row == idxs[k], slot_of, 0.0), axis=0, keepdims=True)
        meta_t = jnp.where(row == k, slot, meta_t)
        tw_t = jnp.where(row == k, ex[k] / den, tw_t)
    slott_ref[0] = meta_t[0:8, :].astype(I32)
    meta_ref[...] = meta_t.T.astype(I32)
    tw_ref[...] = tw_t.T


def merge_router(y_hy, y_at, proj, xa, xb, whb, wab, wo, g_ffn, wr_hi, wr_lo, br_pad, gate_col, tm):
    d = xa.shape[1]
    n = xa.shape[0] + xb.shape[0]
    nba = xa.shape[0] // tm
    ch = y_hy.shape[1]
    ca = y_at.shape[1]
    gb = gate_col // d
    body = functools.partial(_merge_body, tm=tm, nba=nba)
    full = lambda i: (0, 0)
    row = lambda i: (i, 0)
    return pl.pallas_call(
        body,
        grid=(n // tm,),
        in_specs=[
            pl.BlockSpec((tm, ch), row),
            pl.BlockSpec((tm, ca), row),
            pl.BlockSpec((tm, d), lambda i: (i, gb)),
            pl.BlockSpec((tm, d), lambda i: (i, gb + 1)),
        ] + _pair_specs(tm, d, nba) + [
            pl.BlockSpec((ch, d), full),
            pl.BlockSpec((ca, d), full),
            pl.BlockSpec((d, d), full),
            pl.BlockSpec((1, d), full),
            pl.BlockSpec((d, LANES), full),
            pl.BlockSpec((d, LANES), full),
            pl.BlockSpec((1, LANES), full),
        ],
        out_specs=[
            pl.BlockSpec((tm, d), row),
            pl.BlockSpec((tm, d), row),
            pl.BlockSpec((tm, LANES), row),
            pl.BlockSpec((1, 8, tm), lambda i: (i, 0, 0)),
            pl.BlockSpec((tm, LANES), row),
            pl.BlockSpec((8, LANES), row),
        ],
        out_shape=[
            jax.ShapeDtypeStruct((n, d), F32),
            jax.ShapeDtypeStruct((n, d), BF16),
            jax.ShapeDtypeStruct((n, LANES), I32),
            jax.ShapeDtypeStruct((n // tm, 8, tm), I32),
            jax.ShapeDtypeStruct((n, LANES), F32),
            jax.ShapeDtypeStruct((n // tm * 8, LANES), F32),
        ],
        compiler_params=_cparams(("parallel",)),
        name="merge_router",
    )(y_hy, y_at, proj, proj, xa, xb, whb, wab, wo, g_ffn, wr_hi, wr_lo, br_pad)


RUN_ALIGN = 8
RUN_BITS = ROUTE_BLOCK.bit_length()
SORT_ROWS = TOP_K * ROUTE_BLOCK + N_EXPERTS * RUN_ALIGN


def _run_copies(i, cnt_ref, dst_ref, make_copy, op):
    def per_expert(e, local):
        c = cnt_ref[i * N_EXPERTS + e]
        d = dst_ref[i * N_EXPERTS + e]
        for bit in range(RUN_ALIGN.bit_length() - 1, RUN_BITS):
            size = 1 << bit

            @pl.when((c & size) != 0)
            def _(size=size):
                done = c & (size - 1)
                op(make_copy(pl.multiple_of(local + done, RUN_ALIGN), pl.multiple_of(d + done, RUN_ALIGN), size))

        return local + c

    lax.fori_loop(0, N_EXPERTS, per_expert, 0)


def _pack_bf16_pairs(x):
    half = x.shape[1] // 2
    hi = lax.bitcast_convert_type(x[:, :half], U32)
    lo = lax.bitcast_convert_type(x[:, half:], U32)
    return hi | (lo >> 16)


def _unpack_bf16_pairs(w):
    hi = lax.bitcast_convert_type(w & jnp.uint32(0xFFFF0000), F32).astype(BF16)
    lo = lax.bitcast_convert_type(w << 16, F32).astype(BF16)
    return jnp.concatenate([hi, lo], axis=1)


def _dispatch_body(cnt_ref, dst_ref, pend_ref, padded_ref, nused_ref, slot_ref, xn_ref, xs_hbm,
                   sort_scr, zero_scr, sems, zsem, *, tm, nblk):
    i = pl.program_id(0)
    last = pl.num_programs(0) - 1
    buf = i % 2

    @pl.when(i == 0)
    def _():
        zero_scr[...] = jnp.zeros(zero_scr.shape, U32)

        def zero_block(start):
            cp = pltpu.make_async_copy(zero_scr, xs_hbm.at[pl.ds(start, EXPERT_BLOCK), :], zsem)
            cp.start()
            cp.wait()

        def pad_rows(e, carry):
            @pl.when(padded_ref[e] > 0)
            def _():
                zero_block(pl.multiple_of(pend_ref[e] - EXPERT_BLOCK, EXPERT_BLOCK))

            return carry

        lax.fori_loop(0, N_EXPERTS, pad_rows, 0)

        def tail(bk, carry):
            zero_block(pl.multiple_of(bk * EXPERT_BLOCK, EXPERT_BLOCK))
            return carry

        lax.fori_loop(nused_ref[0], nblk, tail, 0)

    slots = slot_ref[0]
    j = lax.broadcasted_iota(I32, (SORT_ROWS, tm), 0)
    perm = jnp.zeros((SORT_ROWS, tm), F32)
    for k in range(TOP_K):
        perm = perm + (j == slots[k:k + 1, :]).astype(F32)
    srt = jnp.dot(perm.astype(BF16), xn_ref[...], preferred_element_type=F32)
    sort_scr[buf] = _pack_bf16_pairs(srt)

    def copier(b):
        def make_copy(local, d, size):
            return pltpu.make_async_copy(sort_scr.at[b, pl.ds(local, size), :], xs_hbm.at[pl.ds(d, size), :],
                                         sems.at[b])
        return make_copy

    _run_copies(i, cnt_ref, dst_ref, copier(buf), lambda cp: cp.start())

    @pl.when(i > 0)
    def _():
        _run_copies(i - 1, cnt_ref, dst_ref, copier(1 - buf), lambda cp: cp.wait())

    @pl.when(i == last)
    def _():
        _run_copies(i, cnt_ref, dst_ref, copier(buf), lambda cp: cp.wait())


def dispatch(xn, slots_t, cnt_flat, dst_flat, pad_end, padded, nused, cap, tm):
    n, d = xn.shape
    nblk = cap // EXPERT_BLOCK
    body = functools.partial(_dispatch_body, tm=tm, nblk=nblk)
    return pl.pallas_call(
        body,
        grid_spec=pltpu.PrefetchScalarGridSpec(
            num_scalar_prefetch=5,
            grid=(n // tm,),
            in_specs=[
                pl.BlockSpec((1, 8, tm), lambda i, *_: (i, 0, 0)),
                pl.BlockSpec((tm, d), lambda i, *_: (i, 0)),
            ],
            out_specs=pl.BlockSpec(memory_space=pl.ANY),
            scratch_shapes=[
                pltpu.VMEM((2, SORT_ROWS, d // 2), U32),
                pltpu.VMEM((EXPERT_BLOCK, d // 2), U32),
                pltpu.SemaphoreType.DMA((2,)),
                pltpu.SemaphoreType.DMA(()),
            ],
        ),
        out_shape=jax.ShapeDtypeStruct((cap, d // 2), U32),
        compiler_params=_cparams(("arbitrary",)),
        name="dispatch",
    )(cnt_flat, dst_flat, pad_end, padded, nused, slots_t, xn)


def _experts_body(be_ref, nu_ref, x_ref, wg_ref, bg_ref, wu_ref, bu_ref, wd_ref, bd_ref, o_ref):
    i = pl.program_id(0)
    f = pl.program_id(1)
    used = i < nu_ref[0]

    def down_proj():
        xb = _unpack_bf16_pairs(x_ref[...])
        g = jnp.dot(xb, wg_ref[0], preferred_element_type=F32) + bg_ref[0]
        u = jnp.dot(xb, wu_ref[0], preferred_element_type=F32) + bu_ref[0]
        g = jnp.minimum(g, SWIGLU_LIMIT)
        u = jnp.clip(u, -SWIGLU_LIMIT, SWIGLU_LIMIT)
        a = g * jax.nn.sigmoid(SWIGLU_ALPHA * g) * (u + 1.0)
        return jnp.dot(a.astype(BF16), wd_ref[0], preferred_element_type=F32)

    @pl.when(jnp.logical_and(used, f == 0))
    def _():
        o_ref[...] = down_proj() + bd_ref[0]

    @pl.when(jnp.logical_and(used, f > 0))
    def _():
        o_ref[...] = o_ref[...] + down_proj()

    @pl.when(jnp.logical_not(used))
    def _():
        o_ref[...] = jnp.zeros(o_ref.shape, F32)


def experts(xs, blk_e, nused, wg, bg, wu, bu, wd, bd, tf=1024):
    cap = xs.shape[0]
    d, dff = wg.shape[1], wg.shape[2]
    nblk = cap // EXPERT_BLOCK
    nf = dff // tf

    def last_used(i, nu):
        return jnp.maximum(jnp.minimum(i, nu[0] - 1), 0)

    def eidx(i, be, nu):
        return be[last_used(i, nu)]

    def fidx(i, f, nu):
        return jnp.where(i < nu[0], f, nf - 1)

    return pl.pallas_call(
        _experts_body,
        grid_spec=pltpu.PrefetchScalarGridSpec(
            num_scalar_prefetch=2,
            grid=(nblk, nf),
            in_specs=[
                pl.BlockSpec((EXPERT_BLOCK, d // 2), lambda i, f, be, nu: (last_used(i, nu), 0)),
                pl.BlockSpec((1, d, tf), lambda i, f, be, nu: (eidx(i, be, nu), 0, fidx(i, f, nu))),
                pl.BlockSpec((1, 1, tf), lambda i, f, be, nu: (eidx(i, be, nu), 0, fidx(i, f, nu))),
                pl.BlockSpec((1, d, tf), lambda i, f, be, nu: (eidx(i, be, nu), 0, fidx(i, f, nu))),
                pl.BlockSpec((1, 1, tf), lambda i, f, be, nu: (eidx(i, be, nu), 0, fidx(i, f, nu))),
                pl.BlockSpec((1, tf, d), lambda i, f, be, nu: (eidx(i, be, nu), fidx(i, f, nu), 0)),
                pl.BlockSpec((1, 1, d), lambda i, f, be, nu: (eidx(i, be, nu), 0, 0)),
            ],
            out_specs=pl.BlockSpec((EXPERT_BLOCK, d), lambda i, f, be, nu: (i, 0)),
        ),
        out_shape=jax.ShapeDtypeStruct((cap, d), F32),
        compiler_params=_cparams(("arbitrary", "arbitrary")),
        name="experts",
    )(blk_e, nused, xs, wg, bg, wu, bu, wd, bd)


def _combine_body(cnt_ref, dst_ref, slot_ref, h_ref, tw_ref, pa_ref, pb_ref, y_hbm, wpg_ref, wpp_ref, gp_ref,
                  gfin_ref, oa_ref, ob_ref, gath, sems, *, tm, nba):
    i = pl.program_id(0)
    last = pl.num_programs(0) - 1
    buf = i % 2

    def copier(b):
        def make_copy(local, d, size):
            return pltpu.make_async_copy(y_hbm.at[pl.ds(d, size), :], gath.at[b, pl.ds(local, size), :],
                                         sems.at[b])
        return make_copy

    def fetch(step, b):
        gath[b, TOP_K * tm:, :] = jnp.zeros((SORT_ROWS - TOP_K * tm, gath.shape[2]), F32)
        _run_copies(step, cnt_ref, dst_ref, copier(b), lambda cp: cp.start())

    @pl.when(i == 0)
    def _():
        fetch(0, 0)

    @pl.when(i < last)
    def _():
        fetch(i + 1, 1 - buf)

    _run_copies(i, cnt_ref, dst_ref, copier(buf), lambda cp: cp.wait())

    tw = tw_ref[...]
    slots = slot_ref[...]
    lane = lax.broadcasted_iota(I32, (tm, SORT_ROWS), 1)
    wmat = jnp.zeros((tm, SORT_ROWS), F32)
    for k in range(TOP_K):
        wmat = wmat + jnp.where(lane == slots[:, k:k + 1], tw[:, k:k + 1], 0.0)
    moe = jnp.dot(wmat.astype(BF16), gath[buf].astype(BF16), preferred_element_type=F32)
    h2 = h_ref[...] + moe
    a = _rms(h2, gp_ref[...]).astype(BF16)
    gate = jax.nn.sigmoid(jnp.dot(a, wpg_ref[...], preferred_element_type=F32))
    pblk = jnp.where(i < nba, pa_ref[...], pb_ref[...])
    pp = jnp.dot(pblk.astype(BF16), wpp_ref[...], preferred_element_type=F32)
    h3 = h2 + gate * pp
    res = _rms(h3, gfin_ref[...])

    @pl.when(i < nba)
    def _():
        oa_ref[...] = res

    @pl.when(i >= nba)
    def _():
        ob_ref[...] = res


def combine_ple(h1, slots, tw, pa, pb, y, cnt_flat, dst_flat, wpg, wpp, g_ple, g_final, tm):
    n, d = h1.shape
    pd = pa.shape[1]
    nba = pa.shape[0] // tm
    body = functools.partial(_combine_body, tm=tm, nba=nba)
    full = lambda i, *_: (0, 0)
    row = lambda i, *_: (i, 0)
    return pl.pallas_call(
        body,
        grid_spec=pltpu.PrefetchScalarGridSpec(
            num_scalar_prefetch=2,
            grid=(n // tm,),
            in_specs=[
                pl.BlockSpec((tm, LANES), row),
                pl.BlockSpec((tm, d), row),
                pl.BlockSpec((tm, LANES), row),
            ] + _pair_specs(tm, pd, nba) + [
                pl.BlockSpec(memory_space=pl.ANY),
                pl.BlockSpec((d, d), full, pipeline_mode=pl.Buffered(1)),
                pl.BlockSpec((pd, d), full, pipeline_mode=pl.Buffered(1)),
                pl.BlockSpec((1, d), full),
                pl.BlockSpec((1, d), full),
            ],
            out_specs=_pair_specs(tm, d, nba),
            scratch_shapes=[pltpu.VMEM((2, SORT_ROWS, d), F32), pltpu.SemaphoreType.DMA((2,))],
        ),
        out_shape=[jax.ShapeDtypeStruct((pa.shape[0], d), F32), jax.ShapeDtypeStruct((pb.shape[0], d), F32)],
        compiler_params=_cparams(("arbitrary",)),
        name="combine_ple",
    )(cnt_flat, dst_flat, slots, h1, tw, pa, pb, y, wpg, wpp, g_ple, g_final)


def _rope_tables(l):
    rows = l // GRID_W
    row = jnp.repeat(jnp.arange(rows, dtype=F32), GRID_W)
    col = jnp.tile(jnp.arange(GRID_W, dtype=F32), rows)
    axis_dim = HEAD_DIM // 2
    inv = 1.0 / (ROPE_THETA ** (jnp.arange(0, axis_dim, 2, dtype=F32) / axis_dim))
    ang = jnp.concatenate([row[:, None] * inv, col[:, None] * inv], axis=-1)
    c, s = jnp.cos(ang), jnp.sin(ang)
    return jnp.concatenate([c, c], axis=-1), jnp.concatenate([-s, s], axis=-1)


def _filter_features(l):
    t = jnp.linspace(0.0, 1.0, l, dtype=F32)[:, None]
    w = 2.0 * math.pi * jnp.arange(l, dtype=F32)[:, None] / l
    f = jnp.linspace(1e-4, FILTER_BANDS - 1, FILTER_BANDS, dtype=F32)[None, :]
    z = jnp.concatenate([t, jnp.cos(f * w), -jnp.sin(f * w)], axis=-1)
    pos = jnp.concatenate([jnp.arange(l), jnp.zeros((1,), jnp.int32), jnp.arange(l - 1, 0, -1)])
    mask = jnp.ones((2 * l, 1), F32).at[l, 0].set(0.0)
    z2 = jnp.concatenate([z[pos], mask], axis=-1)
    return jnp.pad(z2, ((0, 0), (0, LANES - z2.shape[1])))


def kernel(x_prompt, x_sample, p_prompt, p_sample, g_mix, w_in, w_short, b_short, w_f1, b_f1, w_f2, b_f2,
           w_f3, b_f3, w_f4, filter_freq, hy_bias, g_q, g_k, w_hy_br, w_at_br, w_out, g_ffn, w_router,
           b_router, w_gate, b_gate, w_up, b_up, w_down, b_down, g_ple, w_ple_gate, w_ple_proj, g_final):
    assert w_in.shape[0] == 1, "single layer"
    l, d = x_prompt.shape[1], x_prompt.shape[2]
    c = w_hy_br.shape[1]
    aw = w_at_br.shape[1]
    n_kv = aw // (GROUP * HEAD_DIM)
    kvw = n_kv * HEAD_DIM
    assert 2 * l == FFT_R * FFT_R and x_sample.shape[1] == l

    ba, bb = x_prompt.shape[0], x_sample.shape[0]
    b = ba + bb
    n = b * l
    xa, xb = x_prompt.reshape(ba * l, d), x_sample.reshape(bb * l, d)
    pa, pb = p_prompt[0].reshape(ba * l, -1), p_sample[0].reshape(bb * l, -1)

    o_q, o_k, o_v, o_g = 3 * c, 3 * c + aw, 3 * c + aw + kvw, 3 * c + aw + 2 * kvw
    wi = w_in[0]
    w_perm = jnp.concatenate([wi[:, :o_k], wi[:, o_g:], wi[:, o_k:o_g]], axis=1).astype(BF16)
    w_perm = jnp.pad(w_perm, ((0, 0), (0, -w_perm.shape[1] % INPROJ_TN)))
    q_col, gate_col = o_q, o_k
    k_col = gate_col + 2 * d
    v_col = k_col + kvw

    proj = in_proj(xa, xb, g_mix, w_perm, tn=INPROJ_TN)
    proj3 = proj.reshape(b, l, proj.shape[1])

    consts = _dft_constants()
    z, x0c = hy_prep(proj3, w_short[0], b_short, c)
    w1p = jnp.pad(w_f1[0], ((0, LANES - FILTER_EMB), (0, 0)))
    deltas = jnp.linspace(math.log(DECAY_TARGET) / DECAY_SLOW, math.log(DECAY_TARGET) / DECAY_FAST, c, dtype=F32)
    kt = hy_filter(_filter_features(l), w1p, b_f1, w_f2[0], b_f2, w_f3[0], b_f3, filter_freq, w_f4[0],
                   jnp.abs(deltas)[None, :], c)
    kf = hy_spectrum(kt, consts)
    y_hy = hy_conv(z, x0c, kf, hy_bias, consts)

    cos2, sin2 = _rope_tables(l)
    y_at = attention(proj3, cos2, sin2, g_q, g_k, q_col, k_col, v_col, n_kv)

    oa, ob = _merge_moe_ple(y_hy.reshape(n, c), y_at.reshape(n, aw), proj, gate_col, xa, xb, pa, pb,
                            w_hy_br[0], w_at_br[0], w_out[0], g_ffn, w_router[0], b_router, w_gate[0], b_gate[0],
                            w_up[0], b_up[0], w_down[0], b_down[0], g_ple, w_ple_gate[0], w_ple_proj[0], g_final)
    return oa.reshape(ba, l, d), ob.reshape(bb, l, d)


def _merge_moe_ple(y_hy, y_at, proj, gate_col, xa, xb, pa, pb, w_hy_br, w_at_br, w_out, g_ffn, w_router, b_router,
                   w_gate, b_gate, w_up, b_up, w_down, b_down, g_ple, w_ple_gate, w_ple_proj, g_final):
    n = xa.shape[0] + xb.shape[0]
    wr_pad = jnp.pad(w_router, ((0, 0), (0, LANES - N_EXPERTS)))
    wr_hi = wr_pad.astype(BF16)
    wr_lo = (wr_pad - wr_hi.astype(F32)).astype(BF16)
    br_pad = jnp.pad(b_router, ((0, 0), (0, LANES - N_EXPERTS)), constant_values=-1e30)
    h1, xn, slots, slots_t, tw, cnt = merge_router(
        y_hy, y_at, proj, xa, xb, w_hy_br.astype(BF16), w_at_br.astype(BF16),
        w_out.astype(BF16), g_ffn, wr_hi, wr_lo, br_pad, gate_col, ROUTE_BLOCK)

    nrb = n // ROUTE_BLOCK
    cnt_be = cnt.reshape(nrb, 8, LANES)[:, 0, :N_EXPERTS].astype(I32)
    counts = jnp.sum(cnt_be, axis=0)
    padded = (counts + EXPERT_BLOCK - 1) // EXPERT_BLOCK * EXPERT_BLOCK
    pad_end = jnp.cumsum(padded).astype(I32)
    pad_start = pad_end - padded
    before = jnp.cumsum(cnt_be, axis=0) - cnt_be
    dst_be = pad_start[None, :] + before
    cap = nrb * SORT_ROWS + N_EXPERTS * EXPERT_BLOCK
    nblk = cap // EXPERT_BLOCK
    nused = (pad_end[-1:] // EXPERT_BLOCK).astype(I32)
    blk_start = jnp.arange(nblk, dtype=I32) * EXPERT_BLOCK
    blk_e = jnp.minimum(jnp.sum((pad_end[None, :] <= blk_start[:, None]).astype(I32), axis=1), N_EXPERTS - 1)
    cnt_flat = cnt_be.reshape(-1)
    dst_flat = dst_be.reshape(-1).astype(I32)

    xs = dispatch(xn, slots_t, cnt_flat, dst_flat, pad_end, padded, nused, cap, ROUTE_BLOCK)
    ys = experts(xs, blk_e, nused, w_gate.astype(BF16), b_gate[:, None, :], w_up.astype(BF16),
                 b_up[:, None, :], w_down.astype(BF16), b_down[:, None, :])
    return combine_ple(h1, slots, tw, pa, pb, ys, cnt_flat, dst_flat, w_ple_gate.astype(BF16),
                       w_ple_proj.astype(BF16), g_ple, g_final[None, :], ROUTE_BLOCK)
```

```python
import functools
import math

import jax
import jax.numpy as jnp
import numpy as np
from jax import lax
from jax.experimental import pallas as pl
from jax.experimental.pallas import tpu as pltpu

F32 = jnp.float32
BF16 = jnp.bfloat16
I32 = jnp.int32
U32 = jnp.uint32

EPS = 1e-6
HEAD_DIM = 128
GROUP = 4
GRID_W = 64
ROPE_THETA = 10000.0
HY_SHORT = 3
FILTER_EMB = 33
FILTER_BANDS = 16
DECAY_FAST = 0.3
DECAY_SLOW = 1.5
DECAY_TARGET = 1e-2
N_EXPERTS = 32
TOP_K = 4
SWIGLU_LIMIT = 7.0
SWIGLU_ALPHA = 1.702

LANES = 128
FFT_R = 128
EXPERT_BLOCK = 512
ROUTE_BLOCK = 256
INPROJ_TN = 1024
VMEM_LIMIT = 56 * 1024 * 1024


def _cparams(sem, vmem=VMEM_LIMIT):
    return pltpu.CompilerParams(dimension_semantics=sem, vmem_limit_bytes=vmem)


def _rms(x, g):
    return x * lax.rsqrt(jnp.mean(x * x, axis=-1, keepdims=True) + EPS) * g


def _pair_specs(tm, width, nba, **kw):
    return [
        pl.BlockSpec((tm, width), lambda i, *_: (jnp.minimum(i, nba - 1), 0), **kw),
        pl.BlockSpec((tm, width), lambda i, *_: (jnp.maximum(i - nba, 0), 0), **kw),
    ]


def _inproj_body(xa_ref, xb_ref, g_ref, w_ref, o_ref, a_scr, *, nba):
    @pl.when(pl.program_id(1) == 0)
    def _():
        x = jnp.where(pl.program_id(0) < nba, xa_ref[...], xb_ref[...])
        a_scr[...] = _rms(x, g_ref[...]).astype(BF16)

    o_ref[...] = jnp.dot(a_scr[...], w_ref[...], preferred_element_type=F32).astype(o_ref.dtype)


def in_proj(xa, xb, g, w_bf16, tm=1024, tn=1024):
    d = xa.shape[1]
    n = xa.shape[0] + xb.shape[0]
    nba = xa.shape[0] // tm
    nout = w_bf16.shape[1]
    return pl.pallas_call(
        functools.partial(_inproj_body, nba=nba),
        grid=(n // tm, nout // tn),
        in_specs=_pair_specs(tm, d, nba, pipeline_mode=pl.Buffered(1)) + [
            pl.BlockSpec((1, d), lambda i, j: (0, 0)),
            pl.BlockSpec((d, tn), lambda i, j: (0, j)),
        ],
        out_specs=pl.BlockSpec((tm, tn), lambda i, j: (i, j)),
        out_shape=jax.ShapeDtypeStruct((n, nout), BF16),
        scratch_shapes=[pltpu.VMEM((tm, d), BF16)],
        compiler_params=_cparams(("parallel", "arbitrary")),
        name="in_proj",
    )(xa, xb, g, w_bf16)


def _hyprep_body(u_ref, up_ref, un_ref, w_ref, b_ref, z_ref, x0_ref, *, tr, c):
    r = pl.program_id(1)
    nr = pl.num_programs(1)
    u = u_ref[0].astype(F32)
    hp = up_ref[0][15:16, :].astype(F32)
    hn = un_ref[0][0:1, :].astype(F32)
    hp = jnp.where(r == 0, 0.0, hp)
    hn = jnp.where(r == nr - 1, 0.0, hn)
    row = lax.broadcasted_iota(I32, u.shape, 0)
    prev = jnp.where(row == 0, hp, pltpu.roll(u, 1, axis=0))
    nxt = jnp.where(row == tr - 1, hn, pltpu.roll(u, tr - 1, axis=0))
    w = w_ref[...]
    uc = b_ref[...] + prev * w[0:1] + u * w[1:2] + nxt * w[2:3]
    x0 = uc[:, :c]
    x1 = uc[:, c:2 * c]
    v = uc[:, 2 * c:]
    z_ref[0] = (v * x1).astype(z_ref.dtype)
    x0_ref[0] = x0.astype(x0_ref.dtype)


def hy_prep(proj3, w_short, b_short, c, tr=256):
    b, l, _ = proj3.shape
    hb = tr // 16
    nh = l // 16
    body = functools.partial(_hyprep_body, tr=tr, c=c)
    return pl.pallas_call(
        body,
        grid=(b, l // tr),
        in_specs=[
            pl.BlockSpec((1, tr, 3 * c), lambda i, r: (i, r, 0)),
            pl.BlockSpec((1, 16, 3 * c), lambda i, r: (i, jnp.maximum(r * hb - 1, 0), 0)),
            pl.BlockSpec((1, 16, 3 * c), lambda i, r: (i, jnp.minimum((r + 1) * hb, nh - 1), 0)),
            pl.BlockSpec((HY_SHORT, 3 * c), lambda i, r: (0, 0)),
            pl.BlockSpec((1, 3 * c), lambda i, r: (0, 0)),
        ],
        out_specs=[
            pl.BlockSpec((1, tr, c), lambda i, r: (i, r, 0)),
            pl.BlockSpec((1, tr, c), lambda i, r: (i, r, 0)),
        ],
        out_shape=[jax.ShapeDtypeStruct((b, l, c), BF16), jax.ShapeDtypeStruct((b, l, c), BF16)],
        compiler_params=_cparams(("parallel", "parallel")),
        name="hy_prep",
    )(proj3, proj3, proj3, w_short, b_short)


def _hyfilter_body(z_ref, w1_ref, b1_ref, w2_ref, b2_ref, w3_ref, b3_ref, fr_ref, w4_ref, dl_ref, o_ref):
    hi = lax.Precision.HIGHEST
    zb = z_ref[...]
    fr = fr_ref[...]
    h = jnp.sin(fr * (jnp.dot(zb, w1_ref[...], precision=hi, preferred_element_type=F32) + b1_ref[...]))
    h = jnp.sin(fr * (jnp.dot(h, w2_ref[...], precision=hi, preferred_element_type=F32) + b2_ref[...]))
    h = jnp.sin(fr * (jnp.dot(h, w3_ref[...], precision=hi, preferred_element_type=F32) + b3_ref[...]))
    h4 = jnp.dot(h, w4_ref[...], precision=hi, preferred_element_type=F32)
    t = zb[:, 0:1]
    mask = zb[:, FILTER_EMB:FILTER_EMB + 1]
    o_ref[...] = h4 * jnp.exp(-t * dl_ref[...]) * mask


def hy_filter(zfeat, w1p, b1, w2, b2, w3, b3, freq, w4, absdelta, c, tr=1024):
    rows = zfeat.shape[0]
    half_blocks = rows // 2 // tr
    fo = w2.shape[0]
    full = lambda i: (0, 0)
    return pl.pallas_call(
        _hyfilter_body,
        grid=(rows // tr,),
        in_specs=[
            pl.BlockSpec((tr, LANES), lambda i: (i, 0)),
            pl.BlockSpec((LANES, fo), full),
            pl.BlockSpec((1, fo), full),
            pl.BlockSpec((fo, fo), full),
            pl.BlockSpec((1, fo), full),
            pl.BlockSpec((fo, fo), full),
            pl.BlockSpec((1, fo), full),
            pl.BlockSpec((1, fo), full),
            pl.BlockSpec((fo, c), lambda i: (0, i // half_blocks)),
            pl.BlockSpec((1, c), full),
        ],
        out_specs=pl.BlockSpec((tr, c), lambda i: (i, 0)),
        out_shape=jax.ShapeDtypeStruct((rows, c), F32),
        compiler_params=_cparams(("parallel",)),
        name="hy_filter",
    )(zfeat, w1p, b1, w2, b2, w3, b3, freq, w4, absdelta)


def _dft_constants():
    r = FFT_R
    n = r * r
    k = np.arange(r)
    ang = -2.0 * np.pi * np.outer(k, k) / r
    fr, fi = np.cos(ang), np.sin(ang)
    m1 = np.concatenate([fr, fi], axis=0)
    tw = -2.0 * np.pi * np.outer(k, k) / n
    m2 = np.concatenate([fr, fi], axis=1)
    gr, gi = fr, -fi
    g = np.block([[gr, gi], [-gi, gr]])
    h2 = np.concatenate([gr[: r // 2], gi[: r // 2]], axis=0) / n
    return dict(
        m1=jnp.asarray(m1, BF16), twr=jnp.asarray(np.cos(tw), F32), twi=jnp.asarray(np.sin(tw), F32),
        m2=jnp.asarray(m2, BF16), g=jnp.asarray(g, BF16), h2=jnp.asarray(h2, BF16))


CH_HALF = 64
CH_CHUNK = 16
CH_UNROLL = 8


def _dft_forward(zs, s1, m1_ref, twr, twi, m2_ref, half, n1_rows, emit):
    r = FFT_R

    def stage1(c, carry):
        zc = zs[pl.ds(pl.multiple_of((half * CH_HALF + c) * n1_rows, n1_rows), n1_rows), :].astype(BF16)
        a = jnp.dot(m1_ref[:, :n1_rows], zc, preferred_element_type=F32)
        ar, ai = a[:r], a[r:]
        s1[pl.ds(pl.multiple_of(c * 2 * r, 2 * r), r), :] = (ar * twr - ai * twi).astype(BF16)
        s1[pl.ds(pl.multiple_of(c * 2 * r + r, r), r), :] = (ar * twi + ai * twr).astype(BF16)
        return carry

    lax.fori_loop(0, CH_HALF, stage1, 0, unroll=CH_UNROLL)

    def stage2(j, carry):
        rows = CH_CHUNK * 2 * r
        lhs = s1[pl.ds(pl.multiple_of(j * rows, rows), rows), :]
        o = jnp.dot(lhs, m2_ref[...], preferred_element_type=F32).reshape(CH_CHUNK, 2 * r, 2 * r)
        xr = o[:, :r, :r] - o[:, r:, r:]
        xi = o[:, :r, r:] + o[:, r:, :r]
        emit(j, xr, xi)
        return carry

    lax.fori_loop(0, CH_HALF // CH_CHUNK, stage2, 0)


def _hyspec_body(k_ref, m1_ref, twr_ref, twi_ref, m2_ref, o_ref, zs, s1):
    r = FFT_R
    for n1 in range(r):
        blk = k_ref[n1 * r:(n1 + 1) * r, :]
        zs[pl.ds(n1, LANES, stride=r), :] = blk.T
    twr = twr_ref[...]
    twi = twi_ref[...]
    for half in range(LANES // CH_HALF):
        def emit(j, xr, xi, half=half):
            c0 = pl.multiple_of(half * CH_HALF + j * CH_CHUNK, CH_CHUNK)
            o_ref[pl.ds(c0, CH_CHUNK)] = jnp.concatenate([xr, xi], axis=-1).astype(o_ref.dtype)

        _dft_forward(zs, s1, m1_ref, twr, twi, m2_ref, half, r, emit)


def hy_spectrum(kt, consts):
    rows, c = kt.shape
    r = FFT_R
    full2 = lambda i: (0, 0)
    return pl.pallas_call(
        _hyspec_body,
        grid=(c // LANES,),
        in_specs=[
            pl.BlockSpec((rows, LANES), lambda i: (0, i)),
            pl.BlockSpec((2 * r, r), full2),
            pl.BlockSpec((r, r), full2),
            pl.BlockSpec((r, r), full2),
            pl.BlockSpec((r, 2 * r), full2),
        ],
        out_specs=pl.BlockSpec((LANES, r, 2 * r), lambda i: (i, 0, 0)),
        out_shape=jax.ShapeDtypeStruct((c, r, 2 * r), BF16),
        scratch_shapes=[pltpu.VMEM((LANES * r, r), F32), pltpu.VMEM((CH_HALF * 2 * r, r), BF16)],
        compiler_params=_cparams(("parallel",)),
        name="hy_spectrum",
    )(kt, consts["m1"], consts["twr"], consts["twi"], consts["m2"])


def _hyconv_body(z_ref, x0_ref, kf_ref, m1_ref, twr_ref, twi_ref, m2_ref, g_ref, h2_ref, bias_ref,
                 o_ref, zs, s1, s2):
    r = FFT_R
    n1_rows = r // 2
    for n1 in range(n1_rows):
        blk = z_ref[0, n1 * r:(n1 + 1) * r, :].astype(F32)
        zs[pl.ds(n1, LANES, stride=n1_rows), :] = blk.T
    twr = twr_ref[...]
    twi = twi_ref[...]
    for half in range(LANES // CH_HALF):
        def emit(j, xr, xi, half=half):
            c0 = pl.multiple_of(half * CH_HALF + j * CH_CHUNK, CH_CHUNK)
            kf = kf_ref[pl.ds(c0, CH_CHUNK)].astype(F32)
            kr, ki = kf[..., :r], kf[..., r:]
            y = jnp.concatenate([xr * kr - xi * ki, xr * ki + xi * kr], axis=-1).astype(BF16)
            rows = CH_CHUNK * r
            s2[pl.ds(pl.multiple_of(j * rows, rows), rows), :] = y.reshape(rows, 2 * r)

        _dft_forward(zs, s1, m1_ref, twr, twi, m2_ref, half, n1_rows, emit)

        def inv1(j, carry):
            rows = CH_CHUNK * r
            sl = pl.ds(pl.multiple_of(j * rows, rows), rows)
            bm = jnp.dot(s2[sl, :], g_ref[...], preferred_element_type=F32).reshape(CH_CHUNK, r, 2 * r)
            br, bi = bm[..., :r], bm[..., r:]
            b2 = jnp.concatenate([br * twr + bi * twi, bi * twr - br * twi], axis=-1).astype(BF16)
            s2[sl, :] = b2.reshape(rows, 2 * r)
            return carry

        lax.fori_loop(0, CH_HALF // CH_CHUNK, inv1, 0)

        def inv2(c, carry, half=half):
            q = jnp.dot(h2_ref[...], s2[pl.ds(pl.multiple_of(c * r, r), r), :], preferred_element_type=F32)
            yc = q[:n1_rows, :r] - q[n1_rows:, r:]
            zs[pl.ds(pl.multiple_of((half * CH_HALF + c) * n1_rows, n1_rows), n1_rows), :] = yc
            return carry

        lax.fori_loop(0, CH_HALF, inv2, 0, unroll=CH_UNROLL)

    bias = bias_ref[...]
    for n1 in range(n1_rows):
        y = zs[pl.ds(n1, LANES, stride=n1_rows), :].T
        sl = slice(n1 * r, (n1 + 1) * r)
        zz = z_ref[0, sl, :].astype(F32)
        x0 = x0_ref[0, sl, :].astype(F32)
        o_ref[0, sl, :] = ((y + zz * bias) * x0).astype(o_ref.dtype)


def hy_conv(z, x0c, kf, hy_bias, consts):
    b, l, c = z.shape
    r = FFT_R
    full2 = lambda i, j: (0, 0)
    return pl.pallas_call(
        _hyconv_body,
        grid=(c // LANES, b),
        in_specs=[
            pl.BlockSpec((1, l, LANES), lambda j, i: (i, 0, j)),
            pl.BlockSpec((1, l, LANES), lambda j, i: (i, 0, j)),
            pl.BlockSpec((LANES, r, 2 * r), lambda j, i: (j, 0, 0)),
            pl.BlockSpec((2 * r, r), full2),
            pl.BlockSpec((r, r), full2),
            pl.BlockSpec((r, r), full2),
            pl.BlockSpec((r, 2 * r), full2),
            pl.BlockSpec((2 * r, 2 * r), full2),
            pl.BlockSpec((r, r), full2),
            pl.BlockSpec((1, LANES), lambda j, i: (0, j)),
        ],
        out_specs=pl.BlockSpec((1, l, LANES), lambda j, i: (i, 0, j)),
        out_shape=jax.ShapeDtypeStruct((b, l, c), BF16),
        scratch_shapes=[
            pltpu.VMEM((LANES * r // 2, r), F32),
            pltpu.VMEM((CH_HALF * 2 * r, r), BF16),
            pltpu.VMEM((CH_HALF * r, 2 * r), BF16),
        ],
        compiler_params=_cparams(("parallel", "parallel")),
        name="hy_conv",
    )(z, x0c, kf, consts["m1"], consts["twr"], consts["twi"], consts["m2"], consts["g"], consts["h2"], hy_bias)


DEN_ROWS = 16


def _rope(x, c, s):
    return x * c + pltpu.roll(x, HEAD_DIM // 2, axis=1) * s


def _attn_body(q_ref, k_ref, v_ref, cos_ref, sin_ref, gq_ref, gk_ref, o_ref,
               kn_scr, vt_scr, qt_scr, sa_scr, sb_scr, m_scr, acc_scr, *, tq, tk, l):
    qi = pl.program_id(2)
    nk = l // tk

    @pl.when(qi == 0)
    def _():
        for r in range(nk):
            sl = slice(r * tk, (r + 1) * tk)
            kn = _rms(k_ref[0, sl, :].astype(F32), gk_ref[...])
            kn_scr[sl, :] = _rope(kn, cos_ref[sl, :], sin_ref[sl, :]).astype(BF16)
            vt_scr[r, :HEAD_DIM, :] = v_ref[0, sl, :].astype(F32).T.astype(BF16)
            vt_scr[r, HEAD_DIM:, :] = jnp.ones((DEN_ROWS, tk), BF16)

    row0 = pl.multiple_of(qi * tq, tq)
    c = cos_ref[pl.ds(row0, tq), :]
    s = sin_ref[pl.ds(row0, tq), :]
    scale = HEAD_DIM ** -0.5 * math.log2(math.e)
    for g in range(GROUP):
        qn = _rms(q_ref[0, :, g * HEAD_DIM:(g + 1) * HEAD_DIM].astype(F32), gq_ref[...])
        qt_scr[:, g * tq:(g + 1) * tq] = (_rope(qn, c, s) * scale).T.astype(BF16)

    m_scr[...] = jnp.full(m_scr.shape, -jnp.inf, F32)
    acc_scr[...] = jnp.zeros(acc_scr.shape, F32)
    nq = GROUP * tq

    def scores(j):
        kc = kn_scr[pl.ds(pl.multiple_of(j * tk, tk), tk), :]
        return jnp.dot(kc, qt_scr[...], preferred_element_type=F32)

    sa_scr[...] = scores(0)

    def half_step(j, cur, nxt):
        nxt[...] = scores(jnp.minimum(j + 1, nk - 1))
        st = cur[...]
        m_prev = m_scr[...]
        m_new = jnp.maximum(m_prev, jnp.max(st, axis=0, keepdims=True))
        alpha = jnp.exp2(m_prev - m_new)
        p = jnp.exp2(st - m_new)
        acc_scr[...] = alpha * acc_scr[...] + jnp.dot(vt_scr[j], p.astype(BF16), preferred_element_type=F32)
        m_scr[...] = m_new

    def kv_pair(jj, carry):
        half_step(2 * jj, sa_scr, sb_scr)
        half_step(2 * jj + 1, sb_scr, sa_scr)
        return carry

    lax.fori_loop(0, nk // 2, kv_pair, 0)
    o = acc_scr[:HEAD_DIM, :] / acc_scr[HEAD_DIM:HEAD_DIM + 1, :]
    for g in range(GROUP):
        o_ref[0, :, g * HEAD_DIM:(g + 1) * HEAD_DIM] = o[:, g * tq:(g + 1) * tq].T.astype(o_ref.dtype)


def attention(proj3, cos2, sin2, g_q, g_k, q_col, k_col, v_col, n_kv, tq=256, tk=1024):
    b, l, _ = proj3.shape
    gw = GROUP * HEAD_DIM
    body = functools.partial(_attn_body, tq=tq, tk=tk, l=l)
    qb, kb, vb = q_col // gw, k_col // HEAD_DIM, v_col // HEAD_DIM
    return pl.pallas_call(
        body,
        grid=(b, n_kv, l // tq),
        in_specs=[
            pl.BlockSpec((1, tq, gw), lambda i, h, q: (i, q, qb + h)),
            pl.BlockSpec((1, l, HEAD_DIM), lambda i, h, q: (i, 0, kb + h)),
            pl.BlockSpec((1, l, HEAD_DIM), lambda i, h, q: (i, 0, vb + h)),
            pl.BlockSpec((l, HEAD_DIM), lambda i, h, q: (0, 0)),
            pl.BlockSpec((l, HEAD_DIM), lambda i, h, q: (0, 0)),
            pl.BlockSpec((1, HEAD_DIM), lambda i, h, q: (0, 0)),
            pl.BlockSpec((1, HEAD_DIM), lambda i, h, q: (0, 0)),
        ],
        out_specs=pl.BlockSpec((1, tq, gw), lambda i, h, q: (i, q, h)),
        out_shape=jax.ShapeDtypeStruct((b, l, n_kv * gw), BF16),
        scratch_shapes=[
            pltpu.VMEM((l, HEAD_DIM), BF16),
            pltpu.VMEM((l // tk, HEAD_DIM + DEN_ROWS, tk), BF16),
            pltpu.VMEM((HEAD_DIM, GROUP * tq), BF16),
            pltpu.VMEM((tk, GROUP * tq), F32),
            pltpu.VMEM((tk, GROUP * tq), F32),
            pltpu.VMEM((1, GROUP * tq), F32),
            pltpu.VMEM((HEAD_DIM + DEN_ROWS, GROUP * tq), F32),
        ],
        compiler_params=_cparams(("parallel", "parallel", "arbitrary")),
        name="attention",
    )(proj3, proj3, proj3, cos2, sin2, g_q, g_k)


def _merge_body(yh_ref, ya_ref, gh_ref, ga_ref, ha_ref, hb_ref, whb_ref, wab_ref, wo_ref, gf_ref,
                wrh_ref, wrl_ref, br_ref, h1_ref, xn_ref, meta_ref, slott_ref, tw_ref, cnt_ref, *, tm, nba):
    a = jnp.dot(yh_ref[...], whb_ref[...], preferred_element_type=F32)
    bmat = jnp.dot(ya_ref[...], wab_ref[...], preferred_element_type=F32)
    mix = jax.nn.sigmoid(gh_ref[...].astype(F32)) * a + jax.nn.sigmoid(ga_ref[...].astype(F32)) * bmat
    h0 = jnp.where(pl.program_id(0) < nba, ha_ref[...], hb_ref[...])
    h1 = h0 + jnp.dot(mix.astype(BF16), wo_ref[...], preferred_element_type=F32)
    h1_ref[...] = h1
    xn = _rms(h1, gf_ref[...])
    xn_ref[...] = xn.astype(xn_ref.dtype)

    x_hi = xn.astype(BF16)
    x_lo = (xn - x_hi.astype(F32)).astype(BF16)
    logits = (jnp.dot(x_hi, wrh_ref[...], preferred_element_type=F32)
              + jnp.dot(x_lo, wrh_ref[...], preferred_element_type=F32)
              + jnp.dot(x_hi, wrl_ref[...], preferred_element_type=F32)) + br_ref[...]
    lt = logits.T
    row = lax.broadcasted_iota(I32, lt.shape, 0)
    work = lt
    vals, idxs = [], []
    sel = jnp.zeros(lt.shape, F32)
    for _ in range(TOP_K):
        m = jnp.max(work, axis=0, keepdims=True)
        idx = jnp.min(jnp.where(work == m, row, LANES), axis=0, keepdims=True)
        hit = row == idx
        vals.append(m)
        idxs.append(idx)
        sel = sel + hit.astype(F32)
        work = jnp.where(hit, -jnp.inf, work)
    ex = [jnp.exp(v - vals[0]) for v in vals]
    den = ex[0] + ex[1] + ex[2] + ex[3]

    rr = lax.broadcasted_iota(I32, (tm, tm), 0)
    cc = lax.broadcasted_iota(I32, (tm, tm), 1)
    before = (rr < cc).astype(BF16)
    prefix = jnp.dot(sel.astype(BF16), before, preferred_element_type=F32)
    units = jnp.floor((jnp.sum(sel, axis=1, keepdims=True) + (RUN_ALIGN - 1)) * (1.0 / RUN_ALIGN))
    units_b = jnp.broadcast_to(units, (LANES, LANES))
    cnt_ref[...] = (units_b.T * RUN_ALIGN)[0:8, :]
    er = lax.broadcasted_iota(I32, (LANES, LANES), 0)
    ec = lax.broadcasted_iota(I32, (LANES, LANES), 1)
    lower = (ec < er).astype(BF16)
    eoff = RUN_ALIGN * jnp.dot(lower, units_b.astype(BF16), preferred_element_type=F32)[:, 0:1]
    slot_of = prefix + eoff

    meta_t = jnp.full(lt.shape, -1.0, F32)
    tw_t = jnp.zeros(lt.shape, F32)
    for k in range(TOP_K):
        slot = jnp.sum(jnp.where(row == idxs[k], slot_of, 0.0), axis=0, keepdims=True)
        meta_t = jnp.where(row == k, slot, meta_t)
        tw_t = jnp.where(row == k, ex[k] / den, tw_t)
    slott_ref[0] = meta_t[0:8, :].astype(I32)
    meta_ref[...] = meta_t.T.astype(I32)
    tw_ref[...] = tw_t.T


def merge_router(y_hy, y_at, proj, xa, xb, whb, wab, wo, g_ffn, wr_hi, wr_lo, br_pad, gate_col, tm):
    d = xa.shape[1]
    n = xa.shape[0] + xb.shape[0]
    nba = xa.shape[0] // tm
    ch = y_hy.shape[1]
    ca = y_at.shape[1]
    gb = gate_col // d
    body = functools.partial(_merge_body, tm=tm, nba=nba)
    full = lambda i: (0, 0)
    row = lambda i: (i, 0)
    return pl.pallas_call(
        body,
        grid=(n // tm,),
        in_specs=[
            pl.BlockSpec((tm, ch), row),
            pl.BlockSpec((tm, ca), row),
            pl.BlockSpec((tm, d), lambda i: (i, gb)),
            pl.BlockSpec((tm, d), lambda i: (i, gb + 1)),
        ] + _pair_specs(tm, d, nba) + [
            pl.BlockSpec((ch, d), full),
            pl.BlockSpec((ca, d), full),
            pl.BlockSpec((d, d), full),
            pl.BlockSpec((1, d), full),
            pl.BlockSpec((d, LANES), full),
            pl.BlockSpec((d, LANES), full),
            pl.BlockSpec((1, LANES), full),
        ],
        out_specs=[
            pl.BlockSpec((tm, d), row),
            pl.BlockSpec((tm, d), row),
            pl.BlockSpec((tm, LANES), row),
            pl.BlockSpec((1, 8, tm), lambda i: (i, 0, 0)),
            pl.BlockSpec((tm, LANES), row),
            pl.BlockSpec((8, LANES), row),
        ],
        out_shape=[
            jax.ShapeDtypeStruct((n, d), F32),
            jax.ShapeDtypeStruct((n, d), BF16),
            jax.ShapeDtypeStruct((n, LANES), I32),
            jax.ShapeDtypeStruct((n // tm, 8, tm), I32),
            jax.ShapeDtypeStruct((n, LANES), F32),
            jax.ShapeDtypeStruct((n // tm * 8, LANES), F32),
        ],
        compiler_params=_cparams(("parallel",)),
        name="merge_router",
    )(y_hy, y_at, proj, proj, xa, xb, whb, wab, wo, g_ffn, wr_hi, wr_lo, br_pad)


RUN_ALIGN = 8
RUN_BITS = ROUTE_BLOCK.bit_length()
SORT_ROWS = TOP_K * ROUTE_BLOCK + N_EXPERTS * RUN_ALIGN


def _run_copies(i, cnt_ref, dst_ref, make_copy, op):
    def per_expert(e, local):
        c = cnt_ref[i * N_EXPERTS + e]
        d = dst_ref[i * N_EXPERTS + e]
        for bit in range(RUN_ALIGN.bit_length() - 1, RUN_BITS):
            size = 1 << bit

            @pl.when((c & size) != 0)
            def _(size=size):
                done = c & (size - 1)
                op(make_copy(pl.multiple_of(local + done, RUN_ALIGN), pl.multiple_of(d + done, RUN_ALIGN), size))

        return local + c

    lax.fori_loop(0, N_EXPERTS, per_expert, 0)


def _pack_bf16_pairs(x):
    half = x.shape[1] // 2
    hi = lax.bitcast_convert_type(x[:, :half], U32)
    lo = lax.bitcast_convert_type(x[:, half:], U32)
    return hi | (lo >> 16)


def _unpack_bf16_pairs(w):
    hi = lax.bitcast_convert_type(w & jnp.uint32(0xFFFF0000), F32).astype(BF16)
    lo = lax.bitcast_convert_type(w << 16, F32).astype(BF16)
    return jnp.concatenate([hi, lo], axis=1)


def _dispatch_body(cnt_ref, dst_ref, pend_ref, padded_ref, nused_ref, slot_ref, xn_ref, xs_hbm,
                   sort_scr, zero_scr, sems, zsem, *, tm, nblk):
    i = pl.program_id(0)
    last = pl.num_programs(0) - 1
    buf = i % 2

    @pl.when(i == 0)
    def _():
        zero_scr[...] = jnp.zeros(zero_scr.shape, U32)

        def zero_block(start):
            cp = pltpu.make_async_copy(zero_scr, xs_hbm.at[pl.ds(start, EXPERT_BLOCK), :], zsem)
            cp.start()
            cp.wait()

        def pad_rows(e, carry):
            @pl.when(padded_ref[e] > 0)
            def _():
                zero_block(pl.multiple_of(pend_ref[e] - EXPERT_BLOCK, EXPERT_BLOCK))

            return carry

        lax.fori_loop(0, N_EXPERTS, pad_rows, 0)

        def tail(bk, carry):
            zero_block(pl.multiple_of(bk * EXPERT_BLOCK, EXPERT_BLOCK))
            return carry

        lax.fori_loop(nused_ref[0], nblk, tail, 0)

    slots = slot_ref[0]
    j = lax.broadcasted_iota(I32, (SORT_ROWS, tm), 0)
    perm = jnp.zeros((SORT_ROWS, tm), F32)
    for k in range(TOP_K):
        perm = perm + (j == slots[k:k + 1, :]).astype(F32)
    srt = jnp.dot(perm.astype(BF16), xn_ref[...], preferred_element_type=F32)
    sort_scr[buf] = _pack_bf16_pairs(srt)

    def copier(b):
        def make_copy(local, d, size):
            return pltpu.make_async_copy(sort_scr.at[b, pl.ds(local, size), :], xs_hbm.at[pl.ds(d, size), :],
                                         sems.at[b])
        return make_copy

    _run_copies(i, cnt_ref, dst_ref, copier(buf), lambda cp: cp.start())

    @pl.when(i > 0)
    def _():
        _run_copies(i - 1, cnt_ref, dst_ref, copier(1 - buf), lambda cp: cp.wait())

    @pl.when(i == last)
    def _():
        _run_copies(i, cnt_ref, dst_ref, copier(buf), lambda cp: cp.wait())


def dispatch(xn, slots_t, cnt_flat, dst_flat, pad_end, padded, nused, cap, tm):
    n, d = xn.shape
    nblk = cap // EXPERT_BLOCK
    body = functools.partial(_dispatch_body, tm=tm, nblk=nblk)
    return pl.pallas_call(
        body,
        grid_spec=pltpu.PrefetchScalarGridSpec(
            num_scalar_prefetch=5,
            grid=(n // tm,),
            in_specs=[
                pl.BlockSpec((1, 8, tm), lambda i, *_: (i, 0, 0)),
                pl.BlockSpec((tm, d), lambda i, *_: (i, 0)),
            ],
            out_specs=pl.BlockSpec(memory_space=pl.ANY),
            scratch_shapes=[
                pltpu.VMEM((2, SORT_ROWS, d // 2), U32),
                pltpu.VMEM((EXPERT_BLOCK, d // 2), U32),
                pltpu.SemaphoreType.DMA((2,)),
                pltpu.SemaphoreType.DMA(()),
            ],
        ),
        out_shape=jax.ShapeDtypeStruct((cap, d // 2), U32),
        compiler_params=_cparams(("arbitrary",)),
        name="dispatch",
    )(cnt_flat, dst_flat, pad_end, padded, nused, slots_t, xn)


def _experts_body(be_ref, nu_ref, x_ref, wg_ref, bg_ref, wu_ref, bu_ref, wd_ref, bd_ref, o_ref):
    i = pl.program_id(0)
    f = pl.program_id(1)
    used = i < nu_ref[0]

    def down_proj():
        xb = _unpack_bf16_pairs(x_ref[...])
        g = jnp.dot(xb, wg_ref[0], preferred_element_type=F32) + bg_ref[0]
        u = jnp.dot(xb, wu_ref[0], preferred_element_type=F32) + bu_ref[0]
        g = jnp.minimum(g, SWIGLU_LIMIT)
        u = jnp.clip(u, -SWIGLU_LIMIT, SWIGLU_LIMIT)
        a = g * jax.nn.sigmoid(SWIGLU_ALPHA * g) * (u + 1.0)
        return jnp.dot(a.astype(BF16), wd_ref[0], preferred_element_type=F32)

    @pl.when(jnp.logical_and(used, f == 0))
    def _():
        o_ref[...] = down_proj() + bd_ref[0]

    @pl.when(jnp.logical_and(used, f > 0))
    def _():
        o_ref[...] = o_ref[...] + down_proj()

    @pl.when(jnp.logical_not(used))
    def _():
        o_ref[...] = jnp.zeros(o_ref.shape, F32)


def experts(xs, blk_e, nused, wg, bg, wu, bu, wd, bd, tf=1024):
    cap = xs.shape[0]
    d, dff = wg.shape[1], wg.shape[2]
    nblk = cap // EXPERT_BLOCK
    nf = dff // tf

    def last_used(i, nu):
        return jnp.maximum(jnp.minimum(i, nu[0] - 1), 0)

    def eidx(i, be, nu):
        return be[last_used(i, nu)]

    def fidx(i, f, nu):
        return jnp.where(i < nu[0], f, nf - 1)

    return pl.pallas_call(
        _experts_body,
        grid_spec=pltpu.PrefetchScalarGridSpec(
            num_scalar_prefetch=2,
            grid=(nblk, nf),
            in_specs=[
                pl.BlockSpec((EXPERT_BLOCK, d // 2), lambda i, f, be, nu: (last_used(i, nu), 0)),
                pl.BlockSpec((1, d, tf), lambda i, f, be, nu: (eidx(i, be, nu), 0, fidx(i, f, nu))),
                pl.BlockSpec((1, 1, tf), lambda i, f, be, nu: (eidx(i, be, nu), 0, fidx(i, f, nu))),
                pl.BlockSpec((1, d, tf), lambda i, f, be, nu: (eidx(i, be, nu), 0, fidx(i, f, nu))),
                pl.BlockSpec((1, 1, tf), lambda i, f, be, nu: (eidx(i, be, nu), 0, fidx(i, f, nu))),
                pl.BlockSpec((1, tf, d), lambda i, f, be, nu: (eidx(i, be, nu), fidx(i, f, nu), 0)),
                pl.BlockSpec((1, 1, d), lambda i, f, be, nu: (eidx(i, be, nu), 0, 0)),
            ],
            out_specs=pl.BlockSpec((EXPERT_BLOCK, d), lambda i, f, be, nu: (i, 0)),
        ),
        out_shape=jax.ShapeDtypeStruct((cap, d), F32),
        compiler_params=_cparams(("arbitrary", "arbitrary")),
        name="experts",
    )(blk_e, nused, xs, wg, bg, wu, bu, wd, bd)


def _combine_body(cnt_ref, dst_ref, slot_ref, h_ref, tw_ref, pa_ref, pb_ref, y_hbm, wpg_ref, wpp_ref, gp_ref,
                  gfin_ref, oa_ref, ob_ref, gath, sems, *, tm, nba):
    i = pl.program_id(0)
    last = pl.num_programs(0) - 1
    buf = i % 2

    def copier(b):
        def make_copy(local, d, size):
            return pltpu.make_async_copy(y_hbm.at[pl.ds(d, size), :], gath.at[b, pl.ds(local, size), :],
                                         sems.at[b])
        return make_copy

    def fetch(step, b):
        gath[b, TOP_K * tm:, :] = jnp.zeros((SORT_ROWS - TOP_K * tm, gath.shape[2]), F32)
        _run_copies(step, cnt_ref, dst_ref, copier(b), lambda cp: cp.start())

    @pl.when(i == 0)
    def _():
        fetch(0, 0)

    @pl.when(i < last)
    def _():
        fetch(i + 1, 1 - buf)

    _run_copies(i, cnt_ref, dst_ref, copier(buf), lambda cp: cp.wait())

    tw = tw_ref[...]
    slots = slot_ref[...]
    lane = lax.broadcasted_iota(I32, (tm, SORT_ROWS), 1)
    wmat = jnp.zeros((tm, SORT_ROWS), F32)
    for k in range(TOP_K):
        wmat = wmat + jnp.where(lane == slots[:, k:k + 1], tw[:, k:k + 1], 0.0)
    moe = jnp.dot(wmat.astype(BF16), gath[buf].astype(BF16), preferred_element_type=F32)
    h2 = h_ref[...] + moe
    a = _rms(h2, gp_ref[...]).astype(BF16)
    gate = jax.nn.sigmoid(jnp.dot(a, wpg_ref[...], preferred_element_type=F32))
    pblk = jnp.where(i < nba, pa_ref[...], pb_ref[...])
    pp = jnp.dot(pblk.astype(BF16), wpp_ref[...], preferred_element_type=F32)
    h3 = h2 + gate * pp
    res = _rms(h3, gfin_ref[...])

    @pl.when(i < nba)
    def _():
        oa_ref[...] = res

    @pl.when(i >= nba)
    def _():
        ob_ref[...] = res


def combine_ple(h1, slots, tw, pa, pb, y, cnt_flat, dst_flat, wpg, wpp, g_ple, g_final, tm):
    n, d = h1.shape
    pd = pa.shape[1]
    nba = pa.shape[0] // tm
    body = functools.partial(_combine_body, tm=tm, nba=nba)
    full = lambda i, *_: (0, 0)
    row = lambda i, *_: (i, 0)
    return pl.pallas_call(
        body,
        grid_spec=pltpu.PrefetchScalarGridSpec(
            num_scalar_prefetch=2,
            grid=(n // tm,),
            in_specs=[
                pl.BlockSpec((tm, LANES), row),
                pl.BlockSpec((tm, d), row),
                pl.BlockSpec((tm, LANES), row),
            ] + _pair_specs(tm, pd, nba) + [
                pl.BlockSpec(memory_space=pl.ANY),
                pl.BlockSpec((d, d), full, pipeline_mode=pl.Buffered(1)),
                pl.BlockSpec((pd, d), full, pipeline_mode=pl.Buffered(1)),
                pl.BlockSpec((1, d), full),
                pl.BlockSpec((1, d), full),
            ],
            out_specs=_pair_specs(tm, d, nba),
            scratch_shapes=[pltpu.VMEM((2, SORT_ROWS, d), F32), pltpu.SemaphoreType.DMA((2,))],
        ),
        out_shape=[jax.ShapeDtypeStruct((pa.shape[0], d), F32), jax.ShapeDtypeStruct((pb.shape[0], d), F32)],
        compiler_params=_cparams(("arbitrary",)),
        name="combine_ple",
    )(cnt_flat, dst_flat, slots, h1, tw, pa, pb, y, wpg, wpp, g_ple, g_final)


def _rope_tables(l):
    rows = l // GRID_W
    row = jnp.repeat(jnp.arange(rows, dtype=F32), GRID_W)
    col = jnp.tile(jnp.arange(GRID_W, dtype=F32), rows)
    axis_dim = HEAD_DIM // 2
    inv = 1.0 / (ROPE_THETA ** (jnp.arange(0, axis_dim, 2, dtype=F32) / axis_dim))
    ang = jnp.concatenate([row[:, None] * inv, col[:, None] * inv], axis=-1)
    c, s = jnp.cos(ang), jnp.sin(ang)
    return jnp.concatenate([c, c], axis=-1), jnp.concatenate([-s, s], axis=-1)


def _filter_features(l):
    t = jnp.linspace(0.0, 1.0, l, dtype=F32)[:, None]
    w = 2.0 * math.pi * jnp.arange(l, dtype=F32)[:, None] / l
    f = jnp.linspace(1e-4, FILTER_BANDS - 1, FILTER_BANDS, dtype=F32)[None, :]
    z = jnp.concatenate([t, jnp.cos(f * w), -jnp.sin(f * w)], axis=-1)
    pos = jnp.concatenate([jnp.arange(l), jnp.zeros((1,), jnp.int32), jnp.arange(l - 1, 0, -1)])
    mask = jnp.ones((2 * l, 1), F32).at[l, 0].set(0.0)
    z2 = jnp.concatenate([z[pos], mask], axis=-1)
    return jnp.pad(z2, ((0, 0), (0, LANES - z2.shape[1])))


def kernel(x_prompt, x_sample, p_prompt, p_sample, g_mix, w_in, w_short, b_short, w_f1, b_f1, w_f2, b_f2,
           w_f3, b_f3, w_f4, filter_freq, hy_bias, g_q, g_k, w_hy_br, w_at_br, w_out, g_ffn, w_router,
           b_router, w_gate, b_gate, w_up, b_up, w_down, b_down, g_ple, w_ple_gate, w_ple_proj, g_final):
    assert w_in.shape[0] == 1, "single layer"
    l, d = x_prompt.shape[1], x_prompt.shape[2]
    c = w_hy_br.shape[1]
    aw = w_at_br.shape[1]
    n_kv = aw // (GROUP * HEAD_DIM)
    kvw = n_kv * HEAD_DIM
    assert 2 * l == FFT_R * FFT_R and x_sample.shape[1] == l

    ba, bb = x_prompt.shape[0], x_sample.shape[0]
    b = ba + bb
    n = b * l
    xa, xb = x_prompt.reshape(ba * l, d), x_sample.reshape(bb * l, d)
    pa, pb = p_prompt[0].reshape(ba * l, -1), p_sample[0].reshape(bb * l, -1)

    o_q, o_k, o_v, o_g = 3 * c, 3 * c + aw, 3 * c + aw + kvw, 3 * c + aw + 2 * kvw
    wi = w_in[0]
    w_perm = jnp.concatenate([wi[:, :o_k], wi[:, o_g:], wi[:, o_k:o_g]], axis=1).astype(BF16)
    w_perm = jnp.pad(w_perm, ((0, 0), (0, -w_perm.shape[1] % INPROJ_TN)))
    q_col, gate_col = o_q, o_k
    k_col = gate_col + 2 * d
    v_col = k_col + kvw

    proj = in_proj(xa, xb, g_mix, w_perm, tn=INPROJ_TN)
    proj3 = proj.reshape(b, l, proj.shape[1])

    consts = _dft_constants()
    z, x0c = hy_prep(proj3, w_short[0], b_short, c)
    w1p = jnp.pad(w_f1[0], ((0, LANES - FILTER_EMB), (0, 0)))
    deltas = jnp.linspace(math.log(DECAY_TARGET) / DECAY_SLOW, math.log(DECAY_TARGET) / DECAY_FAST, c, dtype=F32)
    kt = hy_filter(_filter_features(l), w1p, b_f1, w_f2[0], b_f2, w_f3[0], b_f3, filter_freq, w_f4[0],
                   jnp.abs(deltas)[None, :], c)
    kf = hy_spectrum(kt, consts)
    y_hy = hy_conv(z, x0c, kf, hy_bias, consts)

    cos2, sin2 = _rope_tables(l)
    y_at = attention(proj3, cos2, sin2, g_q, g_k, q_col, k_col, v_col, n_kv)

    oa, ob = _merge_moe_ple(y_hy.reshape(n, c), y_at.reshape(n, aw), proj, gate_col, xa, xb, pa, pb,
                            w_hy_br[0], w_at_br[0], w_out[0], g_ffn, w_router[0], b_router, w_gate[0], b_gate[0],
                            w_up[0], b_up[0], w_down[0], b_down[0], g_ple, w_ple_gate[0], w_ple_proj[0], g_final)
    return oa.reshape(ba, l, d), ob.reshape(bb, l, d)


def _merge_moe_ple(y_hy, y_at, proj, gate_col, xa, xb, pa, pb, w_hy_br, w_at_br, w_out, g_ffn, w_router, b_router,
                   w_gate, b_gate, w_up, b_up, w_down, b_down, g_ple, w_ple_gate, w_ple_proj, g_final):
    n = xa.shape[0] + xb.shape[0]
    wr_pad = jnp.pad(w_router, ((0, 0), (0, LANES - N_EXPERTS)))
    wr_hi = wr_pad.astype(BF16)
    wr_lo = (wr_pad - wr_hi.astype(F32)).astype(BF16)
    br_pad = jnp.pad(b_router, ((0, 0), (0, LANES - N_EXPERTS)), constant_values=-1e30)
    h1, xn, slots, slots_t, tw, cnt = merge_router(
        y_hy, y_at, proj, xa, xb, w_hy_br.astype(BF16), w_at_br.astype(BF16),
        w_out.astype(BF16), g_ffn, wr_hi, wr_lo, br_pad, gate_col, ROUTE_BLOCK)

    nrb = n // ROUTE_BLOCK
    cnt_be = cnt.reshape(nrb, 8, LANES)[:, 0, :N_EXPERTS].astype(I32)
    counts = jnp.sum(cnt_be, axis=0)
    padded = (counts + EXPERT_BLOCK - 1) // EXPERT_BLOCK * EXPERT_BLOCK
    pad_end = jnp.cumsum(padded).astype(I32)
    pad_start = pad_end - padded
    before = jnp.cumsum(cnt_be, axis=0) - cnt_be
    dst_be = pad_start[None, :] + before
    cap = nrb * SORT_ROWS + N_EXPERTS * EXPERT_BLOCK
    nblk = cap // EXPERT_BLOCK
    nused = (pad_end[-1:] // EXPERT_BLOCK).astype(I32)
    blk_start = jnp.arange(nblk, dtype=I32) * EXPERT_BLOCK
    blk_e = jnp.minimum(jnp.sum((pad_end[None, :] <= blk_start[:, None]).astype(I32), axis=1), N_EXPERTS - 1)
    cnt_flat = cnt_be.reshape(-1)
    dst_flat = dst_be.reshape(-1).astype(I32)

    xs = dispatch(xn, slots_t, cnt_flat, dst_flat, pad_end, padded, nused, cap, ROUTE_BLOCK)
    ys = experts(xs, blk_e, nused, w_gate.astype(BF16), b_gate[:, None, :], w_up.astype(BF16),
                 b_up[:, None, :], w_down.astype(BF16), b_down[:, None, :])
    return combine_ple(h1, slots, tw, pa, pb, ys, cnt_flat, dst_flat, w_ple_gate.astype(BF16),
                       w_ple_proj.astype(BF16), g_ple, g_final[None, :], ROUTE_BLOCK)
```

```python
import functools
import math

import jax
import jax.numpy as jnp
import numpy as np
from jax import lax
from jax.experimental import pallas as pl
from jax.experimental.pallas import tpu as pltpu

F32 = jnp.float32
BF16 = jnp.bfloat16
I32 = jnp.int32
U32 = jnp.uint32

EPS = 1e-6
HEAD_DIM = 128
GROUP = 4
GRID_W = 64
ROPE_THETA = 10000.0
HY_SHORT = 3
FILTER_EMB = 33
FILTER_BANDS = 16
DECAY_FAST = 0.3
DECAY_SLOW = 1.5
DECAY_TARGET = 1e-2
N_EXPERTS = 32
TOP_K = 4
SWIGLU_LIMIT = 7.0
SWIGLU_ALPHA = 1.702

LANES = 128
FFT_R = 128
EXPERT_BLOCK = 512
ROUTE_BLOCK = 256
INPROJ_TN = 1024
VMEM_LIMIT = 56 * 1024 * 1024


def _cparams(sem, vmem=VMEM_LIMIT):
    return pltpu.CompilerParams(dimension_semantics=sem, vmem_limit_bytes=vmem)


def _rms(x, g):
    return x * lax.rsqrt(jnp.mean(x * x, axis=-1, keepdims=True) + EPS) * g


def _pair_specs(tm, width, nba, **kw):
    return [
        pl.BlockSpec((tm, width), lambda i, *_: (jnp.minimum(i, nba - 1), 0), **kw),
        pl.BlockSpec((tm, width), lambda i, *_: (jnp.maximum(i - nba, 0), 0), **kw),
    ]


def _inproj_body(xa_ref, xb_ref, g_ref, w_ref, o_ref, a_scr, *, nba):
    @pl.when(pl.program_id(1) == 0)
    def _():
        x = jnp.where(pl.program_id(0) < nba, xa_ref[...], xb_ref[...])
        a_scr[...] = _rms(x, g_ref[...]).astype(BF16)

    o_ref[...] = jnp.dot(a_scr[...], w_ref[...], preferred_element_type=F32).astype(o_ref.dtype)


def in_proj(xa, xb, g, w_bf16, tm=1024, tn=1024):
    d = xa.shape[1]
    n = xa.shape[0] + xb.shape[0]
    nba = xa.shape[0] // tm
    nout = w_bf16.shape[1]
    return pl.pallas_call(
        functools.partial(_inproj_body, nba=nba),
        grid=(n // tm, nout // tn),
        in_specs=_pair_specs(tm, d, nba, pipeline_mode=pl.Buffered(1)) + [
            pl.BlockSpec((1, d), lambda i, j: (0, 0)),
            pl.BlockSpec((d, tn), lambda i, j: (0, j)),
        ],
        out_specs=pl.BlockSpec((tm, tn), lambda i, j: (i, j)),
        out_shape=jax.ShapeDtypeStruct((n, nout), BF16),
        scratch_shapes=[pltpu.VMEM((tm, d), BF16)],
        compiler_params=_cparams(("parallel", "arbitrary")),
        name="in_proj",
    )(xa, xb, g, w_bf16)


def _hyprep_body(u_ref, up_ref, un_ref, w_ref, b_ref, z_ref, x0_ref, *, tr, c):
    r = pl.program_id(1)
    nr = pl.num_programs(1)
    u = u_ref[0].astype(F32)
    hp = up_ref[0][15:16, :].astype(F32)
    hn = un_ref[0][0:1, :].astype(F32)
    hp = jnp.where(r == 0, 0.0, hp)
    hn = jnp.where(r == nr - 1, 0.0, hn)
    row = lax.broadcasted_iota(I32, u.shape, 0)
    prev = jnp.where(row == 0, hp, pltpu.roll(u, 1, axis=0))
    nxt = jnp.where(row == tr - 1, hn, pltpu.roll(u, tr - 1, axis=0))
    w = w_ref[...]
    uc = b_ref[...] + prev * w[0:1] + u * w[1:2] + nxt * w[2:3]
    x0 = uc[:, :c]
    x1 = uc[:, c:2 * c]
    v = uc[:, 2 * c:]
    z_ref[0] = (v * x1).astype(z_ref.dtype)
    x0_ref[0] = x0.astype(x0_ref.dtype)


def hy_prep(proj3, w_short, b_short, c, tr=256):
    b, l, _ = proj3.shape
    hb = tr // 16
    nh = l // 16
    body = functools.partial(_hyprep_body, tr=tr, c=c)
    return pl.pallas_call(
        body,
        grid=(b, l // tr),
        in_specs=[
            pl.BlockSpec((1, tr, 3 * c), lambda i, r: (i, r, 0)),
            pl.BlockSpec((1, 16, 3 * c), lambda i, r: (i, jnp.maximum(r * hb - 1, 0), 0)),
            pl.BlockSpec((1, 16, 3 * c), lambda i, r: (i, jnp.minimum((r + 1) * hb, nh - 1), 0)),
            pl.BlockSpec((HY_SHORT, 3 * c), lambda i, r: (0, 0)),
            pl.BlockSpec((1, 3 * c), lambda i, r: (0, 0)),
        ],
        out_specs=[
            pl.BlockSpec((1, tr, c), lambda i, r: (i, r, 0)),
            pl.BlockSpec((1, tr, c), lambda i, r: (i, r, 0)),
        ],
        out_shape=[jax.ShapeDtypeStruct((b, l, c), BF16), jax.ShapeDtypeStruct((b, l, c), BF16)],
        compiler_params=_cparams(("parallel", "parallel")),
        name="hy_prep",
    )(proj3, proj3, proj3, w_short, b_short)


def _hyfilter_body(z_ref, w1_ref, b1_ref, w2_ref, b2_ref, w3_ref, b3_ref, fr_ref, w4_ref, dl_ref, o_ref):
    hi = lax.Precision.HIGHEST
    zb = z_ref[...]
    fr = fr_ref[...]
    h = jnp.sin(fr * (jnp.dot(zb, w1_ref[...], precision=hi, preferred_element_type=F32) + b1_ref[...]))
    h = jnp.sin(fr * (jnp.dot(h, w2_ref[...], precision=hi, preferred_element_type=F32) + b2_ref[...]))
    h = jnp.sin(fr * (jnp.dot(h, w3_ref[...], precision=hi, preferred_element_type=F32) + b3_ref[...]))
    h4 = jnp.dot(h, w4_ref[...], precision=hi, preferred_element_type=F32)
    t = zb[:, 0:1]
    mask = zb[:, FILTER_EMB:FILTER_EMB + 1]
    o_ref[...] = h4 * jnp.exp(-t * dl_ref[...]) * mask


def hy_filter(zfeat, w1p, b1, w2, b2, w3, b3, freq, w4, absdelta, c, tr=1024):
    rows = zfeat.shape[0]
    half_blocks = rows // 2 // tr
    fo = w2.shape[0]
    full = lambda i: (0, 0)
    return pl.pallas_call(
        _hyfilter_body,
        grid=(rows // tr,),
        in_specs=[
            pl.BlockSpec((tr, LANES), lambda i: (i, 0)),
            pl.BlockSpec((LANES, fo), full),
            pl.BlockSpec((1, fo), full),
            pl.BlockSpec((fo, fo), full),
            pl.BlockSpec((1, fo), full),
            pl.BlockSpec((fo, fo), full),
            pl.BlockSpec((1, fo), full),
            pl.BlockSpec((1, fo), full),
            pl.BlockSpec((fo, c), lambda i: (0, i // half_blocks)),
            pl.BlockSpec((1, c), full),
        ],
        out_specs=pl.BlockSpec((tr, c), lambda i: (i, 0)),
        out_shape=jax.ShapeDtypeStruct((rows, c), F32),
        compiler_params=_cparams(("parallel",)),
        name="hy_filter",
    )(zfeat, w1p, b1, w2, b2, w3, b3, freq, w4, absdelta)


def _dft_constants():
    r = FFT_R
    n = r * r
    k = np.arange(r)
    ang = -2.0 * np.pi * np.outer(k, k) / r
    fr, fi = np.cos(ang), np.sin(ang)
    m1 = np.concatenate([fr, fi], axis=0)
    tw = -2.0 * np.pi * np.outer(k, k) / n
    m2 = np.concatenate([fr, fi], axis=1)
    gr, gi = fr, -fi
    g = np.block([[gr, gi], [-gi, gr]])
    h2 = np.concatenate([gr[: r // 2], gi[: r // 2]], axis=0) / n
    return dict(
        m1=jnp.asarray(m1, BF16), twr=jnp.asarray(np.cos(tw), F32), twi=jnp.asarray(np.sin(tw), F32),
        m2=jnp.asarray(m2, BF16), g=jnp.asarray(g, BF16), h2=jnp.asarray(h2, BF16))


CH_HALF = 64
CH_CHUNK = 32
CH_UNROLL = 16


def _dft_forward(zs, s1, m1_ref, twr, twi, m2_ref, half, n1_rows, emit):
    r = FFT_R

    def stage1(c, carry):
        zc = zs[pl.ds(pl.multiple_of((half * CH_HALF + c) * n1_rows, n1_rows), n1_rows), :].astype(BF16)
        a = jnp.dot(m1_ref[:, :n1_rows], zc, preferred_element_type=F32)
        ar, ai = a[:r], a[r:]
        s1[pl.ds(pl.multiple_of(c * 2 * r, 2 * r), r), :] = (ar * twr - ai * twi).astype(BF16)
        s1[pl.ds(pl.multiple_of(c * 2 * r + r, r), r), :] = (ar * twi + ai * twr).astype(BF16)
        return carry

    lax.fori_loop(0, CH_HALF, stage1, 0, unroll=CH_UNROLL)

    def stage2(j, carry):
        rows = CH_CHUNK * 2 * r
        lhs = s1[pl.ds(pl.multiple_of(j * rows, rows), rows), :]
        o = jnp.dot(lhs, m2_ref[...], preferred_element_type=F32).reshape(CH_CHUNK, 2 * r, 2 * r)
        xr = o[:, :r, :r] - o[:, r:, r:]
        xi = o[:, :r, r:] + o[:, r:, :r]
        emit(j, xr, xi)
        return carry

    lax.fori_loop(0, CH_HALF // CH_CHUNK, stage2, 0)


def _hyspec_body(k_ref, m1_ref, twr_ref, twi_ref, m2_ref, o_ref, zs, s1):
    r = FFT_R
    for n1 in range(r):
        blk = k_ref[n1 * r:(n1 + 1) * r, :]
        zs[pl.ds(n1, LANES, stride=r), :] = blk.T
    twr = twr_ref[...]
    twi = twi_ref[...]
    for half in range(LANES // CH_HALF):
        def emit(j, xr, xi, half=half):
            c0 = pl.multiple_of(half * CH_HALF + j * CH_CHUNK, CH_CHUNK)
            o_ref[pl.ds(c0, CH_CHUNK)] = jnp.concatenate([xr, xi], axis=-1).astype(o_ref.dtype)

        _dft_forward(zs, s1, m1_ref, twr, twi, m2_ref, half, r, emit)


def hy_spectrum(kt, consts):
    rows, c = kt.shape
    r = FFT_R
    full2 = lambda i: (0, 0)
    return pl.pallas_call(
        _hyspec_body,
        grid=(c // LANES,),
        in_specs=[
            pl.BlockSpec((rows, LANES), lambda i: (0, i)),
            pl.BlockSpec((2 * r, r), full2),
            pl.BlockSpec((r, r), full2),
            pl.BlockSpec((r, r), full2),
            pl.BlockSpec((r, 2 * r), full2),
        ],
        out_specs=pl.BlockSpec((LANES, r, 2 * r), lambda i: (i, 0, 0)),
        out_shape=jax.ShapeDtypeStruct((c, r, 2 * r), BF16),
        scratch_shapes=[pltpu.VMEM((LANES * r, r), F32), pltpu.VMEM((CH_HALF * 2 * r, r), BF16)],
        compiler_params=_cparams(("parallel",)),
        name="hy_spectrum",
    )(kt, consts["m1"], consts["twr"], consts["twi"], consts["m2"])


def _hyconv_body(z_ref, x0_ref, kf_ref, m1_ref, twr_ref, twi_ref, m2_ref, g_ref, h2_ref, bias_ref,
                 o_ref, zs, s1, s2):
    r = FFT_R
    n1_rows = r // 2
    for n1 in range(n1_rows):
        blk = z_ref[0, n1 * r:(n1 + 1) * r, :].astype(F32)
        zs[pl.ds(n1, LANES, stride=n1_rows), :] = blk.T
    twr = twr_ref[...]
    twi = twi_ref[...]
    for half in range(LANES // CH_HALF):
        def emit(j, xr, xi, half=half):
            c0 = pl.multiple_of(half * CH_HALF + j * CH_CHUNK, CH_CHUNK)
            kf = kf_ref[pl.ds(c0, CH_CHUNK)].astype(F32)
            kr, ki = kf[..., :r], kf[..., r:]
            y = jnp.concatenate([xr * kr - xi * ki, xr * ki + xi * kr], axis=-1).astype(BF16)
            rows = CH_CHUNK * r
            s2[pl.ds(pl.multiple_of(j * rows, rows), rows), :] = y.reshape(rows, 2 * r)

        _dft_forward(zs, s1, m1_ref, twr, twi, m2_ref, half, n1_rows, emit)

        def inv1(j, carry):
            rows = CH_CHUNK * r
            sl = pl.ds(pl.multiple_of(j * rows, rows), rows)
            bm = jnp.dot(s2[sl, :], g_ref[...], preferred_element_type=F32).reshape(CH_CHUNK, r, 2 * r)
            br, bi = bm[..., :r], bm[..., r:]
            b2 = jnp.concatenate([br * twr + bi * twi, bi * twr - br * twi], axis=-1).astype(BF16)
            s2[sl, :] = b2.reshape(rows, 2 * r)
            return carry

        lax.fori_loop(0, CH_HALF // CH_CHUNK, inv1, 0)

        def inv2(c, carry, half=half):
            q = jnp.dot(h2_ref[...], s2[pl.ds(pl.multiple_of(c * r, r), r), :], preferred_element_type=F32)
            yc = q[:n1_rows, :r] - q[n1_rows:, r:]
            zs[pl.ds(pl.multiple_of((half * CH_HALF + c) * n1_rows, n1_rows), n1_rows), :] = yc
            return carry

        lax.fori_loop(0, CH_HALF, inv2, 0, unroll=CH_UNROLL)

    bias = bias_ref[...]
    for n1 in range(n1_rows):
        y = zs[pl.ds(n1, LANES, stride=n1_rows), :].T
        sl = slice(n1 * r, (n1 + 1) * r)
        zz = z_ref[0, sl, :].astype(F32)
        x0 = x0_ref[0, sl, :].astype(F32)
        o_ref[0, sl, :] = ((y + zz * bias) * x0).astype(o_ref.dtype)


def hy_conv(z, x0c, kf, hy_bias, consts):
    b, l, c = z.shape
    r = FFT_R
    full2 = lambda i, j: (0, 0)
    return pl.pallas_call(
        _hyconv_body,
        grid=(c // LANES, b),
        in_specs=[
            pl.BlockSpec((1, l, LANES), lambda j, i: (i, 0, j)),
            pl.BlockSpec((1, l, LANES), lambda j, i: (i, 0, j)),
            pl.BlockSpec((LANES, r, 2 * r), lambda j, i: (j, 0, 0)),
            pl.BlockSpec((2 * r, r), full2),
            pl.BlockSpec((r, r), full2),
            pl.BlockSpec((r, r), full2),
            pl.BlockSpec((r, 2 * r), full2),
            pl.BlockSpec((2 * r, 2 * r), full2),
            pl.BlockSpec((r, r), full2),
            pl.BlockSpec((1, LANES), lambda j, i: (0, j)),
        ],
        out_specs=pl.BlockSpec((1, l, LANES), lambda j, i: (i, 0, j)),
        out_shape=jax.ShapeDtypeStruct((b, l, c), BF16),
        scratch_shapes=[
            pltpu.VMEM((LANES * r // 2, r), F32),
            pltpu.VMEM((CH_HALF * 2 * r, r), BF16),
            pltpu.VMEM((CH_HALF * r, 2 * r), BF16),
        ],
        compiler_params=_cparams(("parallel", "parallel")),
        name="hy_conv",
    )(z, x0c, kf, consts["m1"], consts["twr"], consts["twi"], consts["m2"], consts["g"], consts["h2"], hy_bias)


DEN_ROWS = 16


def _rope(x, c, s):
    return x * c + pltpu.roll(x, HEAD_DIM // 2, axis=1) * s


def _attn_body(q_ref, k_ref, v_ref, cos_ref, sin_ref, gq_ref, gk_ref, o_ref,
               kn_scr, vt_scr, qt_scr, sa_scr, sb_scr, m_scr, acc_scr, *, tq, tk, l):
    qi = pl.program_id(2)
    nk = l // tk

    @pl.when(qi == 0)
    def _():
        for r in range(nk):
            sl = slice(r * tk, (r + 1) * tk)
            kn = _rms(k_ref[0, sl, :].astype(F32), gk_ref[...])
            kn_scr[sl, :] = _rope(kn, cos_ref[sl, :], sin_ref[sl, :]).astype(BF16)
            vt_scr[r, :HEAD_DIM, :] = v_ref[0, sl, :].astype(F32).T.astype(BF16)
            vt_scr[r, HEAD_DIM:, :] = jnp.ones((DEN_ROWS, tk), BF16)

    row0 = pl.multiple_of(qi * tq, tq)
    c = cos_ref[pl.ds(row0, tq), :]
    s = sin_ref[pl.ds(row0, tq), :]
    scale = HEAD_DIM ** -0.5 * math.log2(math.e)
    for g in range(GROUP):
        qn = _rms(q_ref[0, :, g * HEAD_DIM:(g + 1) * HEAD_DIM].astype(F32), gq_ref[...])
        qt_scr[:, g * tq:(g + 1) * tq] = (_rope(qn, c, s) * scale).T.astype(BF16)

    m_scr[...] = jnp.full(m_scr.shape, -jnp.inf, F32)
    acc_scr[...] = jnp.zeros(acc_scr.shape, F32)
    nq = GROUP * tq

    def scores(j):
        kc = kn_scr[pl.ds(pl.multiple_of(j * tk, tk), tk), :]
        return jnp.dot(kc, qt_scr[...], preferred_element_type=F32)

    sa_scr[...] = scores(0)

    def half_step(j, cur, nxt):
        nxt[...] = scores(jnp.minimum(j + 1, nk - 1))
        st = cur[...]
        m_prev = m_scr[...]
        m_new = jnp.maximum(m_prev, jnp.max(st, axis=0, keepdims=True))
        alpha = jnp.exp2(m_prev - m_new)
        p = jnp.exp2(st - m_new)
        acc_scr[...] = alpha * acc_scr[...] + jnp.dot(vt_scr[j], p.astype(BF16), preferred_element_type=F32)
        m_scr[...] = m_new

    def kv_pair(jj, carry):
        half_step(2 * jj, sa_scr, sb_scr)
        half_step(2 * jj + 1, sb_scr, sa_scr)
        return carry

    lax.fori_loop(0, nk // 2, kv_pair, 0)
    o = acc_scr[:HEAD_DIM, :] / acc_scr[HEAD_DIM:HEAD_DIM + 1, :]
    for g in range(GROUP):
        o_ref[0, :, g * HEAD_DIM:(g + 1) * HEAD_DIM] = o[:, g * tq:(g + 1) * tq].T.astype(o_ref.dtype)


def attention(proj3, cos2, sin2, g_q, g_k, q_col, k_col, v_col, n_kv, tq=256, tk=1024):
    b, l, _ = proj3.shape
    gw = GROUP * HEAD_DIM
    body = functools.partial(_attn_body, tq=tq, tk=tk, l=l)
    qb, kb, vb = q_col // gw, k_col // HEAD_DIM, v_col // HEAD_DIM
    return pl.pallas_call(
        body,
        grid=(b, n_kv, l // tq),
        in_specs=[
            pl.BlockSpec((1, tq, gw), lambda i, h, q: (i, q, qb + h)),
            pl.BlockSpec((1, l, HEAD_DIM), lambda i, h, q: (i, 0, kb + h)),
            pl.BlockSpec((1, l, HEAD_DIM), lambda i, h, q: (i, 0, vb + h)),
            pl.BlockSpec((l, HEAD_DIM), lambda i, h, q: (0, 0)),
            pl.BlockSpec((l, HEAD_DIM), lambda i, h, q: (0, 0)),
            pl.BlockSpec((1, HEAD_DIM), lambda i, h, q: (0, 0)),
            pl.BlockSpec((1, HEAD_DIM), lambda i, h, q: (0, 0)),
        ],
        out_specs=pl.BlockSpec((1, tq, gw), lambda i, h, q: (i, q, h)),
        out_shape=jax.ShapeDtypeStruct((b, l, n_kv * gw), BF16),
        scratch_shapes=[
            pltpu.VMEM((l, HEAD_DIM), BF16),
            pltpu.VMEM((l // tk, HEAD_DIM + DEN_ROWS, tk), BF16),
            pltpu.VMEM((HEAD_DIM, GROUP * tq), BF16),
            pltpu.VMEM((tk, GROUP * tq), F32),
            pltpu.VMEM((tk, GROUP * tq), F32),
            pltpu.VMEM((1, GROUP * tq), F32),
            pltpu.VMEM((HEAD_DIM + DEN_ROWS, GROUP * tq), F32),
        ],
        compiler_params=_cparams(("parallel", "parallel", "arbitrary")),
        name="attention",
    )(proj3, proj3, proj3, cos2, sin2, g_q, g_k)


def _merge_body(yh_ref, ya_ref, gh_ref, ga_ref, ha_ref, hb_ref, whb_ref, wab_ref, wo_ref, gf_ref,
                wrh_ref, wrl_ref, br_ref, h1_ref, xn_ref, meta_ref, slott_ref, tw_ref, cnt_ref, *, tm, nba):
    a = jnp.dot(yh_ref[...], whb_ref[...], preferred_element_type=F32)
    bmat = jnp.dot(ya_ref[...], wab_ref[...], preferred_element_type=F32)
    mix = jax.nn.sigmoid(gh_ref[...].astype(F32)) * a + jax.nn.sigmoid(ga_ref[...].astype(F32)) * bmat
    h0 = jnp.where(pl.program_id(0) < nba, ha_ref[...], hb_ref[...])
    h1 = h0 + jnp.dot(mix.astype(BF16), wo_ref[...], preferred_element_type=F32)
    h1_ref[...] = h1
    xn = _rms(h1, gf_ref[...])
    xn_ref[...] = xn.astype(xn_ref.dtype)

    x_hi = xn.astype(BF16)
    x_lo = (xn - x_hi.astype(F32)).astype(BF16)
    logits = (jnp.dot(x_hi, wrh_ref[...], preferred_element_type=F32)
              + jnp.dot(x_lo, wrh_ref[...], preferred_element_type=F32)
              + jnp.dot(x_hi, wrl_ref[...], preferred_element_type=F32)) + br_ref[...]
    lt = logits.T
    row = lax.broadcasted_iota(I32, lt.shape, 0)
    work = lt
    vals, idxs = [], []
    sel = jnp.zeros(lt.shape, F32)
    for _ in range(TOP_K):
        m = jnp.max(work, axis=0, keepdims=True)
        idx = jnp.min(jnp.where(work == m, row, LANES), axis=0, keepdims=True)
        hit = row == idx
        vals.append(m)
        idxs.append(idx)
        sel = sel + hit.astype(F32)
        work = jnp.where(hit, -jnp.inf, work)
    ex = [jnp.exp(v - vals[0]) for v in vals]
    den = ex[0] + ex[1] + ex[2] + ex[3]

    rr = lax.broadcasted_iota(I32, (tm, tm), 0)
    cc = lax.broadcasted_iota(I32, (tm, tm), 1)
    before = (rr < cc).astype(BF16)
    prefix = jnp.dot(sel.astype(BF16), before, preferred_element_type=F32)
    units = jnp.floor((jnp.sum(sel, axis=1, keepdims=True) + (RUN_ALIGN - 1)) * (1.0 / RUN_ALIGN))
    units_b = jnp.broadcast_to(units, (LANES, LANES))
    cnt_ref[...] = (units_b.T * RUN_ALIGN)[0:8, :]
    er = lax.broadcasted_iota(I32, (LANES, LANES), 0)
    ec = lax.broadcasted_iota(I32, (LANES, LANES), 1)
    lower = (ec < er).astype(BF16)
    eoff = RUN_ALIGN * jnp.dot(lower, units_b.astype(BF16), preferred_element_type=F32)[:, 0:1]
    slot_of = prefix + eoff

    meta_t = jnp.full(lt.shape, -1.0, F32)
    tw_t = jnp.zeros(lt.shape, F32)
    for k in range(TOP_K):
        slot = jnp.sum(jnp.where(row == idxs[k], slot_of, 0.0), axis=0, keepdims=True)
        meta_t = jnp.where(row == k, slot, meta_t)
        tw_t = jnp.where(row == k, ex[k] / den, tw_t)
    slott_ref[0] = meta_t[0:8, :].astype(I32)
    meta_ref[...] = meta_t.T.astype(I32)
    tw_ref[...] = tw_t.T


def merge_router(y_hy, y_at, proj, xa, xb, whb, wab, wo, g_ffn, wr_hi, wr_lo, br_pad, gate_col, tm):
    d = xa.shape[1]
    n = xa.shape[0] + xb.shape[0]
    nba = xa.shape[0] // tm
    ch = y_hy.shape[1]
    ca = y_at.shape[1]
    gb = gate_col // d
    body = functools.partial(_merge_body, tm=tm, nba=nba)
    full = lambda i: (0, 0)
    row = lambda i: (i, 0)
    return pl.pallas_call(
        body,
        grid=(n // tm,),
        in_specs=[
            pl.BlockSpec((tm, ch), row),
            pl.BlockSpec((tm, ca), row),
            pl.BlockSpec((tm, d), lambda i: (i, gb)),
            pl.BlockSpec((tm, d), lambda i: (i, gb + 1)),
        ] + _pair_specs(tm, d, nba) + [
            pl.BlockSpec((ch, d), full),
            pl.BlockSpec((ca, d), full),
            pl.BlockSpec((d, d), full),
            pl.BlockSpec((1, d), full),
            pl.BlockSpec((d, LANES), full),
            pl.BlockSpec((d, LANES), full),
            pl.BlockSpec((1, LANES), full),
        ],
        out_specs=[
            pl.BlockSpec((tm, d), row),
            pl.BlockSpec((tm, d), row),
            pl.BlockSpec((tm, LANES), row),
            pl.BlockSpec((1, 8, tm), lambda i: (i, 0, 0)),
            pl.BlockSpec((tm, LANES), row),
            pl.BlockSpec((8, LANES), row),
        ],
        out_shape=[
            jax.ShapeDtypeStruct((n, d), F32),
            jax.ShapeDtypeStruct((n, d), BF16),
            jax.ShapeDtypeStruct((n, LANES), I32),
            jax.ShapeDtypeStruct((n // tm, 8, tm), I32),
            jax.ShapeDtypeStruct((n, LANES), F32),
            jax.ShapeDtypeStruct((n // tm * 8, LANES), F32),
        ],
        compiler_params=_cparams(("parallel",)),
        name="merge_router",
    )(y_hy, y_at, proj, proj, xa, xb, whb, wab, wo, g_ffn, wr_hi, wr_lo, br_pad)


RUN_ALIGN = 8
RUN_BITS = ROUTE_BLOCK.bit_length()
SORT_ROWS = TOP_K * ROUTE_BLOCK + N_EXPERTS * RUN_ALIGN


def _run_copies(i, cnt_ref, dst_ref, make_copy, op):
    def per_expert(e, local):
        c = cnt_ref[i * N_EXPERTS + e]
        d = dst_ref[i * N_EXPERTS + e]
        for bit in range(RUN_ALIGN.bit_length() - 1, RUN_BITS):
            size = 1 << bit

            @pl.when((c & size) != 0)
            def _(size=size):
                done = c & (size - 1)
                op(make_copy(pl.multiple_of(local + done, RUN_ALIGN), pl.multiple_of(d + done, RUN_ALIGN), size))

        return local + c

    lax.fori_loop(0, N_EXPERTS, per_expert, 0)


def _pack_bf16_pairs(x):
    half = x.shape[1] // 2
    hi = lax.bitcast_convert_type(x[:, :half], U32)
    lo = lax.bitcast_convert_type(x[:, half:], U32)
    return hi | (lo >> 16)


def _unpack_bf16_pairs(w):
    hi = lax.bitcast_convert_type(w & jnp.uint32(0xFFFF0000), F32).astype(BF16)
    lo = lax.bitcast_convert_type(w << 16, F32).astype(BF16)
    return jnp.concatenate([hi, lo], axis=1)


def _dispatch_body(cnt_ref, dst_ref, pend_ref, padded_ref, nused_ref, slot_ref, xn_ref, xs_hbm,
                   sort_scr, zero_scr, sems, zsem, *, tm, nblk):
    i = pl.program_id(0)
    last = pl.num_programs(0) - 1
    buf = i % 2

    @pl.when(i == 0)
    def _():
        zero_scr[...] = jnp.zeros(zero_scr.shape, U32)

        def zero_block(start):
            cp = pltpu.make_async_copy(zero_scr, xs_hbm.at[pl.ds(start, EXPERT_BLOCK), :], zsem)
            cp.start()
            cp.wait()

        def pad_rows(e, carry):
            @pl.when(padded_ref[e] > 0)
            def _():
                zero_block(pl.multiple_of(pend_ref[e] - EXPERT_BLOCK, EXPERT_BLOCK))

            return carry

        lax.fori_loop(0, N_EXPERTS, pad_rows, 0)

        def tail(bk, carry):
            zero_block(pl.multiple_of(bk * EXPERT_BLOCK, EXPERT_BLOCK))
            return carry

        lax.fori_loop(nused_ref[0], nblk, tail, 0)

    slots = slot_ref[0]
    j = lax.broadcasted_iota(I32, (SORT_ROWS, tm), 0)
    perm = jnp.zeros((SORT_ROWS, tm), F32)
    for k in range(TOP_K):
        perm = perm + (j == slots[k:k + 1, :]).astype(F32)
    srt = jnp.dot(perm.astype(BF16), xn_ref[...], preferred_element_type=F32)
    sort_scr[buf] = _pack_bf16_pairs(srt)

    def copier(b):
        def make_copy(local, d, size):
            return pltpu.make_async_copy(sort_scr.at[b, pl.ds(local, size), :], xs_hbm.at[pl.ds(d, size), :],
                                         sems.at[b])
        return make_copy

    _run_copies(i, cnt_ref, dst_ref, copier(buf), lambda cp: cp.start())

    @pl.when(i > 0)
    def _():
        _run_copies(i - 1, cnt_ref, dst_ref, copier(1 - buf), lambda cp: cp.wait())

    @pl.when(i == last)
    def _():
        _run_copies(i, cnt_ref, dst_ref, copier(buf), lambda cp: cp.wait())


def dispatch(xn, slots_t, cnt_flat, dst_flat, pad_end, padded, nused, cap, tm):
    n, d = xn.shape
    nblk = cap // EXPERT_BLOCK
    body = functools.partial(_dispatch_body, tm=tm, nblk=nblk)
    return pl.pallas_call(
        body,
        grid_spec=pltpu.PrefetchScalarGridSpec(
            num_scalar_prefetch=5,
            grid=(n // tm,),
            in_specs=[
                pl.BlockSpec((1, 8, tm), lambda i, *_: (i, 0, 0)),
                pl.BlockSpec((tm, d), lambda i, *_: (i, 0)),
            ],
            out_specs=pl.BlockSpec(memory_space=pl.ANY),
            scratch_shapes=[
                pltpu.VMEM((2, SORT_ROWS, d // 2), U32),
                pltpu.VMEM((EXPERT_BLOCK, d // 2), U32),
                pltpu.SemaphoreType.DMA((2,)),
                pltpu.SemaphoreType.DMA(()),
            ],
        ),
        out_shape=jax.ShapeDtypeStruct((cap, d // 2), U32),
        compiler_params=_cparams(("arbitrary",)),
        name="dispatch",
    )(cnt_flat, dst_flat, pad_end, padded, nused, slots_t, xn)


def _experts_body(be_ref, nu_ref, x_ref, wg_ref, bg_ref, wu_ref, bu_ref, wd_ref, bd_ref, o_ref):
    i = pl.program_id(0)
    f = pl.program_id(1)
    used = i < nu_ref[0]

    def down_proj():
        xb = _unpack_bf16_pairs(x_ref[...])
        g = jnp.dot(xb, wg_ref[0], preferred_element_type=F32) + bg_ref[0]
        u = jnp.dot(xb, wu_ref[0], preferred_element_type=F32) + bu_ref[0]
        g = jnp.minimum(g, SWIGLU_LIMIT)
        u = jnp.clip(u, -SWIGLU_LIMIT, SWIGLU_LIMIT)
        a = g * jax.nn.sigmoid(SWIGLU_ALPHA * g) * (u + 1.0)
        return jnp.dot(a.astype(BF16), wd_ref[0], preferred_element_type=F32)

    @pl.when(jnp.logical_and(used, f == 0))
    def _():
        o_ref[...] = down_proj() + bd_ref[0]

    @pl.when(jnp.logical_and(used, f > 0))
    def _():
        o_ref[...] = o_ref[...] + down_proj()

    @pl.when(jnp.logical_not(used))
    def _():
        o_ref[...] = jnp.zeros(o_ref.shape, F32)


def experts(xs, blk_e, nused, wg, bg, wu, bu, wd, bd, tf=1024):
    cap = xs.shape[0]
    d, dff = wg.shape[1], wg.shape[2]
    nblk = cap // EXPERT_BLOCK
    nf = dff // tf

    def last_used(i, nu):
        return jnp.maximum(jnp.minimum(i, nu[0] - 1), 0)

    def eidx(i, be, nu):
        return be[last_used(i, nu)]

    def fidx(i, f, nu):
        return jnp.where(i < nu[0], f, nf - 1)

    return pl.pallas_call(
        _experts_body,
        grid_spec=pltpu.PrefetchScalarGridSpec(
            num_scalar_prefetch=2,
            grid=(nblk, nf),
            in_specs=[
                pl.BlockSpec((EXPERT_BLOCK, d // 2), lambda i, f, be, nu: (last_used(i, nu), 0)),
                pl.BlockSpec((1, d, tf), lambda i, f, be, nu: (eidx(i, be, nu), 0, fidx(i, f, nu))),
                pl.BlockSpec((1, 1, tf), lambda i, f, be, nu: (eidx(i, be, nu), 0, fidx(i, f, nu))),
                pl.BlockSpec((1, d, tf), lambda i, f, be, nu: (eidx(i, be, nu), 0, fidx(i, f, nu))),
                pl.BlockSpec((1, 1, tf), lambda i, f, be, nu: (eidx(i, be, nu), 0, fidx(i, f, nu))),
                pl.BlockSpec((1, tf, d), lambda i, f, be, nu: (eidx(i, be, nu), fidx(i, f, nu), 0)),
                pl.BlockSpec((1, 1, d), lambda i, f, be, nu: (eidx(i, be, nu), 0, 0)),
            ],
            out_specs=pl.BlockSpec((EXPERT_BLOCK, d), lambda i, f, be, nu: (i, 0)),
        ),
        out_shape=jax.ShapeDtypeStruct((cap, d), F32),
        compiler_params=_cparams(("arbitrary", "arbitrary")),
        name="experts",
    )(blk_e, nused, xs, wg, bg, wu, bu, wd, bd)


def _combine_body(cnt_ref, dst_ref, slot_ref, h_ref, tw_ref, pa_ref, pb_ref, y_hbm, wpg_ref, wpp_ref, gp_ref,
                  gfin_ref, oa_ref, ob_ref, gath, sems, *, tm, nba):
    i = pl.program_id(0)
    last = pl.num_programs(0) - 1
    buf = i % 2

    def copier(b):
        def make_copy(local, d, size):
            return pltpu.make_async_copy(y_hbm.at[pl.ds(d, size), :], gath.at[b, pl.ds(local, size), :],
                                         sems.at[b])
        return make_copy

    def fetch(step, b):
        gath[b, TOP_K * tm:, :] = jnp.zeros((SORT_ROWS - TOP_K * tm, gath.shape[2]), F32)
        _run_copies(step, cnt_ref, dst_ref, copier(b), lambda cp: cp.start())

    @pl.when(i == 0)
    def _():
        fetch(0, 0)

    @pl.when(i < last)
    def _():
        fetch(i + 1, 1 - buf)

    _run_copies(i, cnt_ref, dst_ref, copier(buf), lambda cp: cp.wait())

    tw = tw_ref[...]
    slots = slot_ref[...]
    lane = lax.broadcasted_iota(I32, (tm, SORT_ROWS), 1)
    wmat = jnp.zeros((tm, SORT_ROWS), F32)
    for k in range(TOP_K):
        wmat = wmat + jnp.where(lane == slots[:, k:k + 1], tw[:, k:k + 1], 0.0)
    moe = jnp.dot(wmat.astype(BF16), gath[buf].astype(BF16), preferred_element_type=F32)
    h2 = h_ref[...] + moe
    a = _rms(h2, gp_ref[...]).astype(BF16)
    gate = jax.nn.sigmoid(jnp.dot(a, wpg_ref[...], preferred_element_type=F32))
    pblk = jnp.where(i < nba, pa_ref[...], pb_ref[...])
    pp = jnp.dot(pblk.astype(BF16), wpp_ref[...], preferred_element_type=F32)
    h3 = h2 + gate * pp
    res = _rms(h3, gfin_ref[...])

    @pl.when(i < nba)
    def _():
        oa_ref[...] = res

    @pl.when(i >= nba)
    def _():
        ob_ref[...] = res


def combine_ple(h1, slots, tw, pa, pb, y, cnt_flat, dst_flat, wpg, wpp, g_ple, g_final, tm):
    n, d = h1.shape
    pd = pa.shape[1]
    nba = pa.shape[0] // tm
    body = functools.partial(_combine_body, tm=tm, nba=nba)
    full = lambda i, *_: (0, 0)
    row = lambda i, *_: (i, 0)
    return pl.pallas_call(
        body,
        grid_spec=pltpu.PrefetchScalarGridSpec(
            num_scalar_prefetch=2,
            grid=(n // tm,),
            in_specs=[
                pl.BlockSpec((tm, LANES), row),
                pl.BlockSpec((tm, d), row),
                pl.BlockSpec((tm, LANES), row),
            ] + _pair_specs(tm, pd, nba) + [
                pl.BlockSpec(memory_space=pl.ANY),
                pl.BlockSpec((d, d), full, pipeline_mode=pl.Buffered(1)),
                pl.BlockSpec((pd, d), full, pipeline_mode=pl.Buffered(1)),
                pl.BlockSpec((1, d), full),
                pl.BlockSpec((1, d), full),
            ],
            out_specs=_pair_specs(tm, d, nba),
            scratch_shapes=[pltpu.VMEM((2, SORT_ROWS, d), F32), pltpu.SemaphoreType.DMA((2,))],
        ),
        out_shape=[jax.ShapeDtypeStruct((pa.shape[0], d), F32), jax.ShapeDtypeStruct((pb.shape[0], d), F32)],
        compiler_params=_cparams(("arbitrary",)),
        name="combine_ple",
    )(cnt_flat, dst_flat, slots, h1, tw, pa, pb, y, wpg, wpp, g_ple, g_final)


def _rope_tables(l):
    rows = l // GRID_W
    row = jnp.repeat(jnp.arange(rows, dtype=F32), GRID_W)
    col = jnp.tile(jnp.arange(GRID_W, dtype=F32), rows)
    axis_dim = HEAD_DIM // 2
    inv = 1.0 / (ROPE_THETA ** (jnp.arange(0, axis_dim, 2, dtype=F32) / axis_dim))
    ang = jnp.concatenate([row[:, None] * inv, col[:, None] * inv], axis=-1)
    c, s = jnp.cos(ang), jnp.sin(ang)
    return jnp.concatenate([c, c], axis=-1), jnp.concatenate([-s, s], axis=-1)


def _filter_features(l):
    t = jnp.linspace(0.0, 1.0, l, dtype=F32)[:, None]
    w = 2.0 * math.pi * jnp.arange(l, dtype=F32)[:, None] / l
    f = jnp.linspace(1e-4, FILTER_BANDS - 1, FILTER_BANDS, dtype=F32)[None, :]
    z = jnp.concatenate([t, jnp.cos(f * w), -jnp.sin(f * w)], axis=-1)
    pos = jnp.concatenate([jnp.arange(l), jnp.zeros((1,), jnp.int32), jnp.arange(l - 1, 0, -1)])
    mask = jnp.ones((2 * l, 1), F32).at[l, 0].set(0.0)
    z2 = jnp.concatenate([z[pos], mask], axis=-1)
    return jnp.pad(z2, ((0, 0), (0, LANES - z2.shape[1])))


def kernel(x_prompt, x_sample, p_prompt, p_sample, g_mix, w_in, w_short, b_short, w_f1, b_f1, w_f2, b_f2,
           w_f3, b_f3, w_f4, filter_freq, hy_bias, g_q, g_k, w_hy_br, w_at_br, w_out, g_ffn, w_router,
           b_router, w_gate, b_gate, w_up, b_up, w_down, b_down, g_ple, w_ple_gate, w_ple_proj, g_final):
    assert w_in.shape[0] == 1, "single layer"
    l, d = x_prompt.shape[1], x_prompt.shape[2]
    c = w_hy_br.shape[1]
    aw = w_at_br.shape[1]
    n_kv = aw // (GROUP * HEAD_DIM)
    kvw = n_kv * HEAD_DIM
    assert 2 * l == FFT_R * FFT_R and x_sample.shape[1] == l

    ba, bb = x_prompt.shape[0], x_sample.shape[0]
    b = ba + bb
    n = b * l
    xa, xb = x_prompt.reshape(ba * l, d), x_sample.reshape(bb * l, d)
    pa, pb = p_prompt[0].reshape(ba * l, -1), p_sample[0].reshape(bb * l, -1)

    o_q, o_k, o_v, o_g = 3 * c, 3 * c + aw, 3 * c + aw + kvw, 3 * c + aw + 2 * kvw
    wi = w_in[0]
    w_perm = jnp.concatenate([wi[:, :o_k], wi[:, o_g:], wi[:, o_k:o_g]], axis=1).astype(BF16)
    w_perm = jnp.pad(w_perm, ((0, 0), (0, -w_perm.shape[1] % INPROJ_TN)))
    q_col, gate_col = o_q, o_k
    k_col = gate_col + 2 * d
    v_col = k_col + kvw

    proj = in_proj(xa, xb, g_mix, w_perm, tn=INPROJ_TN)
    proj3 = proj.reshape(b, l, proj.shape[1])

    consts = _dft_constants()
    z, x0c = hy_prep(proj3, w_short[0], b_short, c)
    w1p = jnp.pad(w_f1[0], ((0, LANES - FILTER_EMB), (0, 0)))
    deltas = jnp.linspace(math.log(DECAY_TARGET) / DECAY_SLOW, math.log(DECAY_TARGET) / DECAY_FAST, c, dtype=F32)
    kt = hy_filter(_filter_features(l), w1p, b_f1, w_f2[0], b_f2, w_f3[0], b_f3, filter_freq, w_f4[0],
                   jnp.abs(deltas)[None, :], c)
    kf = hy_spectrum(kt, consts)
    y_hy = hy_conv(z, x0c, kf, hy_bias, consts)

    cos2, sin2 = _rope_tables(l)
    y_at = attention(proj3, cos2, sin2, g_q, g_k, q_col, k_col, v_col, n_kv)

    oa, ob = _merge_moe_ple(y_hy.reshape(n, c), y_at.reshape(n, aw), proj, gate_col, xa, xb, pa, pb,
                            w_hy_br[0], w_at_br[0], w_out[0], g_ffn, w_router[0], b_router, w_gate[0], b_gate[0],
                            w_up[0], b_up[0], w_down[0], b_down[0], g_ple, w_ple_gate[0], w_ple_proj[0], g_final)
    return oa.reshape(ba, l, d), ob.reshape(bb, l, d)


def _merge_moe_ple(y_hy, y_at, proj, gate_col, xa, xb, pa, pb, w_hy_br, w_at_br, w_out, g_ffn, w_router, b_router,
                   w_gate, b_gate, w_up, b_up, w_down, b_down, g_ple, w_ple_gate, w_ple_proj, g_final):
    n = xa.shape[0] + xb.shape[0]
    wr_pad = jnp.pad(w_router, ((0, 0), (0, LANES - N_EXPERTS)))
    wr_hi = wr_pad.astype(BF16)
    wr_lo = (wr_pad - wr_hi.astype(F32)).astype(BF16)
    br_pad = jnp.pad(b_router, ((0, 0), (0, LANES - N_EXPERTS)), constant_values=-1e30)
    h1, xn, slots, slots_t, tw, cnt = merge_router(
        y_hy, y_at, proj, xa, xb, w_hy_br.astype(BF16), w_at_br.astype(BF16),
        w_out.astype(BF16), g_ffn, wr_hi, wr_lo, br_pad, gate_col, ROUTE_BLOCK)

    nrb = n // ROUTE_BLOCK
    cnt_be = cnt.reshape(nrb, 8, LANES)[:, 0, :N_EXPERTS].astype(I32)
    counts = jnp.sum(cnt_be, axis=0)
    padded = (counts + EXPERT_BLOCK - 1) // EXPERT_BLOCK * EXPERT_BLOCK
    pad_end = jnp.cumsum(padded).astype(I32)
    pad_start = pad_end - padded
    before = jnp.cumsum(cnt_be, axis=0) - cnt_be
    dst_be = pad_start[None, :] + before
    cap = nrb * SORT_ROWS + N_EXPERTS * EXPERT_BLOCK
    nblk = cap // EXPERT_BLOCK
    nused = (pad_end[-1:] // EXPERT_BLOCK).astype(I32)
    blk_start = jnp.arange(nblk, dtype=I32) * EXPERT_BLOCK
    blk_e = jnp.minimum(jnp.sum((pad_end[None, :] <= blk_start[:, None]).astype(I32), axis=1), N_EXPERTS - 1)
    cnt_flat = cnt_be.reshape(-1)
    dst_flat = dst_be.reshape(-1).astype(I32)

    xs = dispatch(xn, slots_t, cnt_flat, dst_flat, pad_end, padded, nused, cap, ROUTE_BLOCK)
    ys = experts(xs, blk_e, nused, w_gate.astype(BF16), b_gate[:, None, :], w_up.astype(BF16),
                 b_up[:, None, :], w_down.astype(BF16), b_down[:, None, :])
    return combine_ple(h1, slots, tw, pa, pb, ys, cnt_flat, dst_flat, w_ple_gate.astype(BF16),
                       w_ple_proj.astype(BF16), g_ple, g_final[None, :], ROUTE_BLOCK)
```

```python
import functools
import math

import jax
import jax.numpy as jnp
import numpy as np
from jax import lax
from jax.experimental import pallas as pl
from jax.experimental.pallas import tpu as pltpu

F32 = jnp.float32
BF16 = jnp.bfloat16
I32 = jnp.int32
U32 = jnp.uint32

EPS = 1e-6
HEAD_DIM = 128
GROUP = 4
GRID_W = 64
ROPE_THETA = 10000.0
HY_SHORT = 3
FILTER_EMB = 33
FILTER_BANDS = 16
DECAY_FAST = 0.3
DECAY_SLOW = 1.5
DECAY_TARGET = 1e-2
N_EXPERTS = 32
TOP_K = 4
SWIGLU_LIMIT = 7.0
SWIGLU_ALPHA = 1.702

LANES = 128
FFT_R = 128
EXPERT_BLOCK = 512
ROUTE_BLOCK = 256
INPROJ_TN = 1024
VMEM_LIMIT = 56 * 1024 * 1024


def _cparams(sem, vmem=VMEM_LIMIT):
    return pltpu.CompilerParams(dimension_semantics=sem, vmem_limit_bytes=vmem)


def _rms(x, g):
    return x * lax.rsqrt(jnp.mean(x * x, axis=-1, keepdims=True) + EPS) * g


def _pair_specs(tm, width, nba, **kw):
    return [
        pl.BlockSpec((tm, width), lambda i, *_: (jnp.minimum(i, nba - 1), 0), **kw),
        pl.BlockSpec((tm, width), lambda i, *_: (jnp.maximum(i - nba, 0), 0), **kw),
    ]


def _inproj_body(xa_ref, xb_ref, g_ref, w_ref, o_ref, a_scr, *, nba):
    @pl.when(pl.program_id(1) == 0)
    def _():
        x = jnp.where(pl.program_id(0) < nba, xa_ref[...], xb_ref[...])
        a_scr[...] = _rms(x, g_ref[...]).astype(BF16)

    o_ref[...] = jnp.dot(a_scr[...], w_ref[...], preferred_element_type=F32).astype(o_ref.dtype)


def in_proj(xa, xb, g, w_bf16, tm=1024, tn=1024):
    d = xa.shape[1]
    n = xa.shape[0] + xb.shape[0]
    nba = xa.shape[0] // tm
    nout = w_bf16.shape[1]
    return pl.pallas_call(
        functools.partial(_inproj_body, nba=nba),
        grid=(n // tm, nout // tn),
        in_specs=_pair_specs(tm, d, nba, pipeline_mode=pl.Buffered(1)) + [
            pl.BlockSpec((1, d), lambda i, j: (0, 0)),
            pl.BlockSpec((d, tn), lambda i, j: (0, j)),
        ],
        out_specs=pl.BlockSpec((tm, tn), lambda i, j: (i, j)),
        out_shape=jax.ShapeDtypeStruct((n, nout), BF16),
        scratch_shapes=[pltpu.VMEM((tm, d), BF16)],
        compiler_params=_cparams(("parallel", "arbitrary")),
        name="in_proj",
    )(xa, xb, g, w_bf16)


def _hyprep_body(u_ref, up_ref, un_ref, w_ref, b_ref, z_ref, x0_ref, *, tr, c):
    r = pl.program_id(1)
    nr = pl.num_programs(1)
    u = u_ref[0].astype(F32)
    hp = up_ref[0][15:16, :].astype(F32)
    hn = un_ref[0][0:1, :].astype(F32)
    hp = jnp.where(r == 0, 0.0, hp)
    hn = jnp.where(r == nr - 1, 0.0, hn)
    row = lax.broadcasted_iota(I32, u.shape, 0)
    prev = jnp.where(row == 0, hp, pltpu.roll(u, 1, axis=0))
    nxt = jnp.where(row == tr - 1, hn, pltpu.roll(u, tr - 1, axis=0))
    w = w_ref[...]
    uc = b_ref[...] + prev * w[0:1] + u * w[1:2] + nxt * w[2:3]
    x0 = uc[:, :c]
    x1 = uc[:, c:2 * c]
    v = uc[:, 2 * c:]
    z_ref[0] = (v * x1).astype(z_ref.dtype)
    x0_ref[0] = x0.astype(x0_ref.dtype)


def hy_prep(proj3, w_short, b_short, c, tr=256):
    b, l, _ = proj3.shape
    hb = tr // 16
    nh = l // 16
    body = functools.partial(_hyprep_body, tr=tr, c=c)
    return pl.pallas_call(
        body,
        grid=(b, l // tr),
        in_specs=[
            pl.BlockSpec((1, tr, 3 * c), lambda i, r: (i, r, 0)),
            pl.BlockSpec((1, 16, 3 * c), lambda i, r: (i, jnp.maximum(r * hb - 1, 0), 0)),
            pl.BlockSpec((1, 16, 3 * c), lambda i, r: (i, jnp.minimum((r + 1) * hb, nh - 1), 0)),
            pl.BlockSpec((HY_SHORT, 3 * c), lambda i, r: (0, 0)),
            pl.BlockSpec((1, 3 * c), lambda i, r: (0, 0)),
        ],
        out_specs=[
            pl.BlockSpec((1, tr, c), lambda i, r: (i, r, 0)),
            pl.BlockSpec((1, tr, c), lambda i, r: (i, r, 0)),
        ],
        out_shape=[jax.ShapeDtypeStruct((b, l, c), BF16), jax.ShapeDtypeStruct((b, l, c), BF16)],
        compiler_params=_cparams(("parallel", "parallel")),
        name="hy_prep",
    )(proj3, proj3, proj3, w_short, b_short)


def _hyfilter_body(z_ref, w1_ref, b1_ref, w2_ref, b2_ref, w3_ref, b3_ref, fr_ref, w4_ref, dl_ref, o_ref):
    hi = lax.Precision.HIGHEST
    zb = z_ref[...]
    fr = fr_ref[...]
    h = jnp.sin(fr * (jnp.dot(zb, w1_ref[...], precision=hi, preferred_element_type=F32) + b1_ref[...]))
    h = jnp.sin(fr * (jnp.dot(h, w2_ref[...], precision=hi, preferred_element_type=F32) + b2_ref[...]))
    h = jnp.sin(fr * (jnp.dot(h, w3_ref[...], precision=hi, preferred_element_type=F32) + b3_ref[...]))
    h4 = jnp.dot(h, w4_ref[...], precision=hi, preferred_element_type=F32)
    t = zb[:, 0:1]
    mask = zb[:, FILTER_EMB:FILTER_EMB + 1]
    o_ref[...] = h4 * jnp.exp(-t * dl_ref[...]) * mask


def hy_filter(zfeat, w1p, b1, w2, b2, w3, b3, freq, w4, absdelta, c, tr=1024):
    rows = zfeat.shape[0]
    half_blocks = rows // 2 // tr
    fo = w2.shape[0]
    full = lambda i: (0, 0)
    return pl.pallas_call(
        _hyfilter_body,
        grid=(rows // tr,),
        in_specs=[
            pl.BlockSpec((tr, LANES), lambda i: (i, 0)),
            pl.BlockSpec((LANES, fo), full),
            pl.BlockSpec((1, fo), full),
            pl.BlockSpec((fo, fo), full),
            pl.BlockSpec((1, fo), full),
            pl.BlockSpec((fo, fo), full),
            pl.BlockSpec((1, fo), full),
            pl.BlockSpec((1, fo), full),
            pl.BlockSpec((fo, c), lambda i: (0, i // half_blocks)),
            pl.BlockSpec((1, c), full),
        ],
        out_specs=pl.BlockSpec((tr, c), lambda i: (i, 0)),
        out_shape=jax.ShapeDtypeStruct((rows, c), F32),
        compiler_params=_cparams(("parallel",)),
        name="hy_filter",
    )(zfeat, w1p, b1, w2, b2, w3, b3, freq, w4, absdelta)


def _dft_constants():
    r = FFT_R
    n = r * r
    k = np.arange(r)
    ang = -2.0 * np.pi * np.outer(k, k) / r
    fr, fi = np.cos(ang), np.sin(ang)
    m1 = np.concatenate([fr, fi], axis=0)
    tw = -2.0 * np.pi * np.outer(k, k) / n
    m2 = np.concatenate([fr, fi], axis=1)
    gr, gi = fr, -fi
    g = np.block([[gr, gi], [-gi, gr]])
    h2 = np.concatenate([gr[: r // 2], gi[: r // 2]], axis=0) / n
    return dict(
        m1=jnp.asarray(m1, BF16), twr=jnp.asarray(np.cos(tw), F32), twi=jnp.asarray(np.sin(tw), F32),
        m2=jnp.asarray(m2, BF16), g=jnp.asarray(g, BF16), h2=jnp.asarray(h2, BF16))


CH_HALF = 64
CH_CHUNK = 32
CH_UNROLL = 16


def _dft_forward(zs, s1, m1_ref, twr, twi, m2_ref, half, n1_rows, emit):
    r = FFT_R

    def stage1(c, carry):
        zc = zs[pl.ds(pl.multiple_of((half * CH_HALF + c) * n1_rows, n1_rows), n1_rows), :].astype(BF16)
        a = jnp.dot(m1_ref[:, :n1_rows], zc, preferred_element_type=F32)
        ar, ai = a[:r], a[r:]
        s1[pl.ds(pl.multiple_of(c * 2 * r, 2 * r), r), :] = (ar * twr - ai * twi).astype(BF16)
        s1[pl.ds(pl.multiple_of(c * 2 * r + r, r), r), :] = (ar * twi + ai * twr).astype(BF16)
        return carry

    lax.fori_loop(0, CH_HALF, stage1, 0, unroll=CH_UNROLL)

    def stage2(j, carry):
        rows = CH_CHUNK * 2 * r
        lhs = s1[pl.ds(pl.multiple_of(j * rows, rows), rows), :]
        o = jnp.dot(lhs, m2_ref[...], preferred_element_type=F32).reshape(CH_CHUNK, 2 * r, 2 * r)
        xr = o[:, :r, :r] - o[:, r:, r:]
        xi = o[:, :r, r:] + o[:, r:, :r]
        emit(j, xr, xi)
        return carry

    lax.fori_loop(0, CH_HALF // CH_CHUNK, stage2, 0)


def _hyspec_body(k_ref, m1_ref, twr_ref, twi_ref, m2_ref, o_ref, zs, s1):
    r = FFT_R
    for n1 in range(r):
        blk = k_ref[n1 * r:(n1 + 1) * r, :]
        zs[pl.ds(n1, LANES, stride=r), :] = blk.T
    twr = twr_ref[...]
    twi = twi_ref[...]
    for half in range(LANES // CH_HALF):
        def emit(j, xr, xi, half=half):
            c0 = pl.multiple_of(half * CH_HALF + j * CH_CHUNK, CH_CHUNK)
            o_ref[pl.ds(c0, CH_CHUNK)] = jnp.concatenate([xr, xi], axis=-1).astype(o_ref.dtype)

        _dft_forward(zs, s1, m1_ref, twr, twi, m2_ref, half, r, emit)


def hy_spectrum(kt, consts):
    rows, c = kt.shape
    r = FFT_R
    full2 = lambda i: (0, 0)
    return pl.pallas_call(
        _hyspec_body,
        grid=(c // LANES,),
        in_specs=[
            pl.BlockSpec((rows, LANES), lambda i: (0, i)),
            pl.BlockSpec((2 * r, r), full2),
            pl.BlockSpec((r, r), full2),
            pl.BlockSpec((r, r), full2),
            pl.BlockSpec((r, 2 * r), full2),
        ],
        out_specs=pl.BlockSpec((LANES, r, 2 * r), lambda i: (i, 0, 0)),
        out_shape=jax.ShapeDtypeStruct((c, r, 2 * r), BF16),
        scratch_shapes=[pltpu.VMEM((LANES * r, r), F32), pltpu.VMEM((CH_HALF * 2 * r, r), BF16)],
        compiler_params=_cparams(("parallel",)),
        name="hy_spectrum",
    )(kt, consts["m1"], consts["twr"], consts["twi"], consts["m2"])


def _hyconv_body(z_ref, x0_ref, kf_ref, m1_ref, twr_ref, twi_ref, m2_ref, g_ref, h2_ref, bias_ref,
                 o_ref, zs, s1, s2):
    r = FFT_R
    n1_rows = r // 2
    for n1 in range(n1_rows):
        blk = z_ref[0, n1 * r:(n1 + 1) * r, :].astype(F32)
        zs[pl.ds(n1, LANES, stride=n1_rows), :] = blk.T
    twr = twr_ref[...]
    twi = twi_ref[...]
    for half in range(LANES // CH_HALF):
        def emit(j, xr, xi, half=half):
            c0 = pl.multiple_of(half * CH_HALF + j * CH_CHUNK, CH_CHUNK)
            kf = kf_ref[pl.ds(c0, CH_CHUNK)].astype(F32)
            kr, ki = kf[..., :r], kf[..., r:]
            y = jnp.concatenate([xr * kr - xi * ki, xr * ki + xi * kr], axis=-1).astype(BF16)
            rows = CH_CHUNK * r
            s2[pl.ds(pl.multiple_of(j * rows, rows), rows), :] = y.reshape(rows, 2 * r)

        _dft_forward(zs, s1, m1_ref, twr, twi, m2_ref, half, n1_rows, emit)

        def inv1(j, carry):
            rows = CH_CHUNK * r
            sl = pl.ds(pl.multiple_of(j * rows, rows), rows)
            bm = jnp.dot(s2[sl, :], g_ref[...], preferred_element_type=F32).reshape(CH_CHUNK, r, 2 * r)
            br, bi = bm[..., :r], bm[..., r:]
            b2 = jnp.concatenate([br * twr + bi * twi, bi * twr - br * twi], axis=-1).astype(BF16)
            s2[sl, :] = b2.reshape(rows, 2 * r)
            return carry

        lax.fori_loop(0, CH_HALF // CH_CHUNK, inv1, 0)

        def inv2(c, carry, half=half):
            q = jnp.dot(h2_ref[...], s2[pl.ds(pl.multiple_of(c * r, r), r), :], preferred_element_type=F32)
            yc = q[:n1_rows, :r] - q[n1_rows:, r:]
            zs[pl.ds(pl.multiple_of((half * CH_HALF + c) * n1_rows, n1_rows), n1_rows), :] = yc
            return carry

        lax.fori_loop(0, CH_HALF, inv2, 0, unroll=CH_UNROLL)

    bias = bias_ref[...]
    for n1 in range(n1_rows):
        y = zs[pl.ds(n1, LANES, stride=n1_rows), :].T
        sl = slice(n1 * r, (n1 + 1) * r)
        zz = z_ref[0, sl, :].astype(F32)
        x0 = x0_ref[0, sl, :].astype(F32)
        o_ref[0, sl, :] = ((y + zz * bias) * x0).astype(o_ref.dtype)


def hy_conv(z, x0c, kf, hy_bias, consts):
    b, l, c = z.shape
    r = FFT_R
    full2 = lambda i, j: (0, 0)
    return pl.pallas_call(
        _hyconv_body,
        grid=(c // LANES, b),
        in_specs=[
            pl.BlockSpec((1, l, LANES), lambda j, i: (i, 0, j)),
            pl.BlockSpec((1, l, LANES), lambda j, i: (i, 0, j)),
            pl.BlockSpec((LANES, r, 2 * r), lambda j, i: (j, 0, 0)),
            pl.BlockSpec((2 * r, r), full2),
            pl.BlockSpec((r, r), full2),
            pl.BlockSpec((r, r), full2),
            pl.BlockSpec((r, 2 * r), full2),
            pl.BlockSpec((2 * r, 2 * r), full2),
            pl.BlockSpec((r, r), full2),
            pl.BlockSpec((1, LANES), lambda j, i: (0, j)),
        ],
        out_specs=pl.BlockSpec((1, l, LANES), lambda j, i: (i, 0, j)),
        out_shape=jax.ShapeDtypeStruct((b, l, c), BF16),
        scratch_shapes=[
            pltpu.VMEM((LANES * r // 2, r), F32),
            pltpu.VMEM((CH_HALF * 2 * r, r), BF16),
            pltpu.VMEM((CH_HALF * r, 2 * r), BF16),
        ],
        compiler_params=_cparams(("parallel", "parallel")),
        name="hy_conv",
    )(z, x0c, kf, consts["m1"], consts["twr"], consts["twi"], consts["m2"], consts["g"], consts["h2"], hy_bias)


DEN_ROWS = 16


def _rope(x, c, s):
    return x * c + pltpu.roll(x, HEAD_DIM // 2, axis=1) * s


def _attn_body(q_ref, k_ref, v_ref, cos_ref, sin_ref, gq_ref, gk_ref, o_ref,
               kn_scr, vt_scr, qt_scr, sa_scr, sb_scr, m_scr, acc_scr, *, tq, tk, l):
    qi = pl.program_id(2)
    nk = l // tk

    @pl.when(qi == 0)
    def _():
        for r in range(nk):
            sl = slice(r * tk, (r + 1) * tk)
            kn = _rms(k_ref[0, sl, :].astype(F32), gk_ref[...])
            kn_scr[sl, :] = _rope(kn, cos_ref[sl, :], sin_ref[sl, :]).astype(BF16)
            vt_scr[r, :HEAD_DIM, :] = v_ref[0, sl, :].astype(F32).T.astype(BF16)
            vt_scr[r, HEAD_DIM:, :] = jnp.ones((DEN_ROWS, tk), BF16)

    row0 = pl.multiple_of(qi * tq, tq)
    c = cos_ref[pl.ds(row0, tq), :]
    s = sin_ref[pl.ds(row0, tq), :]
    scale = HEAD_DIM ** -0.5 * math.log2(math.e)
    for g in range(GROUP):
        qn = _rms(q_ref[0, :, g * HEAD_DIM:(g + 1) * HEAD_DIM].astype(F32), gq_ref[...])
        qt_scr[:, g * tq:(g + 1) * tq] = (_rope(qn, c, s) * scale).T.astype(BF16)

    m_scr[...] = jnp.full(m_scr.shape, -jnp.inf, F32)
    acc_scr[...] = jnp.zeros(acc_scr.shape, F32)
    nq = GROUP * tq

    def scores(j):
        kc = kn_scr[pl.ds(pl.multiple_of(j * tk, tk), tk), :]
        return jnp.dot(kc, qt_scr[...], preferred_element_type=F32)

    sa_scr[...] = scores(0)

    def half_step(j, cur, nxt):
        nxt[...] = scores(jnp.minimum(j + 1, nk - 1))
        st = cur[...]
        m_prev = m_scr[...]
        m_new = jnp.maximum(m_prev, jnp.max(st, axis=0, keepdims=True))
        alpha = jnp.exp2(m_prev - m_new)
        p = jnp.exp2(st - m_new)
        acc_scr[...] = alpha * acc_scr[...] + jnp.dot(vt_scr[j], p.astype(BF16), preferred_element_type=F32)
        m_scr[...] = m_new

    def kv_pair(jj, carry):
        half_step(2 * jj, sa_scr, sb_scr)
        half_step(2 * jj + 1, sb_scr, sa_scr)
        return carry

    lax.fori_loop(0, nk // 2, kv_pair, 0)
    o = acc_scr[:HEAD_DIM, :] / acc_scr[HEAD_DIM:HEAD_DIM + 1, :]
    for g in range(GROUP):
        o_ref[0, :, g * HEAD_DIM:(g + 1) * HEAD_DIM] = o[:, g * tq:(g + 1) * tq].T.astype(o_ref.dtype)


def attention(proj3, cos2, sin2, g_q, g_k, q_col, k_col, v_col, n_kv, tq=256, tk=1024):
    b, l, _ = proj3.shape
    gw = GROUP * HEAD_DIM
    body = functools.partial(_attn_body, tq=tq, tk=tk, l=l)
    qb, kb, vb = q_col // gw, k_col // HEAD_DIM, v_col // HEAD_DIM
    return pl.pallas_call(
        body,
        grid=(b, n_kv, l // tq),
        in_specs=[
            pl.BlockSpec((1, tq, gw), lambda i, h, q: (i, q, qb + h)),
            pl.BlockSpec((1, l, HEAD_DIM), lambda i, h, q: (i, 0, kb + h)),
            pl.BlockSpec((1, l, HEAD_DIM), lambda i, h, q: (i, 0, vb + h)),
            pl.BlockSpec((l, HEAD_DIM), lambda i, h, q: (0, 0)),
            pl.BlockSpec((l, HEAD_DIM), lambda i, h, q: (0, 0)),
            pl.BlockSpec((1, HEAD_DIM), lambda i, h, q: (0, 0)),
            pl.BlockSpec((1, HEAD_DIM), lambda i, h, q: (0, 0)),
        ],
        out_specs=pl.BlockSpec((1, tq, gw), lambda i, h, q: (i, q, h)),
        out_shape=jax.ShapeDtypeStruct((b, l, n_kv * gw), BF16),
        scratch_shapes=[
            pltpu.VMEM((l, HEAD_DIM), BF16),
            pltpu.VMEM((l // tk, HEAD_DIM + DEN_ROWS, tk), BF16),
            pltpu.VMEM((HEAD_DIM, GROUP * tq), BF16),
            pltpu.VMEM((tk, GROUP * tq), F32),
            pltpu.VMEM((tk, GROUP * tq), F32),
            pltpu.VMEM((1, GROUP * tq), F32),
            pltpu.VMEM((HEAD_DIM + DEN_ROWS, GROUP * tq), F32),
        ],
        compiler_params=_cparams(("parallel", "parallel", "arbitrary")),
        name="attention",
    )(proj3, proj3, proj3, cos2, sin2, g_q, g_k)


def _merge_body(yh_ref, ya_ref, gh_ref, ga_ref, ha_ref, hb_ref, whb_ref, wab_ref, wo_ref, gf_ref,
                wrh_ref, wrl_ref, br_ref, h1_ref, xn_ref, meta_ref, slott_ref, tw_ref, cnt_ref, *, tm, nba):
    a = jnp.dot(yh_ref[...], whb_ref[...], preferred_element_type=F32)
    bmat = jnp.dot(ya_ref[...], wab_ref[...], preferred_element_type=F32)
    mix = jax.nn.sigmoid(gh_ref[...].astype(F32)) * a + jax.nn.sigmoid(ga_ref[...].astype(F32)) * bmat
    h0 = jnp.where(pl.program_id(0) < nba, ha_ref[...], hb_ref[...])
    h1 = h0 + jnp.dot(mix.astype(BF16), wo_ref[...], preferred_element_type=F32)
    h1_ref[...] = h1
    xn = _rms(h1, gf_ref[...])
    xn_ref[...] = xn.astype(xn_ref.dtype)

    x_hi = xn.astype(BF16)
    x_lo = (xn - x_hi.astype(F32)).astype(BF16)
    logits = (jnp.dot(x_hi, wrh_ref[...], preferred_element_type=F32)
              + jnp.dot(x_lo, wrh_ref[...], preferred_element_type=F32)
              + jnp.dot(x_hi, wrl_ref[...], preferred_element_type=F32)) + br_ref[...]
    lt = logits.T
    row = lax.broadcasted_iota(I32, lt.shape, 0)
    work = lt
    vals, idxs = [], []
    sel = jnp.zeros(lt.shape, F32)
    for _ in range(TOP_K):
        m = jnp.max(work, axis=0, keepdims=True)
        idx = jnp.min(jnp.where(work == m, row, LANES), axis=0, keepdims=True)
        hit = row == idx
        vals.append(m)
        idxs.append(idx)
        sel = sel + hit.astype(F32)
        work = jnp.where(hit, -jnp.inf, work)
    ex = [jnp.exp(v - vals[0]) for v in vals]
    den = ex[0] + ex[1] + ex[2] + ex[3]

    rr = lax.broadcasted_iota(I32, (tm, tm), 0)
    cc = lax.broadcasted_iota(I32, (tm, tm), 1)
    before = (rr < cc).astype(BF16)
    prefix = jnp.dot(sel.astype(BF16), before, preferred_element_type=F32)
    units = jnp.floor((jnp.sum(sel, axis=1, keepdims=True) + (RUN_ALIGN - 1)) * (1.0 / RUN_ALIGN))
    units_b = jnp.broadcast_to(units, (LANES, LANES))
    cnt_ref[...] = (units_b.T * RUN_ALIGN)[0:8, :]
    er = lax.broadcasted_iota(I32, (LANES, LANES), 0)
    ec = lax.broadcasted_iota(I32, (LANES, LANES), 1)
    lower = (ec < er).astype(BF16)
    eoff = RUN_ALIGN * jnp.dot(lower, units_b.astype(BF16), preferred_element_type=F32)[:, 0:1]
    slot_of = prefix + eoff

    meta_t = jnp.full(lt.shape, -1.0, F32)
    tw_t = jnp.zeros(lt.shape, F32)
    for k in range(TOP_K):
        slot = jnp.sum(jnp.where(row == idxs[k], slot_of, 0.0), axis=0, keepdims=True)
        meta_t = jnp.where(row == k, slot, meta_t)
        tw_t = jnp.where(row == k, ex[k] / den, tw_t)
    slott_ref[0] = meta_t[0:8, :].astype(I32)
    meta_ref[...] = meta_t.T.astype(I32)
    tw_ref[...] = tw_t.T


def merge_router(y_hy, y_at, proj, xa, xb, whb, wab, wo, g_ffn, wr_hi, wr_lo, br_pad, gate_col, tm):
    d = xa.shape[1]
    n = xa.shape[0] + xb.shape[0]
    nba = xa.shape[0] // tm
    ch = y_hy.shape[1]
    ca = y_at.shape[1]
    gb = gate_col // d
    body = functools.partial(_merge_body, tm=tm, nba=nba)
    full = lambda i: (0, 0)
    row = lambda i: (i, 0)
    return pl.pallas_call(
        body,
        grid=(n // tm,),
        in_specs=[
            pl.BlockSpec((tm, ch), row),
            pl.BlockSpec((tm, ca), row),
            pl.BlockSpec((tm, d), lambda i: (i, gb)),
            pl.BlockSpec((tm, d), lambda i: (i, gb + 1)),
        ] + _pair_specs(tm, d, nba) + [
            pl.BlockSpec((ch, d), full),
            pl.BlockSpec((ca, d), full),
            pl.BlockSpec((d, d), full),
            pl.BlockSpec((1, d), full),
            pl.BlockSpec((d, LANES), full),
            pl.BlockSpec((d, LANES), full),
            pl.BlockSpec((1, LANES), full),
        ],
        out_specs=[
            pl.BlockSpec((tm, d), row),
            pl.BlockSpec((tm, d), row),
            pl.BlockSpec((tm, LANES), row),
            pl.BlockSpec((1, 8, tm), lambda i: (i, 0, 0)),
            pl.BlockSpec((tm, LANES), row),
            pl.BlockSpec((8, LANES), row),
        ],
        out_shape=[
            jax.ShapeDtypeStruct((n, d), F32),
            jax.ShapeDtypeStruct((n, d), BF16),
            jax.ShapeDtypeStruct((n, LANES), I32),
            jax.ShapeDtypeStruct((n // tm, 8, tm), I32),
            jax.ShapeDtypeStruct((n, LANES), F32),
            jax.ShapeDtypeStruct((n // tm * 8, LANES), F32),
        ],
        compiler_params=_cparams(("parallel",)),
        name="merge_router",
    )(y_hy, y_at, proj, proj, xa, xb, whb, wab, wo, g_ffn, wr_hi, wr_lo, br_pad)


RUN_ALIGN = 8
RUN_BITS = ROUTE_BLOCK.bit_length()
SORT_ROWS = TOP_K * ROUTE_BLOCK + N_EXPERTS * RUN_ALIGN


def _run_copies(i, cnt_ref, dst_ref, make_copy, op):
    def per_expert(e, local):
        c = cnt_ref[i * N_EXPERTS + e]
        d = dst_ref[i * N_EXPERTS + e]
        for bit in range(RUN_ALIGN.bit_length() - 1, RUN_BITS):
            size = 1 << bit

            @pl.when((c & size) != 0)
            def _(size=size, queue=bit % 2):
                done = c & (size - 1)
                op(make_copy(pl.multiple_of(local + done, RUN_ALIGN), pl.multiple_of(d + done, RUN_ALIGN), size),
                   queue)

        return local + c

    lax.fori_loop(0, N_EXPERTS, per_expert, 0)


def _pack_bf16_pairs(x):
    half = x.shape[1] // 2
    hi = lax.bitcast_convert_type(x[:, :half], U32)
    lo = lax.bitcast_convert_type(x[:, half:], U32)
    return hi | (lo >> 16)


def _unpack_bf16_pairs(w):
    hi = lax.bitcast_convert_type(w & jnp.uint32(0xFFFF0000), F32).astype(BF16)
    lo = lax.bitcast_convert_type(w << 16, F32).astype(BF16)
    return jnp.concatenate([hi, lo], axis=1)


def _dispatch_body(cnt_ref, dst_ref, pend_ref, padded_ref, nused_ref, slot_ref, xn_ref, xs_hbm,
                   sort_scr, zero_scr, sems, zsem, *, tm, nblk):
    i = pl.program_id(0)
    last = pl.num_programs(0) - 1
    buf = i % 2

    @pl.when(i == 0)
    def _():
        zero_scr[...] = jnp.zeros(zero_scr.shape, U32)

        def zero_block(start):
            cp = pltpu.make_async_copy(zero_scr, xs_hbm.at[pl.ds(start, EXPERT_BLOCK), :], zsem)
            cp.start()
            cp.wait()

        def pad_rows(e, carry):
            @pl.when(padded_ref[e] > 0)
            def _():
                zero_block(pl.multiple_of(pend_ref[e] - EXPERT_BLOCK, EXPERT_BLOCK))

            return carry

        lax.fori_loop(0, N_EXPERTS, pad_rows, 0)

        def tail(bk, carry):
            zero_block(pl.multiple_of(bk * EXPERT_BLOCK, EXPERT_BLOCK))
            return carry

        lax.fori_loop(nused_ref[0], nblk, tail, 0)

    slots = slot_ref[0]
    j = lax.broadcasted_iota(I32, (SORT_ROWS, tm), 0)
    perm = jnp.zeros((SORT_ROWS, tm), F32)
    for k in range(TOP_K):
        perm = perm + (j == slots[k:k + 1, :]).astype(F32)
    srt = jnp.dot(perm.astype(BF16), xn_ref[...], preferred_element_type=F32)
    sort_scr[buf] = _pack_bf16_pairs(srt)

    def copier(b):
        def make_copy(local, d, size):
            return pltpu.make_async_copy(sort_scr.at[b, pl.ds(local, size), :], xs_hbm.at[pl.ds(d, size), :],
                                         sems.at[b])
        return make_copy

    _run_copies(i, cnt_ref, dst_ref, copier(buf), lambda cp, queue: cp.start(priority=queue))

    @pl.when(i > 0)
    def _():
        _run_copies(i - 1, cnt_ref, dst_ref, copier(1 - buf), lambda cp, queue: cp.wait())

    @pl.when(i == last)
    def _():
        _run_copies(i, cnt_ref, dst_ref, copier(buf), lambda cp, queue: cp.wait())


def dispatch(xn, slots_t, cnt_flat, dst_flat, pad_end, padded, nused, cap, tm):
    n, d = xn.shape
    nblk = cap // EXPERT_BLOCK
    body = functools.partial(_dispatch_body, tm=tm, nblk=nblk)
    return pl.pallas_call(
        body,
        grid_spec=pltpu.PrefetchScalarGridSpec(
            num_scalar_prefetch=5,
            grid=(n // tm,),
            in_specs=[
                pl.BlockSpec((1, 8, tm), lambda i, *_: (i, 0, 0)),
                pl.BlockSpec((tm, d), lambda i, *_: (i, 0)),
            ],
            out_specs=pl.BlockSpec(memory_space=pl.ANY),
            scratch_shapes=[
                pltpu.VMEM((2, SORT_ROWS, d // 2), U32),
                pltpu.VMEM((EXPERT_BLOCK, d // 2), U32),
                pltpu.SemaphoreType.DMA((2,)),
                pltpu.SemaphoreType.DMA(()),
            ],
        ),
        out_shape=jax.ShapeDtypeStruct((cap, d // 2), U32),
        compiler_params=_cparams(("arbitrary",)),
        name="dispatch",
    )(cnt_flat, dst_flat, pad_end, padded, nused, slots_t, xn)


def _experts_body(be_ref, nu_ref, x_ref, wg_ref, bg_ref, wu_ref, bu_ref, wd_ref, bd_ref, o_ref):
    i = pl.program_id(0)
    f = pl.program_id(1)
    used = i < nu_ref[0]

    def down_proj():
        xb = _unpack_bf16_pairs(x_ref[...])
        g = jnp.dot(xb, wg_ref[0], preferred_element_type=F32) + bg_ref[0]
        u = jnp.dot(xb, wu_ref[0], preferred_element_type=F32) + bu_ref[0]
        g = jnp.minimum(g, SWIGLU_LIMIT)
        u = jnp.clip(u, -SWIGLU_LIMIT, SWIGLU_LIMIT)
        a = g * jax.nn.sigmoid(SWIGLU_ALPHA * g) * (u + 1.0)
        return jnp.dot(a.astype(BF16), wd_ref[0], preferred_element_type=F32)

    @pl.when(jnp.logical_and(used, f == 0))
    def _():
        o_ref[...] = down_proj() + bd_ref[0]

    @pl.when(jnp.logical_and(used, f > 0))
    def _():
        o_ref[...] = o_ref[...] + down_proj()

    @pl.when(jnp.logical_not(used))
    def _():
        o_ref[...] = jnp.zeros(o_ref.shape, F32)


def experts(xs, blk_e, nused, wg, bg, wu, bu, wd, bd, tf=1024):
    cap = xs.shape[0]
    d, dff = wg.shape[1], wg.shape[2]
    nblk = cap // EXPERT_BLOCK
    nf = dff // tf

    def last_used(i, nu):
        return jnp.maximum(jnp.minimum(i, nu[0] - 1), 0)

    def eidx(i, be, nu):
        return be[last_used(i, nu)]

    def fidx(i, f, nu):
        return jnp.where(i < nu[0], f, nf - 1)

    return pl.pallas_call(
        _experts_body,
        grid_spec=pltpu.PrefetchScalarGridSpec(
            num_scalar_prefetch=2,
            grid=(nblk, nf),
            in_specs=[
                pl.BlockSpec((EXPERT_BLOCK, d // 2), lambda i, f, be, nu: (last_used(i, nu), 0)),
                pl.BlockSpec((1, d, tf), lambda i, f, be, nu: (eidx(i, be, nu), 0, fidx(i, f, nu))),
                pl.BlockSpec((1, 1, tf), lambda i, f, be, nu: (eidx(i, be, nu), 0, fidx(i, f, nu))),
                pl.BlockSpec((1, d, tf), lambda i, f, be, nu: (eidx(i, be, nu), 0, fidx(i, f, nu))),
                pl.BlockSpec((1, 1, tf), lambda i, f, be, nu: (eidx(i, be, nu), 0, fidx(i, f, nu))),
                pl.BlockSpec((1, tf, d), lambda i, f, be, nu: (eidx(i, be, nu), fidx(i, f, nu), 0)),
                pl.BlockSpec((1, 1, d), lambda i, f, be, nu: (eidx(i, be, nu), 0, 0)),
            ],
            out_specs=pl.BlockSpec((EXPERT_BLOCK, d), lambda i, f, be, nu: (i, 0)),
        ),
        out_shape=jax.ShapeDtypeStruct((cap, d), F32),
        compiler_params=_cparams(("arbitrary", "arbitrary")),
        name="experts",
    )(blk_e, nused, xs, wg, bg, wu, bu, wd, bd)


def _combine_body(cnt_ref, dst_ref, slot_ref, h_ref, tw_ref, pa_ref, pb_ref, y_hbm, wpg_ref, wpp_ref, gp_ref,
                  gfin_ref, oa_ref, ob_ref, gath, sems, *, tm, nba):
    i = pl.program_id(0)
    last = pl.num_programs(0) - 1
    buf = i % 2

    def copier(b):
        def make_copy(local, d, size):
            return pltpu.make_async_copy(y_hbm.at[pl.ds(d, size), :], gath.at[b, pl.ds(local, size), :],
                                         sems.at[b])
        return make_copy

    def fetch(step, b):
        gath[b, TOP_K * tm:, :] = jnp.zeros((SORT_ROWS - TOP_K * tm, gath.shape[2]), F32)
        _run_copies(step, cnt_ref, dst_ref, copier(b), lambda cp, queue: cp.start(priority=queue))

    @pl.when(i == 0)
    def _():
        fetch(0, 0)

    @pl.when(i < last)
    def _():
        fetch(i + 1, 1 - buf)

    _run_copies(i, cnt_ref, dst_ref, copier(buf), lambda cp, queue: cp.wait())

    tw = tw_ref[...]
    slots = slot_ref[...]
    lane = lax.broadcasted_iota(I32, (tm, SORT_ROWS), 1)
    wmat = jnp.zeros((tm, SORT_ROWS), F32)
    for k in range(TOP_K):
        wmat = wmat + jnp.where(lane == slots[:, k:k + 1], tw[:, k:k + 1], 0.0)
    moe = jnp.dot(wmat.astype(BF16), gath[buf].astype(BF16), preferred_element_type=F32)
    h2 = h_ref[...] + moe
    a = _rms(h2, gp_ref[...]).astype(BF16)
    gate = jax.nn.sigmoid(jnp.dot(a, wpg_ref[...], preferred_element_type=F32))
    pblk = jnp.where(i < nba, pa_ref[...], pb_ref[...])
    pp = jnp.dot(pblk.astype(BF16), wpp_ref[...], preferred_element_type=F32)
    h3 = h2 + gate * pp
    res = _rms(h3, gfin_ref[...])

    @pl.when(i < nba)
    def _():
        oa_ref[...] = res

    @pl.when(i >= nba)
    def _():
        ob_ref[...] = res


def combine_ple(h1, slots, tw, pa, pb, y, cnt_flat, dst_flat, wpg, wpp, g_ple, g_final, tm):
    n, d = h1.shape
    pd = pa.shape[1]
    nba = pa.shape[0] // tm
    body = functools.partial(_combine_body, tm=tm, nba=nba)
    full = lambda i, *_: (0, 0)
    row = lambda i, *_: (i, 0)
    return pl.pallas_call(
        body,
        grid_spec=pltpu.PrefetchScalarGridSpec(
            num_scalar_prefetch=2,
            grid=(n // tm,),
            in_specs=[
                pl.BlockSpec((tm, LANES), row),
                pl.BlockSpec((tm, d), row),
                pl.BlockSpec((tm, LANES), row),
            ] + _pair_specs(tm, pd, nba) + [
                pl.BlockSpec(memory_space=pl.ANY),
                pl.BlockSpec((d, d), full, pipeline_mode=pl.Buffered(1)),
                pl.BlockSpec((pd, d), full, pipeline_mode=pl.Buffered(1)),
                pl.BlockSpec((1, d), full),
                pl.BlockSpec((1, d), full),
            ],
            out_specs=_pair_specs(tm, d, nba),
            scratch_shapes=[pltpu.VMEM((2, SORT_ROWS, d), F32), pltpu.SemaphoreType.DMA((2,))],
        ),
        out_shape=[jax.ShapeDtypeStruct((pa.shape[0], d), F32), jax.ShapeDtypeStruct((pb.shape[0], d), F32)],
        compiler_params=_cparams(("arbitrary",)),
        name="combine_ple",
    )(cnt_flat, dst_flat, slots, h1, tw, pa, pb, y, wpg, wpp, g_ple, g_final)


def _rope_tables(l):
    rows = l // GRID_W
    row = jnp.repeat(jnp.arange(rows, dtype=F32), GRID_W)
    col = jnp.tile(jnp.arange(GRID_W, dtype=F32), rows)
    axis_dim = HEAD_DIM // 2
    inv = 1.0 / (ROPE_THETA ** (jnp.arange(0, axis_dim, 2, dtype=F32) / axis_dim))
    ang = jnp.concatenate([row[:, None] * inv, col[:, None] * inv], axis=-1)
    c, s = jnp.cos(ang), jnp.sin(ang)
    return jnp.concatenate([c, c], axis=-1), jnp.concatenate([-s, s], axis=-1)


def _filter_features(l):
    t = jnp.linspace(0.0, 1.0, l, dtype=F32)[:, None]
    w = 2.0 * math.pi * jnp.arange(l, dtype=F32)[:, None] / l
    f = jnp.linspace(1e-4, FILTER_BANDS - 1, FILTER_BANDS, dtype=F32)[None, :]
    z = jnp.concatenate([t, jnp.cos(f * w), -jnp.sin(f * w)], axis=-1)
    pos = jnp.concatenate([jnp.arange(l), jnp.zeros((1,), jnp.int32), jnp.arange(l - 1, 0, -1)])
    mask = jnp.ones((2 * l, 1), F32).at[l, 0].set(0.0)
    z2 = jnp.concatenate([z[pos], mask], axis=-1)
    return jnp.pad(z2, ((0, 0), (0, LANES - z2.shape[1])))


def kernel(x_prompt, x_sample, p_prompt, p_sample, g_mix, w_in, w_short, b_short, w_f1, b_f1, w_f2, b_f2,
           w_f3, b_f3, w_f4, filter_freq, hy_bias, g_q, g_k, w_hy_br, w_at_br, w_out, g_ffn, w_router,
           b_router, w_gate, b_gate, w_up, b_up, w_down, b_down, g_ple, w_ple_gate, w_ple_proj, g_final):
    assert w_in.shape[0] == 1, "single layer"
    l, d = x_prompt.shape[1], x_prompt.shape[2]
    c = w_hy_br.shape[1]
    aw = w_at_br.shape[1]
    n_kv = aw // (GROUP * HEAD_DIM)
    kvw = n_kv * HEAD_DIM
    assert 2 * l == FFT_R * FFT_R and x_sample.shape[1] == l

    ba, bb = x_prompt.shape[0], x_sample.shape[0]
    b = ba + bb
    n = b * l
    xa, xb = x_prompt.reshape(ba * l, d), x_sample.reshape(bb * l, d)
    pa, pb = p_prompt[0].reshape(ba * l, -1), p_sample[0].reshape(bb * l, -1)

    o_q, o_k, o_v, o_g = 3 * c, 3 * c + aw, 3 * c + aw + kvw, 3 * c + aw + 2 * kvw
    wi = w_in[0]
    w_perm = jnp.concatenate([wi[:, :o_k], wi[:, o_g:], wi[:, o_k:o_g]], axis=1).astype(BF16)
    w_perm = jnp.pad(w_perm, ((0, 0), (0, -w_perm.shape[1] % INPROJ_TN)))
    q_col, gate_col = o_q, o_k
    k_col = gate_col + 2 * d
    v_col = k_col + kvw

    proj = in_proj(xa, xb, g_mix, w_perm, tn=INPROJ_TN)
    proj3 = proj.reshape(b, l, proj.shape[1])

    consts = _dft_constants()
    z, x0c = hy_prep(proj3, w_short[0], b_short, c)
    w1p = jnp.pad(w_f1[0], ((0, LANES - FILTER_EMB), (0, 0)))
    deltas = jnp.linspace(math.log(DECAY_TARGET) / DECAY_SLOW, math.log(DECAY_TARGET) / DECAY_FAST, c, dtype=F32)
    kt = hy_filter(_filter_features(l), w1p, b_f1, w_f2[0], b_f2, w_f3[0], b_f3, filter_freq, w_f4[0],
                   jnp.abs(deltas)[None, :], c)
    kf = hy_spectrum(kt, consts)
    y_hy = hy_conv(z, x0c, kf, hy_bias, consts)

    cos2, sin2 = _rope_tables(l)
    y_at = attention(proj3, cos2, sin2, g_q, g_k, q_col, k_col, v_col, n_kv)

    oa, ob = _merge_moe_ple(y_hy.reshape(n, c), y_at.reshape(n, aw), proj, gate_col, xa, xb, pa, pb,
                            w_hy_br[0], w_at_br[0], w_out[0], g_ffn, w_router[0], b_router, w_gate[0], b_gate[0],
                            w_up[0], b_up[0], w_down[0], b_down[0], g_ple, w_ple_gate[0], w_ple_proj[0], g_final)
    return oa.reshape(ba, l, d), ob.reshape(bb, l, d)


def _merge_moe_ple(y_hy, y_at, proj, gate_col, xa, xb, pa, pb, w_hy_br, w_at_br, w_out, g_ffn, w_router, b_router,
                   w_gate, b_gate, w_up, b_up, w_down, b_down, g_ple, w_ple_gate, w_ple_proj, g_final):
    n = xa.shape[0] + xb.shape[0]
    wr_pad = jnp.pad(w_router, ((0, 0), (0, LANES - N_EXPERTS)))
    wr_hi = wr_pad.astype(BF16)
    wr_lo = (wr_pad - wr_hi.astype(F32)).astype(BF16)
    br_pad = jnp.pad(b_router, ((0, 0), (0, LANES - N_EXPERTS)), constant_values=-1e30)
    h1, xn, slots, slots_t, tw, cnt = merge_router(
        y_hy, y_at, proj, xa, xb, w_hy_br.astype(BF16), w_at_br.astype(BF16),
        w_out.astype(BF16), g_ffn, wr_hi, wr_lo, br_pad, gate_col, ROUTE_BLOCK)

    nrb = n // ROUTE_BLOCK
    cnt_be = cnt.reshape(nrb, 8, LANES)[:, 0, :N_EXPERTS].astype(I32)
    counts = jnp.sum(cnt_be, axis=0)
    padded = (counts + EXPERT_BLOCK - 1) // EXPERT_BLOCK * EXPERT_BLOCK
    pad_end = jnp.cumsum(padded).astype(I32)
    pad_start = pad_end - padded
    before = jnp.cumsum(cnt_be, axis=0) - cnt_be
    dst_be = pad_start[None, :] + before
    cap = nrb * SORT_ROWS + N_EXPERTS * EXPERT_BLOCK
    nblk = cap // EXPERT_BLOCK
    nused = (pad_end[-1:] // EXPERT_BLOCK).astype(I32)
    blk_start = jnp.arange(nblk, dtype=I32) * EXPERT_BLOCK
    blk_e = jnp.minimum(jnp.sum((pad_end[None, :] <= blk_start[:, None]).astype(I32), axis=1), N_EXPERTS - 1)
    cnt_flat = cnt_be.reshape(-1)
    dst_flat = dst_be.reshape(-1).astype(I32)

    xs = dispatch(xn, slots_t, cnt_flat, dst_flat, pad_end, padded, nused, cap, ROUTE_BLOCK)
    ys = experts(xs, blk_e, nused, w_gate.astype(BF16), b_gate[:, None, :], w_up.astype(BF16),
                 b_up[:, None, :], w_down.astype(BF16), b_down[:, None, :])
    return combine_ple(h1, slots, tw, pa, pb, ys, cnt_flat, dst_flat, w_ple_gate.astype(BF16),
                       w_ple_proj.astype(BF16), g_ple, g_final[None, :], ROUTE_BLOCK)
```
